```python
import math
import numpy as np
import jax
import jax.numpy as jnp
from jax import lax

D_MODEL = 1024
BATCH = 16
SEQ = 256
DEPTH = 4
DEC_BATCH = 2
DEC_SEQ = 4096
PAST_LEN = 256

GRID_W = 64
N_BRANCH = 4
BR_W = 256
MLA_H = 4
MLA_NOPE = 64
MLA_ROPE = 32
MLA_V = 64
MLA_Q_LORA = 256
MLA_KV_LORA = 128
GDN_H = 4
GDN_DK = 64
GDN_DV = 64
GDN_CHUNK = 64
SHORT_K = 3
HY_W = 256
HY_BANDS = 16
HY_EMB = 1 + 2 * HY_BANDS
HY_FH = 64
HY_SLOW_DECAY = math.log(1e-2) / 1.5
HY_FAST_DECAY = math.log(1e-2) / 0.3
DA_H = 4
DA_DK = 32
DA_DV = 64
D_FF = 4 * D_MODEL
ROPE_BASE = 10000.0
Q_BLOCK = 128
EPS = 1e-6
IN_SPLITS = (MLA_Q_LORA, MLA_KV_LORA, MLA_ROPE,
             GDN_H * GDN_DK, GDN_H * GDN_DK, GDN_H * GDN_DV, GDN_H * GDN_DV, 2 * GDN_H, 2 * GDN_H,
             3 * HY_W,
             DA_H * 2 * DA_DK, DA_H * 2 * DA_DK, DA_H * DA_DV,
             N_BRANCH * D_MODEL)
IN_W = (MLA_Q_LORA + MLA_KV_LORA + MLA_ROPE + 2 * GDN_H * GDN_DK + 2 * GDN_H * GDN_DV + 4 * GDN_H
        + 3 * HY_W + 4 * DA_H * DA_DK + DA_H * DA_DV + N_BRANCH * D_MODEL)

kernel_name = 'hybrid_diffusion_prefix_trunk'


def _rmsnorm(x, w):
    xf = x.astype(jnp.float32)
    y = xf * lax.rsqrt(jnp.mean(xf * xf, axis=-1, keepdims=True) + EPS)
    return (y * w.astype(jnp.float32)).astype(x.dtype)


def _l2norm(x):
    xf = x.astype(jnp.float32)
    return xf * lax.rsqrt(jnp.sum(xf * xf, axis=-1, keepdims=True) + EPS)


def _dwconv_centred(x, w):
    seq = x.shape[1]
    pad = w.shape[0] // 2
    xp = jnp.pad(x, ((0, 0), (pad, pad), (0, 0)))
    out = xp[:, :seq] * w[0]
    for i in range(1, w.shape[0]):
        out = out + xp[:, i:i + seq] * w[i]
    return out


def _axial_rope_angles(seq, rot_dim):
    rows = seq // GRID_W
    m = rot_dim // 4
    inv = ROPE_BASE ** (-jnp.arange(m, dtype=jnp.float32) / m)
    row_pos = jnp.repeat(jnp.arange(rows, dtype=jnp.float32), GRID_W)
    col_pos = jnp.tile(jnp.arange(GRID_W, dtype=jnp.float32), rows)
    return row_pos[:, None] * inv, col_pos[:, None] * inv


def _rope_half(x, ang):
    shape = (ang.shape[0],) + (1,) * (x.ndim - 3) + (ang.shape[1],)
    cos = jnp.cos(ang).reshape(shape)
    sin = jnp.sin(ang).reshape(shape)
    x1, x2 = jnp.split(x.astype(jnp.float32), 2, axis=-1)
    return jnp.concatenate([x1 * cos - x2 * sin, x2 * cos + x1 * sin], axis=-1)


def _apply_axial_rope(x, ang_row, ang_col):
    xr, xc = jnp.split(x, 2, axis=-1)
    return jnp.concatenate([_rope_half(xr, ang_row), _rope_half(xc, ang_col)], axis=-1).astype(x.dtype)


def _map_query_blocks(fn, q):
    bsz, lq = q.shape[:2]
    nb = lq // Q_BLOCK
    qb = jnp.moveaxis(q.reshape((bsz, nb, Q_BLOCK) + q.shape[2:]), 1, 0)
    out = lax.map(fn, qb)
    return jnp.moveaxis(out, 0, 1).reshape((bsz, lq) + out.shape[3:])


def _mla_expand(ckv, kpe, w_ukv):
    bsz, seq = ckv.shape[:2]
    kv = (ckv @ w_ukv).reshape(bsz, seq, MLA_H, MLA_NOPE + MLA_V)
    k = jnp.concatenate([kv[..., :MLA_NOPE], jnp.broadcast_to(kpe[:, :, None, :], (bsz, seq, MLA_H, MLA_ROPE))], axis=-1)
    return k, kv[..., MLA_NOPE:]


def _mla_attend(q, k, v):
    scale = (MLA_NOPE + MLA_ROPE) ** -0.5

    def block(qb):
        s = jnp.einsum('bqhd,bkhd->bhqk', qb, k).astype(jnp.float32) * scale
        p = jax.nn.softmax(s, axis=-1).astype(v.dtype)
        return jnp.einsum('bhqk,bkhd->bqhd', p, v)

    return _map_query_blocks(block, q)


def _diff_attend(q, k, v, lam):
    scale = DA_DK ** -0.5
    k1, k2 = k[..., 0, :], k[..., 1, :]

    def block(qb):
        s1 = jnp.einsum('bqhd,bkhd->bhqk', qb[..., 0, :], k1).astype(jnp.float32) * scale
        s2 = jnp.einsum('bqhd,bkhd->bhqk', qb[..., 1, :], k2).astype(jnp.float32) * scale
        p = jax.nn.softmax(s1, axis=-1) - lam * jax.nn.softmax(s2, axis=-1)
        return jnp.einsum('bhqk,bkhd->bqhd', p.astype(v.dtype), v)

    return _map_query_blocks(block, q)


def _gdn_chunked(q, k, v, g, beta, s0):
    f32 = jnp.float32
    bsz, seq, nh, dk = q.shape
    dv = v.shape[-1]
    n = seq // GDN_CHUNK

    def chunks(t):
        return t.astype(f32).reshape(bsz, n, GDN_CHUNK, nh, -1).transpose(1, 0, 3, 2, 4)

    qc = chunks(q) * (dk ** -0.5)
    kc = chunks(k)
    vc = chunks(v)
    gc = jnp.cumsum(chunks(g[..., None])[..., 0], axis=-1)
    bc = chunks(beta[..., None])
    incl = jnp.tril(jnp.ones((GDN_CHUNK, GDN_CHUNK), dtype=bool))
    strict = jnp.tril(jnp.ones((GDN_CHUNK, GDN_CHUNK), dtype=bool), k=-1)
    gdiff = gc[..., :, None] - gc[..., None, :]
    decay = jnp.where(incl, jnp.exp(jnp.where(incl, gdiff, 0.0)), 0.0)
    kb = kc * bc
    lower = jnp.where(strict, jnp.einsum('nbhid,nbhjd->nbhij', kb, kc) * decay, 0.0)
    a_mat = lower + jnp.eye(GDN_CHUNK, dtype=f32)
    rhs = jnp.concatenate([vc * bc, kb * jnp.exp(gc)[..., None]], axis=-1)
    sol = lax.linalg.triangular_solve(a_mat, rhs, left_side=True, lower=True, unit_diagonal=True)
    u, w = sol[..., :dv], sol[..., dv:]
    intra = jnp.where(incl, jnp.einsum('nbhid,nbhjd->nbhij', qc, kc) * decay, 0.0)

    def step(state, xs):
        q_i, k_i, u_i, w_i, g_i, a_i = xs
        v_new = u_i - jnp.einsum('bhck,bhkv->bhcv', w_i, state)
        o_i = (jnp.einsum('bhck,bhkv->bhcv', q_i * jnp.exp(g_i)[..., None], state)
               + jnp.einsum('bhij,bhjv->bhiv', a_i, v_new))
        g_last = g_i[..., -1:]
        state = (state * jnp.exp(g_last)[..., None]
                 + jnp.einsum('bhck,bhcv->bhkv', k_i * jnp.exp(g_last - g_i)[..., None], v_new))
        return state, o_i

    s_final, o = lax.scan(step, s0.astype(f32), (qc, kc, u, w, gc, intra))
    return o.transpose(1, 0, 3, 2, 4).reshape(bsz, seq, nh, dv), s_final


def _reverse_seq(t):
    return jnp.flip(t, axis=1)


def _gdn_bidirectional(q, k, v, g, beta, s0):
    o_f, s_f = _gdn_chunked(q, k, v, g[:, :, 0], beta[:, :, 0], s0[:, 0])
    o_b, s_b = _gdn_chunked(_reverse_seq(q), _reverse_seq(k), _reverse_seq(v),
                            _reverse_seq(g[:, :, 1]), _reverse_seq(beta[:, :, 1]), s0[:, 1])
    return o_f + _reverse_seq(o_b), jnp.stack([s_f, s_b], axis=1)


def _gdn_branch(zq, zk, zv, zz, za, zb, conv_w, a_log, dt_bias, norm_w, s0):
    f32 = jnp.float32
    bsz, seq = zq.shape[:2]
    qkv = jax.nn.silu(_dwconv_centred(jnp.concatenate([zq, zk, zv], axis=-1), conv_w))
    q, k, v = jnp.split(qkv, [GDN_H * GDN_DK, 2 * GDN_H * GDN_DK], axis=-1)
    q = _l2norm(q.reshape(bsz, seq, GDN_H, GDN_DK))
    k = _l2norm(k.reshape(bsz, seq, GDN_H, GDN_DK))
    v = v.reshape(bsz, seq, GDN_H, GDN_DV)
    g = -jnp.exp(a_log.astype(f32)) * jax.nn.softplus(za.reshape(bsz, seq, 2, GDN_H).astype(f32) + dt_bias.astype(f32))
    beta = jax.nn.sigmoid(zb.reshape(bsz, seq, 2, GDN_H).astype(f32))
    o, s = _gdn_bidirectional(q, k, v, g, beta, s0)
    o = _rmsnorm(o, norm_w) * jax.nn.silu(zz.reshape(bsz, seq, GDN_H, GDN_DV).astype(f32))
    return o.reshape(bsz, seq, GDN_H * GDN_DV).astype(zq.dtype), s.astype(zq.dtype)


def _hyena_filters(seq, w1, b1, w2, b2, w3, freq):
    f32 = jnp.float32
    pos = jnp.arange(seq, dtype=f32)
    t = pos / max(seq - 1, 1)
    ang = (2.0 * math.pi * pos / seq)[:, None] * jnp.linspace(1e-4, HY_BANDS - 1, HY_BANDS, dtype=f32)
    z = jnp.concatenate([t[:, None], jnp.cos(ang), -jnp.sin(ang)], axis=-1)
    fr = freq.astype(f32)
    h = jnp.sin(fr * (z @ w1.astype(f32) + b1.astype(f32)))
    h = jnp.sin(fr * (h @ w2.astype(f32) + b2.astype(f32)))
    h = h @ w3.astype(f32)
    deltas = jnp.abs(jnp.linspace(HY_SLOW_DECAY, HY_FAST_DECAY, HY_W, dtype=f32))
    return h * jnp.exp(-t[:, None] * jnp.tile(deltas, 2))


def _hyena_branch(u, conv_w, conv_b, filt, d_bias):
    f32 = jnp.float32
    seq = u.shape[1]
    uc = _dwconv_centred(u, conv_w) + conv_b
    x0, x1, v = jnp.split(uc.astype(f32), 3, axis=-1)
    v = v * x1
    h_fwd, h_bwd = filt[:, :HY_W], filt[:, HY_W:]
    h_circ = jnp.concatenate([h_fwd, jnp.zeros((1, HY_W), f32), jnp.flip(h_bwd[1:], axis=0)], axis=0)
    n_fft = 2 * seq
    y = jnp.fft.irfft(jnp.fft.rfft(v, n=n_fft, axis=1) * jnp.fft.rfft(h_circ, n=n_fft, axis=0), n=n_fft, axis=1)[:, :seq]
    return ((y + v * d_bias.astype(f32)) * x0).astype(u.dtype)


def _gated_merge(branches, z_gate, w_branch, w_out):
    bsz, seq = z_gate.shape[:2]
    stacked = jnp.stack(branches, axis=2)
    proj = jnp.einsum('blnc,ncd->blnd', stacked, w_branch)
    gates = jax.nn.sigmoid(z_gate.reshape(bsz, seq, N_BRANCH, D_MODEL))
    return jnp.einsum('bld,de->ble', jnp.sum(gates * proj, axis=2), w_out)


def _mixer_sublayer(h, p, l, ctx):
    bsz, seq = h.shape[:2]
    split_at = [int(s) for s in np.cumsum(IN_SPLITS)[:-1]]
    (m_cq, m_ckv, m_kpe, g_q, g_k, g_v, g_z, g_a, g_b, hy_u, d_q, d_k, d_v, z_gate) = jnp.split(
        h @ p['w_in'][l], split_at, axis=-1)
    w_ukv = p['mla_w_ukv'][l]
    q = (_rmsnorm(m_cq, p['mla_q_norm_w'][l]) @ p['mla_w_uq'][l]).reshape(bsz, seq, MLA_H, MLA_NOPE + MLA_ROPE)
    ckv = _rmsnorm(m_ckv, p['mla_kv_norm_w'][l])
    kpe = m_kpe
    dq = d_q.reshape(bsz, seq, DA_H, 2, DA_DK)
    dk = d_k.reshape(bsz, seq, DA_H, 2, DA_DK)
    dv = d_v.reshape(bsz, seq, DA_H, DA_DV)
    if ctx is None:
        s0 = jnp.zeros((bsz, 2, GDN_H, GDN_DK, GDN_DV), h.dtype)
        q_a = q
        k_a, v_a = _mla_expand(ckv, kpe, w_ukv)
        q_d, k_d, v_d = dq, dk, dv
    else:
        ckv_c, kpe_c, dk_c, dv_c, s0 = ctx
        ang_r, ang_c = _axial_rope_angles(seq, MLA_ROPE)
        q_a = jnp.concatenate([q[..., :MLA_NOPE], _apply_axial_rope(q[..., MLA_NOPE:], ang_r, ang_c)], axis=-1)
        k_lat, v_lat = _mla_expand(ckv, _apply_axial_rope(kpe, ang_r, ang_c), w_ukv)
        k_ctx, v_ctx = _mla_expand(ckv_c, kpe_c, w_ukv)
        k_a = jnp.concatenate([k_lat, k_ctx], axis=1)
        v_a = jnp.concatenate([v_lat, v_ctx], axis=1)
        dang_r, dang_c = _axial_rope_angles(seq, DA_DK)
        q_d = _apply_axial_rope(dq, dang_r, dang_c)
        k_d = jnp.concatenate([_apply_axial_rope(dk, dang_r, dang_c), dk_c], axis=1)
        v_d = jnp.concatenate([dv, dv_c], axis=1)
    o_a = _mla_attend(q_a, k_a, v_a).reshape(bsz, seq, MLA_H * MLA_V)
    o_b, s_gdn = _gdn_branch(g_q, g_k, g_v, g_z, g_a, g_b, p['gdn_conv_w'][l], p['gdn_a_log'][l],
                             p['gdn_dt_bias'][l], p['gdn_norm_w'][l], s0)
    filt = _hyena_filters(seq, p['hy_f_w1'][l], p['hy_f_b1'][l], p['hy_f_w2'][l], p['hy_f_b2'][l],
                          p['hy_f_w3'][l], p['hy_f_freq'][l])
    o_c = _hyena_branch(hy_u, p['hy_conv_w'][l], p['hy_conv_b'][l], filt, p['hy_d'][l])
    lam_init = 0.8 - 0.6 * math.exp(-0.3 * l)
    f32 = jnp.float32
    lam = (jnp.exp(jnp.sum(p['da_lq1'][l].astype(f32) * p['da_lk1'][l].astype(f32)))
           - jnp.exp(jnp.sum(p['da_lq2'][l].astype(f32) * p['da_lk2'][l].astype(f32))) + lam_init)
    o_d = _diff_attend(q_d, k_d, v_d, lam)
    o_d = (_rmsnorm(o_d, p['da_norm_w'][l]) * (1.0 - lam_init)).reshape(bsz, seq, DA_H * DA_DV)
    merged = _gated_merge([o_a, o_b, o_c, o_d], z_gate, p['w_branch'][l], p['w_out'][l])
    new_ctx = (ckv, kpe, dk, dv, s_gdn) if ctx is None else None
    return merged, new_ctx


def _trunk_layer(x, p, l, cond, ctx):
    mod = (cond @ p['w_ada'][l] + p['b_ada'][l]).reshape(cond.shape[0], 1, 6, D_MODEL)
    sh_a, sc_a, gt_a, sh_m, sc_m, gt_m = [mod[:, :, i] for i in range(6)]
    h = _rmsnorm(x, p['norm_mix_w'][l]) * (1.0 + sc_a) + sh_a
    mix, new_ctx = _mixer_sublayer(h, p, l, ctx)
    x = x + gt_a * mix
    h = _rmsnorm(x, p['norm_mlp_w'][l]) * (1.0 + sc_m) + sh_m
    x = x + gt_m * (jnp.square(jax.nn.relu(h @ p['mlp_w1'][l])) @ p['mlp_w2'][l])
    return x, new_ctx


def setup_inputs(seed: int = 0) -> dict:
    key = jax.random.key(seed)
    ks = iter(jax.random.split(key, 48))
    f32 = jnp.float32

    def nrm(shape, scale):
        return jax.random.normal(next(ks), shape, f32) * scale

    def gain(shape):
        return 1.0 + nrm(shape, 0.02)

    a_vals = jax.random.uniform(next(ks), (DEPTH, 2, GDN_H), f32, 1.0, 16.0)
    dt = jnp.exp(jax.random.uniform(next(ks), (DEPTH, 2, GDN_H), f32, math.log(1e-3), math.log(1e-1)))
    gdn_dt_bias = dt + jnp.log(-jnp.expm1(-dt))
    return {
        'x_prompt': nrm((BATCH, SEQ, D_MODEL), 1.0),
        'x_sample': nrm((DEC_BATCH, DEC_SEQ, D_MODEL), 1.0),
        'cache_mla_ckv': nrm((DEC_BATCH, DEPTH, PAST_LEN, MLA_KV_LORA), 1.0),
        'cache_mla_kpe': nrm((DEC_BATCH, DEPTH, PAST_LEN, MLA_ROPE), 1.0),
        'cache_diff_k': nrm((DEC_BATCH, DEPTH, PAST_LEN, DA_H, 2, DA_DK), 1.0),
        'cache_diff_v': nrm((DEC_BATCH, DEPTH, PAST_LEN, DA_H, DA_DV), 1.0),
        'state_gdn': nrm((DEC_BATCH, DEPTH, 2, GDN_H, GDN_DK, GDN_DV), 0.1),
        'c': nrm((DEC_BATCH, D_MODEL), 1.0),
        'c_ctx': nrm((D_MODEL,), 1.0),
        'w_ada': nrm((DEPTH, D_MODEL, 6 * D_MODEL), D_MODEL ** -0.5),
        'b_ada': nrm((DEPTH, 6 * D_MODEL), 0.01),
        'norm_mix_w': gain((DEPTH, D_MODEL)),
        'norm_mlp_w': gain((DEPTH, D_MODEL)),
        'w_in': nrm((DEPTH, D_MODEL, IN_W), D_MODEL ** -0.5),
        'mla_q_norm_w': gain((DEPTH, MLA_Q_LORA)),
        'mla_w_uq': nrm((DEPTH, MLA_Q_LORA, MLA_H * (MLA_NOPE + MLA_ROPE)), MLA_Q_LORA ** -0.5),
        'mla_kv_norm_w': gain((DEPTH, MLA_KV_LORA)),
        'mla_w_ukv': nrm((DEPTH, MLA_KV_LORA, MLA_H * (MLA_NOPE + MLA_V)), MLA_KV_LORA ** -0.5),
        'gdn_conv_w': nrm((DEPTH, SHORT_K, 2 * GDN_H * GDN_DK + GDN_H * GDN_DV), SHORT_K ** -0.5),
        'gdn_a_log': jnp.log(a_vals),
        'gdn_dt_bias': gdn_dt_bias,
        'gdn_norm_w': gain((DEPTH, GDN_DV)),
        'hy_conv_w': nrm((DEPTH, SHORT_K, 3 * HY_W), SHORT_K ** -0.5),
        'hy_conv_b': nrm((DEPTH, 3 * HY_W), 0.01),
        'hy_f_w1': nrm((DEPTH, HY_EMB, HY_FH), HY_EMB ** -0.5),
        'hy_f_b1': nrm((DEPTH, HY_FH), 0.01),
        'hy_f_w2': nrm((DEPTH, HY_FH, HY_FH), HY_FH ** -0.5),
        'hy_f_b2': nrm((DEPTH, HY_FH), 0.01),
        'hy_f_w3': nrm((DEPTH, HY_FH, 2 * HY_W), 0.1 * HY_FH ** -0.5),
        'hy_f_freq': gain((DEPTH, HY_FH)),
        'hy_d': nrm((DEPTH, HY_W), 0.5),
        'da_lq1': nrm((DEPTH, DA_DK), 0.1),
        'da_lk1': nrm((DEPTH, DA_DK), 0.1),
        'da_lq2': nrm((DEPTH, DA_DK), 0.1),
        'da_lk2': nrm((DEPTH, DA_DK), 0.1),
        'da_norm_w': gain((DEPTH, DA_DV)),
        'w_branch': nrm((DEPTH, N_BRANCH, BR_W, D_MODEL), BR_W ** -0.5),
        'w_out': nrm((DEPTH, D_MODEL, D_MODEL), D_MODEL ** -0.5),
        'mlp_w1': nrm((DEPTH, D_MODEL, D_FF), D_MODEL ** -0.5),
        'mlp_w2': nrm((DEPTH, D_FF, D_MODEL), D_FF ** -0.5),
        'final_norm_w': gain((D_MODEL,)),
    }


def reference(x_prompt, x_sample, cache_mla_ckv, cache_mla_kpe, cache_diff_k, cache_diff_v, state_gdn,
              c, c_ctx, w_ada, b_ada, norm_mix_w, norm_mlp_w, w_in, mla_q_norm_w, mla_w_uq, mla_kv_norm_w,
              mla_w_ukv, gdn_conv_w, gdn_a_log, gdn_dt_bias, gdn_norm_w, hy_conv_w, hy_conv_b, hy_f_w1, hy_f_b1,
              hy_f_w2, hy_f_b2, hy_f_w3, hy_f_freq, hy_d, da_lq1, da_lk1, da_lq2, da_lk2, da_norm_w, w_branch,
              w_out, mlp_w1, mlp_w2, final_norm_w):
    p = {
        'w_ada': w_ada, 'b_ada': b_ada, 'norm_mix_w': norm_mix_w, 'norm_mlp_w': norm_mlp_w, 'w_in': w_in,
        'mla_q_norm_w': mla_q_norm_w, 'mla_w_uq': mla_w_uq, 'mla_kv_norm_w': mla_kv_norm_w, 'mla_w_ukv': mla_w_ukv,
        'gdn_conv_w': gdn_conv_w, 'gdn_a_log': gdn_a_log, 'gdn_dt_bias': gdn_dt_bias, 'gdn_norm_w': gdn_norm_w,
        'hy_conv_w': hy_conv_w, 'hy_conv_b': hy_conv_b, 'hy_f_w1': hy_f_w1, 'hy_f_b1': hy_f_b1,
        'hy_f_w2': hy_f_w2, 'hy_f_b2': hy_f_b2, 'hy_f_w3': hy_f_w3, 'hy_f_freq': hy_f_freq, 'hy_d': hy_d,
        'da_lq1': da_lq1, 'da_lk1': da_lk1, 'da_lq2': da_lq2, 'da_lk2': da_lk2, 'da_norm_w': da_norm_w,
        'w_branch': w_branch, 'w_out': w_out, 'mlp_w1': mlp_w1, 'mlp_w2': mlp_w2,
    }
    cond_ctx = jax.nn.silu(c_ctx)[None, :]
    x = x_prompt
    ctx_tensors = []
    for l in range(DEPTH):
        x, new_ctx = _trunk_layer(x, p, l, cond_ctx, None)
        ctx_tensors.append(new_ctx)
    y_prompt = _rmsnorm(x, final_norm_w)
    new_mla_ckv = jnp.stack([t[0] for t in ctx_tensors], axis=1)
    new_mla_kpe = jnp.stack([t[1] for t in ctx_tensors], axis=1)
    new_diff_k = jnp.stack([t[2] for t in ctx_tensors], axis=1)
    new_diff_v = jnp.stack([t[3] for t in ctx_tensors], axis=1)
    new_gdn_state = jnp.stack([t[4] for t in ctx_tensors], axis=1)
    cond = jax.nn.silu(c)
    x = x_sample
    for l in range(DEPTH):
        ctx_l = (cache_mla_ckv[:, l], cache_mla_kpe[:, l], cache_diff_k[:, l], cache_diff_v[:, l], state_gdn[:, l])
        x, _ = _trunk_layer(x, p, l, cond, ctx_l)
    y_sample = _rmsnorm(x, final_norm_w)
    return (y_prompt, y_sample, new_mla_ckv, new_mla_kpe, new_diff_k, new_diff_v, new_gdn_state)
```

```python
import functools
import math

import numpy as np
import jax
import jax.numpy as jnp
from jax import lax
from jax.experimental import pallas as pl
from jax.experimental.pallas import tpu as pltpu

f32 = jnp.float32
bf16 = jnp.bfloat16

D_MODEL = 1024
GRID_W = 64
N_BRANCH = 4
MLA_H = 4
MLA_NOPE = 64
MLA_ROPE = 32
MLA_V = 64
MLA_Q_LORA = 256
MLA_KV_LORA = 128
GDN_H = 4
GDN_DK = 64
GDN_DV = 64
GDN_CHUNK = 64
HY_W = 256
HY_BANDS = 16
HY_EMB = 1 + 2 * HY_BANDS
HY_FH = 64
HY_SLOW_DECAY = math.log(1e-2) / 1.5
HY_FAST_DECAY = math.log(1e-2) / 0.3
DA_H = 4
DA_DK = 32
DA_DV = 64
D_FF = 4 * D_MODEL
ROPE_BASE = 10000.0
EPS = 1e-6
IN_SPLITS = (MLA_Q_LORA, MLA_KV_LORA, MLA_ROPE,
             GDN_H * GDN_DK, GDN_H * GDN_DK, GDN_H * GDN_DV, GDN_H * GDN_DV, 2 * GDN_H, 2 * GDN_H,
             3 * HY_W,
             DA_H * 2 * DA_DK, DA_H * 2 * DA_DK, DA_H * DA_DV,
             N_BRANCH * D_MODEL)

LANES = 128
DFT_N2 = 256
VMEM_LIMIT = 56 * 1024 * 1024


def _cparams(sem):
    return pltpu.CompilerParams(dimension_semantics=sem, vmem_limit_bytes=VMEM_LIMIT)


def _dot(a, b):
    return jnp.dot(a.astype(bf16), b.astype(bf16), preferred_element_type=f32)


def _dot_nt(a, b):
    return lax.dot_general(a.astype(bf16), b.astype(bf16), (((1,), (1,)), ((), ())), preferred_element_type=f32)


def _dot_tn(a, b):
    return lax.dot_general(a.astype(bf16), b.astype(bf16), (((0,), (0,)), ((), ())), preferred_element_type=f32)


def _split(x):
    hi = x.astype(bf16)
    lo = (x - hi.astype(f32)).astype(bf16)
    return hi, lo


def _dot3(a, b):
    ah, al = _split(a)
    bh, bl = _split(b)
    return (jnp.dot(ah, bh, preferred_element_type=f32) + jnp.dot(ah, bl, preferred_element_type=f32)
            + jnp.dot(al, bh, preferred_element_type=f32))


def _dot3_w(a, bh, bl):
    ah, al = _split(a)
    return (jnp.dot(ah, bh, preferred_element_type=f32) + jnp.dot(ah, bl, preferred_element_type=f32)
            + jnp.dot(al, bh, preferred_element_type=f32))


def _sigmoid(x):
    return 1.0 / (1.0 + jnp.exp(-x))


def _silu(x):
    return x * _sigmoid(x)


def _softplus(x):
    return jnp.maximum(x, 0.0) + jnp.log(1.0 + jnp.exp(-jnp.abs(x)))


def _rms(x, w):
    return x * lax.rsqrt(jnp.mean(x * x, axis=-1, keepdims=True) + EPS) * w


def _mod_kernel(c_ref, w_ref, b_ref, o_ref):
    c = _silu(c_ref[...])
    o_ref[0] = _dot3(c, w_ref[0]) + b_ref[0]


def _modulation(cond8, w_ada, b_ada):
    depth, d, n6 = w_ada.shape
    tn = 1536
    return pl.pallas_call(
        _mod_kernel,
        grid=(depth, n6 // tn),
        in_specs=[pl.BlockSpec((8, d), lambda l, j: (0, 0)),
                  pl.BlockSpec((1, d, tn), lambda l, j: (l, 0, j)),
                  pl.BlockSpec((1, 1, tn), lambda l, j: (l, 0, j))],
        out_specs=pl.BlockSpec((1, 8, tn), lambda l, j: (l, 0, j)),
        out_shape=jax.ShapeDtypeStruct((depth, 8, n6), f32),
        compiler_params=_cparams(("arbitrary", "arbitrary")),
        name="modulation",
    )(cond8, w_ada, b_ada.reshape(depth, 1, n6))


def _norm_mod(x, nw, scale, shift):
    return _rms(x, nw) * (1.0 + scale) + shift


def _in_kernel(x_ref, mod_ref, nw_ref, wa_ref, wd_ref, wg_ref, wab_ref, whT_ref,
               za_ref, zd_ref, zg_ref, zab_ref, zhT_ref):
    g = pl.program_id(0)
    d = D_MODEL
    mod = mod_ref[pl.ds(g, 1), :]
    h = _norm_mod(x_ref[0], nw_ref[...], mod[:, d:2 * d], mod[:, 0:d]).astype(bf16)
    za_ref[0] = jnp.dot(h, wa_ref[...], preferred_element_type=f32)
    zd_ref[0] = jnp.dot(h, wd_ref[...], preferred_element_type=f32)
    zg = jnp.dot(h, wg_ref[...], preferred_element_type=f32)
    for j in range(zg_ref.shape[0]):
        zg_ref[j, 0] = zg[:, 64 * j:64 * (j + 1)]
    zab_ref[0] = jnp.dot(h, wab_ref[...], preferred_element_type=f32)
    zhT_ref[0] = lax.dot_general(whT_ref[...], h, (((1,), (1,)), ((), ())), preferred_element_type=f32)


def _in_proj(x, mod_l, nw, wa, wd, wg, wab, whT, tm):
    G, T, d = x.shape
    na, nd, ng, nab, nh = wa.shape[1], wd.shape[1], wg.shape[1], wab.shape[1], whT.shape[0]
    full = lambda shape: pl.BlockSpec(shape, lambda g, i: (0,) * len(shape))
    return pl.pallas_call(
        _in_kernel,
        grid=(G, T // tm),
        in_specs=[pl.BlockSpec((1, tm, d), lambda g, i: (g, i, 0)),
                  full(mod_l.shape), full((1, d)), full(wa.shape), full(wd.shape), full(wg.shape),
                  full(wab.shape), full(whT.shape)],
        out_specs=[pl.BlockSpec((1, tm, na), lambda g, i: (g, i, 0)),
                   pl.BlockSpec((1, tm, nd), lambda g, i: (g, i, 0)),
                   pl.BlockSpec((ng // 64, 1, tm, 64), lambda g, i: (0, g, i, 0)),
                   pl.BlockSpec((1, tm, nab), lambda g, i: (g, i, 0)),
                   pl.BlockSpec((1, nh, tm), lambda g, i: (g, 0, i))],
        out_shape=[jax.ShapeDtypeStruct((G, T, na), f32), jax.ShapeDtypeStruct((G, T, nd), f32),
                   jax.ShapeDtypeStruct((ng // 64, G, T, 64), f32), jax.ShapeDtypeStruct((G, T, nab), f32),
                   jax.ShapeDtypeStruct((G, nh, T), f32)],
        compiler_params=_cparams(("arbitrary", "arbitrary")),
        name="in_proj",
    )(x, mod_l, nw.reshape(1, d), wa, wd, wg, wab, whT)


def _prep_kernel(za_ref, zd_ref, ca_ref, cb_ref, cd_ref, sd_ref, qnw_ref, kvnw_ref, wqa_ref, wqb_ref, wka_ref, wv_ref,
                 q_ref, k_ref, v_ref, ckv_ref, dq_ref, dk_ref, dv_ref):
    za = za_ref[0]
    cqn = _rms(za[:, 0:256], qnw_ref[...]).astype(bf16)
    ckv = _rms(za[:, 256:384], kvnw_ref[...])
    ckv_ref[0] = ckv
    ckvb = ckv.astype(bf16)
    ca = ca_ref[0]
    cb = cb_ref[0]
    qa = jnp.dot(cqn, wqa_ref[...], preferred_element_type=f32)
    qb = jnp.dot(cqn, wqb_ref[...], preferred_element_type=f32)
    kn = jnp.dot(ckvb, wka_ref[...], preferred_element_type=f32)
    vv = jnp.dot(ckvb, wv_ref[...], preferred_element_type=f32)
    krope = za[:, 384:512] * ca + za[:, 512:640] * cb
    for h in range(MLA_H):
        sl = slice(128 * h, 128 * (h + 1))
        q_ref[h, 0] = (qa[:, sl] * ca + qb[:, sl] * cb).astype(bf16)
        k_ref[h, 0] = (kn[:, sl] + krope).astype(bf16)
        v_ref[h, 0] = vv[:, 64 * h:64 * (h + 1)].astype(bf16)
    zd = zd_ref[0]
    cd = cd_ref[0]
    sd = sd_ref[0]
    dq = zd[:, 0:256] * cd + zd[:, 256:512] * sd
    dk = zd[:, 512:768] * cd + zd[:, 768:1024] * sd
    dv = zd[:, 1024:1280]
    for h in range(DA_H):
        sl = slice(64 * h, 64 * (h + 1))
        dq_ref[h, 0] = dq[:, sl].astype(bf16)
        dk_ref[h, 0] = dk[:, sl].astype(bf16)
        dv_ref[h, 0] = dv[:, sl].astype(bf16)


def _prep(za, zd, ca, cb, cd, sd, qnw, kvnw, wqa, wqb, wka, wv, tm):
    G, T, na = za.shape
    nd = zd.shape[2]
    full = lambda shape: pl.BlockSpec(shape, lambda g, i: (0,) * len(shape))
    tab = lambda w: pl.BlockSpec((1, tm, w), lambda g, i: (jnp.minimum(g, 1), i, 0))
    hm = lambda w: pl.BlockSpec((4, 1, tm, w), lambda g, i: (0, g, i, 0))
    return pl.pallas_call(
        _prep_kernel,
        grid=(G, T // tm),
        in_specs=[pl.BlockSpec((1, tm, na), lambda g, i: (g, i, 0)),
                  pl.BlockSpec((1, tm, nd), lambda g, i: (g, i, 0)),
                  tab(128), tab(128), tab(256), tab(256),
                  full((1, 256)), full((1, 128)), full(wqa.shape), full(wqb.shape), full(wka.shape), full(wv.shape)],
        out_specs=[hm(128), hm(128), hm(64), pl.BlockSpec((1, tm, 128), lambda g, i: (g, i, 0)),
                   hm(64), hm(64), hm(64)],
        out_shape=[jax.ShapeDtypeStruct((4, G, T, 128), bf16), jax.ShapeDtypeStruct((4, G, T, 128), bf16),
                   jax.ShapeDtypeStruct((4, G, T, 64), bf16), jax.ShapeDtypeStruct((G, T, 128), f32),
                   jax.ShapeDtypeStruct((4, G, T, 64), bf16), jax.ShapeDtypeStruct((4, G, T, 64), bf16),
                   jax.ShapeDtypeStruct((4, G, T, 64), bf16)],
        compiler_params=_cparams(("arbitrary", "arbitrary")),
        name="attn_prep",
    )(za, zd, ca, cb, cd, sd, qnw.reshape(1, -1), kvnw.reshape(1, -1), wqa, wqb, wka, wv)


def _cache_kv_kernel(ckv_ref, kpe_ref, wka_ref, wv_ref, k_ref, v_ref):
    ckvb = ckv_ref[0, 0].astype(bf16)
    kn = jnp.dot(ckvb, wka_ref[0], preferred_element_type=f32)
    vv = jnp.dot(ckvb, wv_ref[0], preferred_element_type=f32)
    kpe = kpe_ref[0, 0]
    for h in range(MLA_H):
        k_ref[0, h, 0] = (kn[:, 128 * h:128 * (h + 1)] + kpe).astype(bf16)
        v_ref[0, h, 0] = vv[:, 64 * h:64 * (h + 1)].astype(bf16)


def _cache_kv(cache_ckv, cache_kpe_pad, wka, wv):
    bl, depth, p, _ = cache_ckv.shape
    return pl.pallas_call(
        _cache_kv_kernel,
        grid=(depth, bl),
        in_specs=[pl.BlockSpec((1, 1, p, 128), lambda l, b: (b, l, 0, 0)),
                  pl.BlockSpec((1, 1, p, 128), lambda l, b: (b, l, 0, 0)),
                  pl.BlockSpec((1,) + wka.shape[1:], lambda l, b: (l, 0, 0)),
                  pl.BlockSpec((1,) + wv.shape[1:], lambda l, b: (l, 0, 0))],
        out_specs=[pl.BlockSpec((1, 4, 1, p, 128), lambda l, b: (l, 0, b, 0, 0)),
                   pl.BlockSpec((1, 4, 1, p, 64), lambda l, b: (l, 0, b, 0, 0))],
        out_shape=[jax.ShapeDtypeStruct((depth, 4, bl, p, 128), bf16),
                   jax.ShapeDtypeStruct((depth, 4, bl, p, 64), bf16)],
        compiler_params=_cparams(("arbitrary", "arbitrary")),
        name="cache_kv",
    )(cache_ckv, cache_kpe_pad, wka, wv)


def _softmax_chunk(q, k, v, scale, state):
    m, l, acc = state
    s = _dot_nt(q, k) * scale
    m_new = jnp.maximum(m, jnp.max(s, axis=1, keepdims=True))
    alpha = jnp.exp(m - m_new)
    p = jnp.exp(s - m_new)
    l = alpha * l + jnp.sum(p, axis=1, keepdims=True)
    acc = alpha * acc + jnp.dot(p.astype(bf16), v, preferred_element_type=f32)
    return m_new, l, acc


def _softmax_init(tq, dv):
    return (jnp.full((tq, 1), -jnp.inf, f32), jnp.zeros((tq, 1), f32), jnp.zeros((tq, dv), f32))


def _mla_attn_kernel(*refs, kc, has_cache):
    if has_cache:
        q_ref, k_ref, v_ref, kc_ref, vc_ref, o_ref = refs
    else:
        q_ref, k_ref, v_ref, o_ref = refs
    tq = q_ref.shape[2]
    lk = k_ref.shape[2]
    scale = (MLA_NOPE + MLA_ROPE) ** -0.5
    for h in range(MLA_H):
        q = q_ref[h, 0]

        def body(c, state, h=h, q=q):
            r = pl.multiple_of(c * kc, kc)
            return _softmax_chunk(q, k_ref[h, 0, pl.ds(r, kc), :], v_ref[h, 0, pl.ds(r, kc), :], scale, state)

        state = lax.fori_loop(0, lk // kc, body, _softmax_init(tq, MLA_V))
        if has_cache:
            state = _softmax_chunk(q, kc_ref[h, 0], vc_ref[h, 0], scale, state)
        o_ref[0, :, MLA_V * h:MLA_V * (h + 1)] = state[2] / state[1]


def _mla_attention(q, k, v, seq_off, nseq, L, tq, kc, cache=None):
    H = q.shape[0]
    in_specs = [pl.BlockSpec((H, 1, tq, 128), lambda b, i: (0, b + seq_off, i, 0)),
                pl.BlockSpec((H, 1, L, 128), lambda b, i: (0, b + seq_off, 0, 0)),
                pl.BlockSpec((H, 1, L, 64), lambda b, i: (0, b + seq_off, 0, 0))]
    args = [q, k, v]
    if cache is not None:
        kcache, vcache = cache
        p = kcache.shape[2]
        in_specs += [pl.BlockSpec((H, 1, p, 128), lambda b, i: (0, b, 0, 0)),
                     pl.BlockSpec((H, 1, p, 64), lambda b, i: (0, b, 0, 0))]
        args += [kcache, vcache]
    return pl.pallas_call(
        functools.partial(_mla_attn_kernel, kc=kc, has_cache=cache is not None),
        grid=(nseq, L // tq),
        in_specs=in_specs,
        out_specs=pl.BlockSpec((1, tq, H * MLA_V), lambda b, i: (b, i, 0)),
        out_shape=jax.ShapeDtypeStruct((nseq, L, H * MLA_V), f32),
        compiler_params=_cparams(("arbitrary", "arbitrary")),
        name="mla_attn",
    )(*args)


def _diff_attn_kernel(*refs, kc, has_cache):
    if has_cache:
        q_ref, k_ref, v_ref, kc_ref, vc_ref, lam_ref, nw_ref, o_ref = refs
    else:
        q_ref, k_ref, v_ref, lam_ref, nw_ref, o_ref = refs
    tq = q_ref.shape[2]
    lk = k_ref.shape[2]
    scale = DA_DK ** -0.5
    lamv = lam_ref[...]
    lam_init = lamv[4:5, 0:1]
    lam = (jnp.exp(jnp.sum(lamv[0:1] * lamv[1:2], axis=1, keepdims=True))
           - jnp.exp(jnp.sum(lamv[2:3] * lamv[3:4], axis=1, keepdims=True)) + lam_init)
    lane = lax.broadcasted_iota(jnp.int32, (tq, 2 * DA_DK), 1)
    for h in range(DA_H):
        q = q_ref[h, 0]
        q1 = jnp.where(lane < DA_DK, q, jnp.zeros_like(q))
        q2 = jnp.where(lane >= DA_DK, q, jnp.zeros_like(q))

        def body(c, st, h=h, q1=q1, q2=q2):
            r = pl.multiple_of(c * kc, kc)
            kk = k_ref[h, 0, pl.ds(r, kc), :]
            vv = v_ref[h, 0, pl.ds(r, kc), :]
            return (_softmax_chunk(q1, kk, vv, scale, st[0]), _softmax_chunk(q2, kk, vv, scale, st[1]))

        st = lax.fori_loop(0, lk // kc, body, (_softmax_init(tq, DA_DV), _softmax_init(tq, DA_DV)))
        if has_cache:
            st = (_softmax_chunk(q1, kc_ref[h, 0], vc_ref[h, 0], scale, st[0]),
                  _softmax_chunk(q2, kc_ref[h, 0], vc_ref[h, 0], scale, st[1]))
        o = st[0][2] / st[0][1] - lam * (st[1][2] / st[1][1])
        o_ref[0, :, DA_DV * h:DA_DV * (h + 1)] = _rms(o, nw_ref[...]) * (1.0 - lam_init)


def _diff_attention(q, k, v, lamv, nw, seq_off, nseq, L, tq, kc, cache=None):
    H = q.shape[0]
    hm = lambda rows: pl.BlockSpec((H, 1, rows, 64), lambda b, i: (0, b + seq_off, 0, 0))
    in_specs = [pl.BlockSpec((H, 1, tq, 64), lambda b, i: (0, b + seq_off, i, 0)), hm(L), hm(L)]
    args = [q, k, v]
    if cache is not None:
        kcache, vcache = cache
        p = kcache.shape[2]
        in_specs += [pl.BlockSpec((H, 1, p, 64), lambda b, i: (0, b, 0, 0)),
                     pl.BlockSpec((H, 1, p, 64), lambda b, i: (0, b, 0, 0))]
        args += [kcache, vcache]
    in_specs += [pl.BlockSpec((8, DA_DK), lambda b, i: (0, 0)), pl.BlockSpec((1, DA_DV), lambda b, i: (0, 0))]
    args += [lamv, nw.reshape(1, DA_DV)]
    return pl.pallas_call(
        functools.partial(_diff_attn_kernel, kc=kc, has_cache=cache is not None),
        grid=(nseq, L // tq),
        in_specs=in_specs,
        out_specs=pl.BlockSpec((1, tq, H * DA_DV), lambda b, i: (b, i, 0)),
        out_shape=jax.ShapeDtypeStruct((nseq, L, H * DA_DV), f32),
        compiler_params=_cparams(("arbitrary", "arbitrary")),
        name="diff_attn",
    )(*args)


def _conv3_rows(x, w):
    L = x.shape[0]
    row = lax.broadcasted_iota(jnp.int32, x.shape, 0)
    xp = jnp.where(row == 0, 0.0, pltpu.roll(x, 1, 0))
    xn = jnp.where(row == L - 1, 0.0, pltpu.roll(x, L - 1, 0))
    return xp * w[0:1] + x * w[1:2] + xn * w[2:3]


def _gdn_chunk(q, k, v, gab, alog, dtb, h, direction, state):
    c = GDN_CHUNK
    ii = lax.broadcasted_iota(jnp.int32, (c, c), 0)
    jj = lax.broadcasted_iota(jnp.int32, (c, c), 1)
    if direction == 0:
        incl, strict, incl_t = ii >= jj, ii > jj, ii <= jj
    else:
        incl, strict, incl_t = ii <= jj, ii < jj, ii >= jj
    lane = lax.broadcasted_iota(jnp.int32, gab.shape, 1)
    gfull = -jnp.exp(alog) * _softplus(gab + dtb)
    bfull = _sigmoid(gab)
    col = direction * GDN_H + h
    g_col = jnp.sum(jnp.where(lane == col, gfull, 0.0), axis=1, keepdims=True)
    b_col = jnp.sum(jnp.where(lane == 2 * GDN_H + col, bfull, 0.0), axis=1, keepdims=True)
    g_b = jnp.broadcast_to(g_col, (c, c))
    g_row = jnp.sum(jnp.where(ii == jj, g_b, 0.0), axis=0, keepdims=True)
    gc_row = jnp.sum(jnp.where(incl_t, g_b, 0.0), axis=0, keepdims=True)
    gc_col = jnp.sum(jnp.where(incl, jnp.broadcast_to(g_row, (c, c)), 0.0), axis=1, keepdims=True)
    decay = jnp.where(incl, jnp.exp(jnp.where(incl, gc_col - gc_row, 0.0)), 0.0)
    kb = k * b_col
    tri = jnp.where(strict, _dot_nt(kb, k) * decay, 0.0)
    base = 8
    nm = -jnp.where((ii // base) == (jj // base), tri, 0.0)
    t = (ii == jj).astype(f32) + nm
    p = nm
    for _ in range(2):
        p = _dot3(p, p)
        t = t + _dot3(t, p)
    s = base
    while s < c:
        off = jnp.where(((ii // (2 * s)) == (jj // (2 * s))) & ((ii // s) != (jj // s)), tri, 0.0)
        t = t - _dot3(_dot3(t, off), t)
        s *= 2
    egc = jnp.exp(gc_col)
    u = _dot3(t, v * b_col)
    w = _dot3(t, kb * egc)
    qs = q * (GDN_DK ** -0.5)
    intra = jnp.where(incl, _dot_nt(qs, k) * decay, 0.0)
    v_new = u - _dot(w, state)
    o = _dot(qs * egc, state) + _dot(intra, v_new)
    g_last = jnp.sum(g_row, axis=1, keepdims=True)
    state = state * jnp.exp(g_last) + _dot_tn(k * jnp.exp(g_last - gc_col), v_new)
    return o, state


def _gdn_kernel(*refs, has_s0):
    if has_s0:
        (q_ref, k_ref, v_ref, z_ref, cwq_ref, cwk_ref, cwv_ref, gab_ref, alog_ref, dtb_ref, nw_ref, s0_ref,
         o_ref, qn, kn, vn, of, ob) = refs
        sfin_ref = None
    else:
        (q_ref, k_ref, v_ref, z_ref, cwq_ref, cwk_ref, cwv_ref, gab_ref, alog_ref, dtb_ref, nw_ref,
         o_ref, sfin_ref, qn, kn, vn, of, ob) = refs
    h = pl.program_id(1)
    L = q_ref.shape[2]
    n = L // GDN_CHUNK

    def l2(x):
        return x * lax.rsqrt(jnp.sum(x * x, axis=-1, keepdims=True) + EPS)

    qn[...] = l2(_silu(_conv3_rows(q_ref[0, 0], cwq_ref[0])))
    kn[...] = l2(_silu(_conv3_rows(k_ref[0, 0], cwk_ref[0])))
    vn[...] = _silu(_conv3_rows(v_ref[0, 0], cwv_ref[0]))
    alog = alog_ref[...]
    dtb = dtb_ref[...]

    def body(i, states):
        new = []
        for direction, out in ((0, of), (1, ob)):
            ci = i if direction == 0 else n - 1 - i
            r = pl.multiple_of(ci * GDN_CHUNK, GDN_CHUNK)
            rows = pl.ds(r, GDN_CHUNK)
            o, s = _gdn_chunk(qn[rows, :], kn[rows, :], vn[rows, :], gab_ref[0, rows, :], alog, dtb, h, direction,
                              states[direction])
            out[rows, :] = o
            new.append(s)
        return tuple(new)

    if has_s0:
        init = (s0_ref[0, 0, 0, 0], s0_ref[0, 0, 1, 0])
    else:
        init = (jnp.zeros((GDN_DK, GDN_DV), f32), jnp.zeros((GDN_DK, GDN_DV), f32))
    s_f, s_b = lax.fori_loop(0, n, body, init)
    if sfin_ref is not None:
        sfin_ref[0, 0, 0] = s_f
        sfin_ref[0, 1, 0] = s_b
    o = of[...] + ob[...]
    o_ref[0, 0] = _rms(o, nw_ref[...]) * _silu(z_ref[0, 0])


def _gdn(zg, zab, cw, alog_row, dtb_row, nw, seq_off, nseq, L, s0=None, layer=0):
    H = GDN_H
    hm = lambda k: pl.BlockSpec((1, 1, L, 64), lambda s, h, k=k: (k * H + h, s + seq_off, 0, 0))
    cws = lambda k: pl.BlockSpec((1, 3, 64), lambda s, h, k=k: (k * H + h, 0, 0))
    full = lambda shape: pl.BlockSpec(shape, lambda s, h: (0,) * len(shape))
    in_specs = [hm(0), hm(1), hm(2), hm(3), cws(0), cws(1), cws(2),
                pl.BlockSpec((1, L, 128), lambda s, h: (s + seq_off, 0, 0)),
                full((1, 128)), full((1, 128)), full((1, 64))]
    args = [zg, zg, zg, zg, cw, cw, cw, zab, alog_row, dtb_row, nw.reshape(1, 64)]
    out_specs = [pl.BlockSpec((1, 1, L, 64), lambda s, h: (h, s, 0, 0))]
    out_shape = [jax.ShapeDtypeStruct((H, nseq, L, 64), f32)]
    if s0 is not None:
        in_specs.append(pl.BlockSpec((1, 1, 2, 1, GDN_DK, GDN_DV), lambda s, h: (s, layer, 0, h, 0, 0)))
        args.append(s0)
    else:
        out_specs.append(pl.BlockSpec((1, 2, 1, GDN_DK, GDN_DV), lambda s, h: (s, 0, h, 0, 0)))
        out_shape.append(jax.ShapeDtypeStruct((nseq, 2, H, GDN_DK, GDN_DV), f32))
    return pl.pallas_call(
        functools.partial(_gdn_kernel, has_s0=s0 is not None),
        grid=(nseq, H),
        in_specs=in_specs,
        out_specs=out_specs,
        out_shape=out_shape,
        scratch_shapes=[pltpu.VMEM((L, 64), f32)] * 5,
        compiler_params=_cparams(("arbitrary", "arbitrary")),
        name="gdn",
    )(*args)


def _bitrev(p, bits):
    r = 0
    for _ in range(bits):
        r = (r << 1) | (p & 1)
        p >>= 1
    return r


@functools.lru_cache(maxsize=None)
def _fft_tables(L):
    n = 2 * L
    n2 = DFT_N2
    n1 = n // n2
    bits = n1.bit_length() - 1
    npair = max(n1 // 2, 1)
    sta = np.zeros((max(bits, 1) * npair, 2 * n2), np.float64)
    stb = np.zeros_like(sta)
    for s in range(bits):
        half = n1 >> (s + 1)
        for p in range(npair):
            j = p % half
            ang = -2.0 * np.pi * j / (2 * half)
            wr, wi = np.cos(ang), np.sin(ang)
            sta[s * npair + p, :] = wr
            stb[s * npair + p, :n2] = -wi
            stb[s * npair + p, n2:] = wi
    twa = np.zeros((n1, 2 * n2), np.float64)
    twb = np.zeros_like(twa)
    lanes = np.arange(n2)
    for p in range(n1):
        ang = -2.0 * np.pi * lanes * _bitrev(p, bits) / n
        twa[p, :n2] = np.cos(ang)
        twa[p, n2:] = np.cos(ang)
        twb[p, :n2] = -np.sin(ang)
        twb[p, n2:] = np.sin(ang)
    kn = np.outer(lanes, lanes) * (-2.0 * np.pi / n2)
    fr, fi = np.cos(kn), np.sin(kn)
    fwd = np.block([[fr, fi], [-fi, fr]])
    inv = np.block([[fr, -fi], [fi, fr]])
    as32 = lambda a: np.asarray(a, np.float32)
    return dict(n1=n1, bits=bits, npair=npair, sta=as32(sta), stb=as32(stb), twa=as32(twa), twb=as32(twb),
                fwd=as32(fwd), inv=as32(inv))


@functools.lru_cache(maxsize=None)
def _hyena_pos_table(L):
    n = 2 * L
    idx = np.arange(n)
    pos = np.where(idx < L, idx, n - idx).astype(np.float64)
    pos[L] = 0.0
    t = pos / max(L - 1, 1)
    bands = np.linspace(1e-4, HY_BANDS - 1, HY_BANDS).astype(np.float32).astype(np.float64)
    ang = (2.0 * math.pi * pos / L)[None, :] * bands[:, None]
    z = np.zeros((LANES, n), np.float64)
    z[0] = t
    z[1:1 + HY_BANDS] = np.cos(ang)
    z[1 + HY_BANDS:1 + 2 * HY_BANDS] = -np.sin(ang)
    deltas = np.abs(np.linspace(HY_SLOW_DECAY, HY_FAST_DECAY, HY_W)).reshape(HY_W, 1)
    return np.asarray(z, np.float32), np.asarray(deltas, np.float32)


def _swap_halves(x):
    n2 = x.shape[1] // 2
    return jnp.concatenate([x[:, n2:], x[:, :n2]], axis=1)


def _fft_forward(X, sta_ref, stb_ref, twa_ref, twb_ref, fh_ref, fl_ref, n1, bits, npair, ct, mrows):
    for s in range(bits):
        half = n1 >> (s + 1)

        def pair(p, carry, s=s, half=half):
            grp = p // half
            j = p - grp * half
            a = grp * 2 * half + j
            ra = pl.ds(pl.multiple_of(a * ct, ct), ct)
            rb = pl.ds(pl.multiple_of((a + half) * ct, ct), ct)
            xa = X[ra, :]
            xb = X[rb, :]
            X[ra, :] = xa + xb
            d = xa - xb
            X[rb, :] = d * sta_ref[pl.ds(s * npair + p, 1), :] + _swap_halves(d) * stb_ref[pl.ds(s * npair + p, 1), :]
            return carry

        lax.fori_loop(0, npair, pair, 0)

    def blk(p, carry):
        r = pl.ds(pl.multiple_of(p * ct, ct), ct)
        y = X[r, :]
        X[r, :] = y * twa_ref[pl.ds(p, 1), :] + _swap_halves(y) * twb_ref[pl.ds(p, 1), :]
        return carry

    lax.fori_loop(0, n1, blk, 0)

    def mm(i, carry):
        r = pl.ds(pl.multiple_of(i * mrows, mrows), mrows)
        X[r, :] = _dot3_w(X[r, :], fh_ref[...], fl_ref[...])
        return carry

    lax.fori_loop(0, n1 * ct // mrows, mm, 0)


def _fft_inverse(X, sta_ref, stb_ref, twa_ref, twb_ref, fh_ref, fl_ref, n1, bits, npair, ct, mrows):
    def mm(i, carry):
        r = pl.ds(pl.multiple_of(i * mrows, mrows), mrows)
        X[r, :] = _dot3_w(X[r, :], fh_ref[...], fl_ref[...])
        return carry

    lax.fori_loop(0, n1 * ct // mrows, mm, 0)

    def blk(p, carry):
        r = pl.ds(pl.multiple_of(p * ct, ct), ct)
        y = X[r, :]
        X[r, :] = y * twa_ref[pl.ds(p, 1), :] - _swap_halves(y) * twb_ref[pl.ds(p, 1), :]
        return carry

    lax.fori_loop(0, n1, blk, 0)

    for s in reversed(range(bits)):
        half = n1 >> (s + 1)

        def pair(p, carry, s=s, half=half):
            grp = p // half
            j = p - grp * half
            a = grp * 2 * half + j
            ra = pl.ds(pl.multiple_of(a * ct, ct), ct)
            rb = pl.ds(pl.multiple_of((a + half) * ct, ct), ct)
            xa = X[ra, :]
            xb = X[rb, :]
            tw = xb * sta_ref[pl.ds(s * npair + p, 1), :] - _swap_halves(xb) * stb_ref[pl.ds(s * npair + p, 1), :]
            X[ra, :] = xa + tw
            X[rb, :] = xa - tw
            return carry

        lax.fori_loop(0, npair, pair, 0)


def _conv3_lanes(x, w, b):
    L = x.shape[1]
    lane = lax.broadcasted_iota(jnp.int32, x.shape, 1)
    xp = jnp.where(lane == 0, 0.0, pltpu.roll(x, 1, 1))
    xn = jnp.where(lane == L - 1, 0.0, pltpu.roll(x, L - 1, 1))
    return xp * w[:, 0:1] + x * w[:, 1:2] + xn * w[:, 2:3] + b


def _hyena_kernel(x0a_ref, x1a_ref, va_ref, x0b_ref, x1b_ref, vb_ref, cw0_ref, cw1_ref, cw2_ref,
                  cb0_ref, cb1_ref, cb2_ref, d_ref, zt_ref, dl_ref, w1_ref, b1_ref, w2_ref, b2_ref,
                  w3f_ref, w3b_ref, fr_ref, sta_ref, stb_ref, twa_ref, twb_ref,
                  ffh_ref, ffl_ref, fih_ref, fil_ref, o_ref, X, HA, HB, *, L, n1, bits, npair, ct, mrows):
    n2 = DFT_N2
    n = 2 * L
    fft_args = (sta_ref, stb_ref, twa_ref, twb_ref)

    @pl.when(pl.program_id(1) == 0)
    def _():
        zt = zt_ref[...]
        fr = fr_ref[...]
        h = jnp.sin(fr * (_dot3(w1_ref[...], zt) + b1_ref[...]))
        h = jnp.sin(fr * (_dot3(w2_ref[...], h) + b2_ref[...]))
        hf = _dot3(w3f_ref[...], h)
        hb = _dot3(w3b_ref[...], h)
        lane = lax.broadcasted_iota(jnp.int32, hf.shape, 1)
        dec = jnp.exp(-zt[0:1, :] * dl_ref[...])
        hc = jnp.where(lane < L, hf, jnp.where(lane > L, hb, 0.0)) * dec
        for b in range(n1):
            X[b * ct:(b + 1) * ct, 0:n2] = hc[:, b * n2:(b + 1) * n2]
            X[b * ct:(b + 1) * ct, n2:2 * n2] = jnp.zeros((ct, n2), f32)
        _fft_forward(X, *fft_args, ffh_ref, ffl_ref, n1, bits, npair, ct, mrows)
        hs = X[...]
        HA[...] = jnp.concatenate([hs[:, :n2], hs[:, :n2]], axis=1)
        HB[...] = jnp.concatenate([-hs[:, n2:], hs[:, n2:]], axis=1)

    x0a = _conv3_lanes(x0a_ref[0], cw0_ref[...], cb0_ref[...])
    x0b = _conv3_lanes(x0b_ref[0], cw0_ref[...], cb0_ref[...])
    vva = _conv3_lanes(va_ref[0], cw2_ref[...], cb2_ref[...]) * _conv3_lanes(x1a_ref[0], cw1_ref[...], cb1_ref[...])
    vvb = _conv3_lanes(vb_ref[0], cw2_ref[...], cb2_ref[...]) * _conv3_lanes(x1b_ref[0], cw1_ref[...], cb1_ref[...])
    nb = L // n2
    for b in range(nb):
        X[b * ct:(b + 1) * ct, 0:n2] = vva[:, b * n2:(b + 1) * n2]
        X[b * ct:(b + 1) * ct, n2:2 * n2] = vvb[:, b * n2:(b + 1) * n2]
    X[nb * ct:n1 * ct, :] = jnp.zeros(((n1 - nb) * ct, 2 * n2), f32)
    _fft_forward(X, *fft_args, ffh_ref, ffl_ref, n1, bits, npair, ct, mrows)

    def spec(i, carry):
        r = pl.ds(pl.multiple_of(i * ct, ct), ct)
        x = X[r, :]
        X[r, :] = x * HA[r, :] + _swap_halves(x) * HB[r, :]
        return carry

    lax.fori_loop(0, n1, spec, 0)
    _fft_inverse(X, *fft_args, fih_ref, fil_ref, n1, bits, npair, ct, mrows)
    inv_n = 1.0 / n
    ya = jnp.concatenate([X[b * ct:(b + 1) * ct, 0:n2] for b in range(nb)], axis=1) * inv_n
    yb = jnp.concatenate([X[b * ct:(b + 1) * ct, n2:2 * n2] for b in range(nb)], axis=1) * inv_n
    dcol = d_ref[...]
    o_ref[0] = (ya + vva * dcol) * x0a
    o_ref[1] = (yb + vvb * dcol) * x0b


def _hyena(zhT, grp_a, grp_b, lane_a, lane_b, npairs, L, ct, hw):
    tabs = _fft_tables(L)
    n1, bits, npair = tabs["n1"], tabs["bits"], tabs["npair"]
    n = 2 * L
    ntile = HY_W // ct
    mrows = min(512, n1 * ct)
    zt, deltas = _hyena_pos_table(L)
    fwd = jnp.asarray(tabs["fwd"])
    inv = jnp.asarray(tabs["inv"])
    ffh = fwd.astype(bf16)
    ffl = (fwd - ffh.astype(f32)).astype(bf16)
    fih = inv.astype(bf16)
    fil = (inv - fih.astype(f32)).astype(bf16)
    xin = lambda k, grp, ln: pl.BlockSpec((1, ct, L), lambda j, p, k=k: (grp(p), j + ntile * k, ln(p)))
    chan = lambda k, w: pl.BlockSpec((ct, w), lambda j, p, k=k: (j + ntile * k, 0))
    full = lambda a: pl.BlockSpec(a.shape, lambda j, p: (0,) * a.ndim)
    consts = [jnp.asarray(zt), jnp.asarray(deltas)]
    in_specs = ([xin(0, grp_a, lane_a), xin(1, grp_a, lane_a), xin(2, grp_a, lane_a),
                 xin(0, grp_b, lane_b), xin(1, grp_b, lane_b), xin(2, grp_b, lane_b),
                 chan(0, 3), chan(1, 3), chan(2, 3), chan(0, 1), chan(1, 1), chan(2, 1), chan(0, 1),
                 full(consts[0]), chan(0, 1),
                 full(hw["w1T"]), full(hw["b1"]), full(hw["w2T"]), full(hw["b2"]),
                 chan(0, HY_FH), chan(1, HY_FH), full(hw["freq"])]
                + [full(jnp.asarray(tabs[k])) for k in ("sta", "stb", "twa", "twb")]
                + [full(ffh), full(ffl), full(fih), full(fil)])
    args = ([zhT] * 6 + [hw["cwT"]] * 3 + [hw["cb"]] * 3 + [hw["d"], consts[0], consts[1],
            hw["w1T"], hw["b1"], hw["w2T"], hw["b2"], hw["w3T"], hw["w3T"], hw["freq"]]
            + [jnp.asarray(tabs[k]) for k in ("sta", "stb", "twa", "twb")] + [ffh, ffl, fih, fil])
    return pl.pallas_call(
        functools.partial(_hyena_kernel, L=L, n1=n1, bits=bits, npair=npair, ct=ct, mrows=mrows),
        grid=(ntile, npairs),
        in_specs=in_specs,
        out_specs=pl.BlockSpec((2, ct, L), lambda j, p: (p, j, 0)),
        out_shape=jax.ShapeDtypeStruct((2 * npairs, HY_W, L), f32),
        scratch_shapes=[pltpu.VMEM((n1 * ct, 2 * DFT_N2), f32)] * 3,
        compiler_params=_cparams(("arbitrary", "arbitrary")),
        name="hyena",
    )(*args)


def _merge_kernel(x_ref, mod_ref, nw_ref, oa_ref, ob_ref, oc_ref, od_ref, wg_ref, wb_ref, wo_ref, xo_ref):
    g = pl.program_id(0)
    d = D_MODEL
    x = x_ref[0]
    mod = mod_ref[pl.ds(g, 1), :]
    h = _norm_mod(x, nw_ref[...], mod[:, d:2 * d], mod[:, 0:d]).astype(bf16)
    pa = _dot(oa_ref[0], wb_ref[0])
    pb = _dot(ob_ref[0, 0], wb_ref[1, 0:64, :])
    for hh in range(1, GDN_H):
        pb = pb + _dot(ob_ref[hh, 0], wb_ref[1, 64 * hh:64 * (hh + 1), :])
    pc = _dot_tn(oc_ref[0], wb_ref[2])
    pd = _dot(od_ref[0], wb_ref[3])
    acc = jnp.zeros_like(x)
    for nbr, proj in enumerate((pa, pb, pc, pd)):
        gate = _sigmoid(jnp.dot(h, wg_ref[:, nbr * d:(nbr + 1) * d], preferred_element_type=f32))
        acc = acc + gate * proj
    xo_ref[0] = x + mod[:, 2 * d:3 * d] * _dot(acc, wo_ref[...])


def _merge(x, mod_l, nw, oa, ob, oc, od, wg, wb, wo, tm):
    G, T, d = x.shape
    full = lambda shape: pl.BlockSpec(shape, lambda g, i: (0,) * len(shape))
    tok = lambda w: pl.BlockSpec((1, tm, w), lambda g, i: (g, i, 0))
    return pl.pallas_call(
        _merge_kernel,
        grid=(G, T // tm),
        in_specs=[tok(d), full(mod_l.shape), full((1, d)), tok(256),
                  pl.BlockSpec((GDN_H, 1, tm, 64), lambda g, i: (0, g, i, 0)),
                  pl.BlockSpec((1, HY_W, tm), lambda g, i: (g, 0, i)), tok(256),
                  full(wg.shape), full(wb.shape), full(wo.shape)],
        out_specs=tok(d),
        out_shape=jax.ShapeDtypeStruct((G, T, d), f32),
        compiler_params=_cparams(("arbitrary", "arbitrary")),
        name="merge",
    )(x, mod_l, nw.reshape(1, d), oa, ob, oc, od, wg, wb, wo)


def _mlp_kernel(x_ref, mod_ref, nw_ref, w1_ref, w2_ref, xo_ref):
    g = pl.program_id(0)
    d = D_MODEL
    x = x_ref[0]
    mod = mod_ref[pl.ds(g, 1), :]
    h = _norm_mod(x, nw_ref[...], mod[:, 4 * d:5 * d], mod[:, 3 * d:4 * d]).astype(bf16)
    acc = jnp.zeros_like(x)
    for c in range(D_FF // d):
        a = jnp.maximum(jnp.dot(h, w1_ref[:, c * d:(c + 1) * d], preferred_element_type=f32), 0.0)
        acc = acc + _dot(a * a, w2_ref[c * d:(c + 1) * d, :])
    xo_ref[0] = x + mod[:, 5 * d:6 * d] * acc


def _mlp(x, mod_l, nw, w1, w2, tm):
    G, T, d = x.shape
    full = lambda shape: pl.BlockSpec(shape, lambda g, i: (0,) * len(shape))
    tok = pl.BlockSpec((1, tm, d), lambda g, i: (g, i, 0))
    return pl.pallas_call(
        _mlp_kernel,
        grid=(G, T // tm),
        in_specs=[tok, full(mod_l.shape), full((1, d)), full(w1.shape), full(w2.shape)],
        out_specs=tok,
        out_shape=jax.ShapeDtypeStruct((G, T, d), f32),
        compiler_params=_cparams(("arbitrary", "arbitrary")),
        name="mlp",
    )(x, mod_l, nw.reshape(1, d), w1, w2)


def _final_kernel(x_ref, nw_ref, o_ref):
    o_ref[0] = _rms(x_ref[0], nw_ref[...])


def _final_norm(x, nw, tm):
    G, T, d = x.shape
    tok = pl.BlockSpec((1, tm, d), lambda g, i: (g, i, 0))
    return pl.pallas_call(
        _final_kernel,
        grid=(G, T // tm),
        in_specs=[tok, pl.BlockSpec((1, d), lambda g, i: (0, 0))],
        out_specs=tok,
        out_shape=jax.ShapeDtypeStruct((G, T, d), f32),
        compiler_params=_cparams(("arbitrary", "arbitrary")),
        name="final_norm",
    )(x, nw.reshape(1, d))


@functools.lru_cache(maxsize=None)
def _rope_tables(T):
    m = MLA_ROPE // 4
    inv = ROPE_BASE ** (-np.arange(m, dtype=np.float64) / m)
    rows = T // GRID_W
    row_pos = np.repeat(np.arange(rows), GRID_W)[:, None] * inv
    col_pos = np.tile(np.arange(GRID_W), rows)[:, None] * inv
    cos32 = np.concatenate([np.cos(row_pos), np.cos(row_pos), np.cos(col_pos), np.cos(col_pos)], axis=1)
    sin32 = np.concatenate([-np.sin(row_pos), np.sin(row_pos), -np.sin(col_pos), np.sin(col_pos)], axis=1)
    ca = np.zeros((2, T, 128))
    cb = np.zeros((2, T, 128))
    ca[:, :, 0:96] = 1.0
    ca[1, :, 64:96] = cos32
    cb[1, :, 64:96] = sin32
    cd = np.ones((2, T, 256))
    sd = np.zeros((2, T, 256))
    cd[1] = np.tile(cos32, (1, 8))
    sd[1] = np.tile(sin32, (1, 8))
    return tuple(np.asarray(a, np.float32) for a in (ca, cb, cd, sd))


def _swap_perm(width):
    base = np.concatenate([np.arange(8, 16), np.arange(0, 8), np.arange(24, 32), np.arange(16, 24)])
    return np.concatenate([base + 32 * s for s in range(width // 32)])


def _pack_weights(w_in, mla_w_uq, mla_w_ukv, gdn_conv_w, gdn_a_log, gdn_dt_bias, hy_conv_w, hy_conv_b,
                  hy_f_w1, hy_f_b1, hy_f_w2, hy_f_b2, hy_f_w3, hy_f_freq, hy_d):
    depth = w_in.shape[0]
    offs = [0] + [int(s) for s in np.cumsum(IN_SPLITS)]
    seg = lambda i: w_in[:, :, offs[i]:offs[i + 1]]
    zeros = lambda n: jnp.zeros((depth, D_MODEL, n), w_in.dtype)
    kpe = seg(2)
    kpe_sw = kpe[:, :, _swap_perm(MLA_ROPE)]
    wa = jnp.concatenate([seg(0), seg(1), zeros(64), kpe, zeros(32), zeros(64), kpe_sw, zeros(32)], axis=2)
    dq, dk, dv = seg(10), seg(11), seg(12)
    perm = _swap_perm(256)
    wd = jnp.concatenate([dq, dq[:, :, perm], dk, dk[:, :, perm], dv], axis=2)
    wg = jnp.concatenate([seg(3), seg(4), seg(5), seg(6)], axis=2)
    wab = jnp.concatenate([seg(7), seg(8), zeros(128 - 4 * GDN_H)], axis=2)
    whT = jnp.swapaxes(seg(9), 1, 2)
    wgate = seg(13)
    uq = mla_w_uq.reshape(depth, MLA_Q_LORA, MLA_H, MLA_NOPE + MLA_ROPE)
    z32 = jnp.zeros((depth, MLA_Q_LORA, MLA_H, 32), uq.dtype)
    z64 = jnp.zeros((depth, MLA_Q_LORA, MLA_H, 64), uq.dtype)
    rope_sw = uq[..., MLA_NOPE:][..., _swap_perm(MLA_ROPE)]
    wqa = jnp.concatenate([uq, z32], axis=3).reshape(depth, MLA_Q_LORA, MLA_H * 128)
    wqb = jnp.concatenate([z64, rope_sw, z32], axis=3).reshape(depth, MLA_Q_LORA, MLA_H * 128)
    ukv = mla_w_ukv.reshape(depth, MLA_KV_LORA, MLA_H, MLA_NOPE + MLA_V)
    wka = jnp.concatenate([ukv[..., :MLA_NOPE], jnp.zeros((depth, MLA_KV_LORA, MLA_H, 64), ukv.dtype)],
                          axis=3).reshape(depth, MLA_KV_LORA, MLA_H * 128)
    wv = ukv[..., MLA_NOPE:].reshape(depth, MLA_KV_LORA, MLA_H * MLA_V)
    cast = lambda a: a.astype(bf16)
    pad128 = lambda a: jnp.pad(a.reshape(depth, 1, -1), ((0, 0), (0, 0), (0, 128 - a.shape[1] * a.shape[2])))
    hy = dict(
        cwT=jnp.swapaxes(hy_conv_w, 1, 2),
        cb=hy_conv_b.reshape(depth, -1, 1),
        d=hy_d.reshape(depth, HY_W, 1),
        w1T=jnp.pad(jnp.swapaxes(hy_f_w1, 1, 2), ((0, 0), (0, 0), (0, LANES - HY_EMB))),
        b1=hy_f_b1.reshape(depth, HY_FH, 1),
        w2T=jnp.swapaxes(hy_f_w2, 1, 2),
        b2=hy_f_b2.reshape(depth, HY_FH, 1),
        w3T=jnp.swapaxes(hy_f_w3, 1, 2),
        freq=hy_f_freq.reshape(depth, HY_FH, 1),
    )
    return dict(wa=cast(wa), wd=cast(wd), wg=cast(wg), wab=cast(wab), whT=cast(whT), wgate=cast(wgate),
                wqa=cast(wqa), wqb=cast(wqb), wka=cast(wka), wv=cast(wv),
                gcw=jnp.swapaxes(gdn_conv_w.reshape(depth, 3, 12, 64), 1, 2),
                alog=pad128(gdn_a_log), dtb=pad128(gdn_dt_bias), hy=hy)


def kernel(x_prompt, x_sample, cache_mla_ckv, cache_mla_kpe, cache_diff_k, cache_diff_v, state_gdn, c, c_ctx, w_ada, b_ada, norm_mix_w, norm_mlp_w, w_in, mla_q_norm_w, mla_w_uq, mla_kv_norm_w, mla_w_ukv, gdn_conv_w, gdn_a_log, gdn_dt_bias, gdn_norm_w, hy_conv_w, hy_conv_b, hy_f_w1, hy_f_b1, hy_f_w2, hy_f_b2, hy_f_w3, hy_f_freq, hy_d, da_lq1, da_lk1, da_lq2, da_lk2, da_norm_w, w_branch, w_out, mlp_w1, mlp_w2, final_norm_w):
    depth = w_in.shape[0]
    bc, lc, d = x_prompt.shape
    bl, ll, _ = x_sample.shape
    T = ll
    assert bc * lc == T and d == D_MODEL and bc % 2 == 0 and bl == 2
    G = 1 + bl
    past = cache_mla_ckv.shape[2]
    tm = min(512, T)
    tq = min(256, lc)

    pk = _pack_weights(w_in, mla_w_uq, mla_w_ukv, gdn_conv_w, gdn_a_log, gdn_dt_bias, hy_conv_w, hy_conv_b,
                       hy_f_w1, hy_f_b1, hy_f_w2, hy_f_b2, hy_f_w3, hy_f_freq, hy_d)
    wb_bf = w_branch.astype(bf16)
    wo_bf = w_out.astype(bf16)
    w1_bf = mlp_w1.astype(bf16)
    w2_bf = mlp_w2.astype(bf16)
    ca, cb, cd, sd = (jnp.asarray(t) for t in _rope_tables(T))

    cond8 = jnp.concatenate([c_ctx.reshape(1, d), c, jnp.zeros((8 - G, d), f32)], axis=0)
    mod = _modulation(cond8, w_ada, b_ada)

    kpe_pad = jnp.pad(cache_mla_kpe, ((0, 0), (0, 0), (0, 0), (MLA_NOPE, 128 - MLA_NOPE - MLA_ROPE)))
    kc_mla, vc_mla = _cache_kv(cache_mla_ckv, kpe_pad, pk["wka"], pk["wv"])
    kc_da = jnp.transpose(cache_diff_k.reshape(bl, depth, past, DA_H, 2 * DA_DK), (1, 3, 0, 2, 4)).astype(bf16)
    vc_da = jnp.transpose(cache_diff_v, (1, 3, 0, 2, 4)).astype(bf16)

    x = jnp.concatenate([x_prompt.reshape(1, T, d), x_sample], axis=0)
    new_ckv, new_kpe, new_dk, new_dv, new_state = [], [], [], [], []
    for l in range(depth):
        lam_init = 0.8 - 0.6 * math.exp(-0.3 * l)
        za, zd, zg, zab, zhT = _in_proj(x, mod[l], norm_mix_w[l], pk["wa"][l], pk["wd"][l], pk["wg"][l],
                                        pk["wab"][l], pk["whT"][l], tm)
        q, k, v, ckv, dq, dk, dv = _prep(za, zd, ca, cb, cd, sd, mla_q_norm_w[l], mla_kv_norm_w[l],
                                         pk["wqa"][l], pk["wqb"][l], pk["wka"][l], pk["wv"][l], tm)
        new_ckv.append(ckv[0].reshape(bc, lc, MLA_KV_LORA))
        new_kpe.append(za[0, :, 384 + MLA_NOPE:384 + MLA_NOPE + MLA_ROPE].reshape(bc, lc, MLA_ROPE))
        new_dk.append(zd[0, :, 512:768].reshape(bc, lc, DA_H, 2, DA_DK))
        new_dv.append(zd[0, :, 1024:1280].reshape(bc, lc, DA_H, DA_DV))

        ctx_view = lambda a: a.reshape(a.shape[0], G * bc, lc, a.shape[3])
        oa_c = _mla_attention(ctx_view(q), ctx_view(k), ctx_view(v), 0, bc, lc, tq, lc)
        oa_l = _mla_attention(q, k, v, 1, bl, ll, tq, min(512, ll), cache=(kc_mla[l], vc_mla[l]))
        oa = jnp.concatenate([oa_c.reshape(1, T, -1), oa_l], axis=0)

        lamv = jnp.concatenate([da_lq1[l][None], da_lk1[l][None], da_lq2[l][None], da_lk2[l][None],
                                jnp.full((1, DA_DK), lam_init, f32), jnp.zeros((3, DA_DK), f32)], axis=0)
        od_c = _diff_attention(ctx_view(dq), ctx_view(dk), ctx_view(dv), lamv, da_norm_w[l], 0, bc, lc, tq, lc)
        od_l = _diff_attention(dq, dk, dv, lamv, da_norm_w[l], 1, bl, ll, tq, min(512, ll),
                               cache=(kc_da[l], vc_da[l]))
        od = jnp.concatenate([od_c.reshape(1, T, -1), od_l], axis=0)

        zg_ctx = zg.reshape(zg.shape[0], G * bc, lc, 64)
        zab_ctx = zab.reshape(G * bc, lc, 128)
        ob_c, s_gdn = _gdn(zg_ctx, zab_ctx, pk["gcw"][l], pk["alog"][l], pk["dtb"][l], gdn_norm_w[l], 0, bc, lc)
        ob_l = _gdn(zg, zab, pk["gcw"][l], pk["alog"][l], pk["dtb"][l], gdn_norm_w[l], 1, bl, ll,
                    s0=state_gdn, layer=l)[0]
        ob = jnp.concatenate([ob_c.reshape(GDN_H, 1, T, 64), ob_l], axis=1)
        new_state.append(s_gdn)

        hw = {name: val[l] for name, val in pk["hy"].items()}
        oc_c = _hyena(zhT, lambda p: 0, lambda p: 0, lambda p: 2 * p, lambda p: 2 * p + 1, bc // 2, lc, 128, hw)
        oc_l = _hyena(zhT, lambda p: 1, lambda p: 2, lambda p: 0, lambda p: 0, 1, ll, 64, hw)
        oc_c = jnp.transpose(oc_c, (1, 0, 2)).reshape(1, HY_W, T)
        oc = jnp.concatenate([oc_c, oc_l], axis=0)

        x = _merge(x, mod[l], norm_mix_w[l], oa, ob, oc, od, pk["wgate"][l], wb_bf[l], wo_bf[l], tm)
        x = _mlp(x, mod[l], norm_mlp_w[l], w1_bf[l], w2_bf[l], tm)

    y = _final_norm(x, final_norm_w, tm)
    y_prompt = y[0].reshape(bc, lc, d)
    y_sample = y[1:]
    return (y_prompt, y_sample, jnp.stack(new_ckv, axis=1), jnp.stack(new_kpe, axis=1), jnp.stack(new_dk, axis=1),
            jnp.stack(new_dv, axis=1), jnp.stack(new_state, axis=1))
```

```python
import functools
import math

import numpy as np
import jax
import jax.numpy as jnp
from jax import lax
from jax.experimental import pallas as pl
from jax.experimental.pallas import tpu as pltpu

f32 = jnp.float32
bf16 = jnp.bfloat16

D_MODEL = 1024
GRID_W = 64
N_BRANCH = 4
MLA_H = 4
MLA_NOPE = 64
MLA_ROPE = 32
MLA_V = 64
MLA_Q_LORA = 256
MLA_KV_LORA = 128
GDN_H = 4
GDN_DK = 64
GDN_DV = 64
GDN_CHUNK = 64
HY_W = 256
HY_BANDS = 16
HY_EMB = 1 + 2 * HY_BANDS
HY_FH = 64
HY_SLOW_DECAY = math.log(1e-2) / 1.5
HY_FAST_DECAY = math.log(1e-2) / 0.3
DA_H = 4
DA_DK = 32
DA_DV = 64
D_FF = 4 * D_MODEL
ROPE_BASE = 10000.0
EPS = 1e-6
IN_SPLITS = (MLA_Q_LORA, MLA_KV_LORA, MLA_ROPE,
             GDN_H * GDN_DK, GDN_H * GDN_DK, GDN_H * GDN_DV, GDN_H * GDN_DV, 2 * GDN_H, 2 * GDN_H,
             3 * HY_W,
             DA_H * 2 * DA_DK, DA_H * 2 * DA_DK, DA_H * DA_DV,
             N_BRANCH * D_MODEL)

LOG2E = math.log2(math.e)
LANES = 128
V_ONE = 64
DFT_N2 = 256
VMEM_LIMIT = 56 * 1024 * 1024


def _cparams(sem):
    return pltpu.CompilerParams(dimension_semantics=sem, vmem_limit_bytes=VMEM_LIMIT)


def _dot(a, b):
    return jnp.dot(a.astype(bf16), b.astype(bf16), preferred_element_type=f32)


def _dot_nt(a, b):
    return lax.dot_general(a.astype(bf16), b.astype(bf16), (((1,), (1,)), ((), ())), preferred_element_type=f32)


def _dot_tn(a, b):
    return lax.dot_general(a.astype(bf16), b.astype(bf16), (((0,), (0,)), ((), ())), preferred_element_type=f32)


def _split(x):
    hi = x.astype(bf16)
    lo = (x - hi.astype(f32)).astype(bf16)
    return hi, lo


def _dot3(a, b):
    ah, al = _split(a)
    bh, bl = _split(b)
    return (jnp.dot(ah, bh, preferred_element_type=f32) + jnp.dot(ah, bl, preferred_element_type=f32)
            + jnp.dot(al, bh, preferred_element_type=f32))


def _dot3_w(a, bh, bl):
    ah, al = _split(a)
    return (jnp.dot(ah, bh, preferred_element_type=f32) + jnp.dot(ah, bl, preferred_element_type=f32)
            + jnp.dot(al, bh, preferred_element_type=f32))


def _sigmoid(x):
    return 1.0 / (1.0 + jnp.exp(-x))


def _silu(x):
    return x * _sigmoid(x)


def _softplus(x):
    return jnp.maximum(x, 0.0) + jnp.log(1.0 + jnp.exp(-jnp.abs(x)))


def _rms(x, w):
    return x * lax.rsqrt(jnp.mean(x * x, axis=-1, keepdims=True) + EPS) * w


def _mod_kernel(c_ref, w_ref, b_ref, o_ref):
    c = _silu(c_ref[...])
    o_ref[0] = _dot3(c, w_ref[0]) + b_ref[0]


def _modulation(cond8, w_ada, b_ada):
    depth, d, n6 = w_ada.shape
    tn = 1536
    return pl.pallas_call(
        _mod_kernel,
        grid=(depth, n6 // tn),
        in_specs=[pl.BlockSpec((8, d), lambda l, j: (0, 0)),
                  pl.BlockSpec((1, d, tn), lambda l, j: (l, 0, j)),
                  pl.BlockSpec((1, 1, tn), lambda l, j: (l, 0, j))],
        out_specs=pl.BlockSpec((1, 8, tn), lambda l, j: (l, 0, j)),
        out_shape=jax.ShapeDtypeStruct((depth, 8, n6), f32),
        compiler_params=_cparams(("arbitrary", "arbitrary")),
        name="modulation",
    )(cond8, w_ada, b_ada.reshape(depth, 1, n6))


def _norm_mod(x, nw, scale, shift):
    return _rms(x, nw) * (1.0 + scale) + shift


def _in_kernel(x_ref, mod_ref, nw_ref, wa_ref, wd_ref, wg_ref, wab_ref, whT_ref,
               za_ref, zd_ref, zg_ref, zab_ref, zhT_ref):
    g = pl.program_id(0)
    d = D_MODEL
    mod = mod_ref[pl.ds(g, 1), :]
    h = _norm_mod(x_ref[0], nw_ref[...], mod[:, d:2 * d], mod[:, 0:d]).astype(bf16)
    za_ref[0] = jnp.dot(h, wa_ref[...], preferred_element_type=f32)
    zd_ref[0] = jnp.dot(h, wd_ref[...], preferred_element_type=f32)
    zg_ref[0] = jnp.dot(h, wg_ref[...], preferred_element_type=f32)
    zab_ref[0] = jnp.dot(h, wab_ref[...], preferred_element_type=f32)
    zhT_ref[0] = lax.dot_general(whT_ref[...], h, (((1,), (1,)), ((), ())), preferred_element_type=f32)


def _in_proj(x, mod_l, nw, wa, wd, wg, wab, whT, tm):
    G, T, d = x.shape
    na, nd, ng, nab, nh = wa.shape[1], wd.shape[1], wg.shape[1], wab.shape[1], whT.shape[0]
    full = lambda shape: pl.BlockSpec(shape, lambda g, i: (0,) * len(shape))
    return pl.pallas_call(
        _in_kernel,
        grid=(G, T // tm),
        in_specs=[pl.BlockSpec((1, tm, d), lambda g, i: (g, i, 0)),
                  full(mod_l.shape), full((1, d)), full(wa.shape), full(wd.shape), full(wg.shape),
                  full(wab.shape), full(whT.shape)],
        out_specs=[pl.BlockSpec((1, tm, na), lambda g, i: (g, i, 0)),
                   pl.BlockSpec((1, tm, nd), lambda g, i: (g, i, 0)),
                   pl.BlockSpec((1, tm, ng), lambda g, i: (g, i, 0)),
                   pl.BlockSpec((1, tm, nab), lambda g, i: (g, i, 0)),
                   pl.BlockSpec((1, nh, tm), lambda g, i: (g, 0, i))],
        out_shape=[jax.ShapeDtypeStruct((G, T, na), f32), jax.ShapeDtypeStruct((G, T, nd), f32),
                   jax.ShapeDtypeStruct((G, T, ng), f32), jax.ShapeDtypeStruct((G, T, nab), f32),
                   jax.ShapeDtypeStruct((G, nh, T), f32)],
        compiler_params=_cparams(("arbitrary", "arbitrary")),
        name="in_proj",
    )(x, mod_l, nw.reshape(1, d), wa, wd, wg, wab, whT)


def _ones_col(rows):
    return jnp.where(lax.broadcasted_iota(jnp.int32, (rows, LANES), 1) == V_ONE, 1.0, 0.0)


def _prep_kernel(za_ref, zd_ref, ca_ref, cb_ref, cd_ref, sd_ref, qnw_ref, kvnw_ref, wqa_ref, wqb_ref, wka_ref, wv_ref,
                 q_ref, k_ref, v_ref, ckv_ref, dq1_ref, dq2_ref, dk_ref, dv_ref):
    za = za_ref[0]
    cqn = _rms(za[:, 0:256], qnw_ref[...]).astype(bf16)
    ckv = _rms(za[:, 256:384], kvnw_ref[...])
    ckv_ref[0] = ckv
    ckvb = ckv.astype(bf16)
    ca = ca_ref[0]
    cb = cb_ref[0]
    qa = jnp.dot(cqn, wqa_ref[...], preferred_element_type=f32)
    qb = jnp.dot(cqn, wqb_ref[...], preferred_element_type=f32)
    kn = jnp.dot(ckvb, wka_ref[...], preferred_element_type=f32)
    vv = jnp.dot(ckvb, wv_ref[...], preferred_element_type=f32)
    krope = za[:, 384:512] * ca + za[:, 512:640] * cb
    qs = (MLA_NOPE + MLA_ROPE) ** -0.5 * LOG2E
    ones_col = _ones_col(za.shape[0])
    for h in range(MLA_H):
        sl = slice(128 * h, 128 * (h + 1))
        q_ref[h, 0] = ((qa[:, sl] * ca + qb[:, sl] * cb) * qs).astype(bf16)
        k_ref[h, 0] = (kn[:, sl] + krope).astype(bf16)
        v_ref[h, 0] = (vv[:, sl] + ones_col).astype(bf16)
    zd = zd_ref[0]
    cd = cd_ref[0]
    sd = sd_ref[0]
    dqs = DA_DK ** -0.5 * LOG2E
    dq = (zd[:, 0:256] * cd + zd[:, 256:512] * sd) * dqs
    dk = zd[:, 512:768] * cd + zd[:, 768:1024] * sd
    first = (lax.broadcasted_iota(jnp.int32, dq.shape, 1) & (2 * DA_DK - 1)) < DA_DK
    dq1 = jnp.where(first, dq, 0.0)
    dq2 = jnp.where(first, 0.0, dq)
    for h in range(DA_H):
        sl = slice(64 * h, 64 * (h + 1))
        dq1_ref[h, 0] = dq1[:, sl].astype(bf16)
        dq2_ref[h, 0] = dq2[:, sl].astype(bf16)
        dk_ref[h, 0] = dk[:, sl].astype(bf16)
        dv_ref[h, 0] = (zd[:, 1024 + 128 * h:1024 + 128 * (h + 1)] + ones_col).astype(bf16)


def _prep(za, zd, ca, cb, cd, sd, qnw, kvnw, wqa, wqb, wka, wv, tm):
    G, T, na = za.shape
    nd = zd.shape[2]
    full = lambda shape: pl.BlockSpec(shape, lambda g, i: (0,) * len(shape))
    tab = lambda w: pl.BlockSpec((1, tm, w), lambda g, i: (jnp.minimum(g, 1), i, 0))
    hm = lambda w: pl.BlockSpec((4, 1, tm, w), lambda g, i: (0, g, i, 0))
    hms = lambda w: jax.ShapeDtypeStruct((4, G, T, w), bf16)
    return pl.pallas_call(
        _prep_kernel,
        grid=(G, T // tm),
        in_specs=[pl.BlockSpec((1, tm, na), lambda g, i: (g, i, 0)),
                  pl.BlockSpec((1, tm, nd), lambda g, i: (g, i, 0)),
                  tab(128), tab(128), tab(256), tab(256),
                  full((1, 256)), full((1, 128)), full(wqa.shape), full(wqb.shape), full(wka.shape), full(wv.shape)],
        out_specs=[hm(128), hm(128), hm(128), pl.BlockSpec((1, tm, 128), lambda g, i: (g, i, 0)),
                   hm(64), hm(64), hm(64), hm(128)],
        out_shape=[hms(128), hms(128), hms(128), jax.ShapeDtypeStruct((G, T, 128), f32),
                   hms(64), hms(64), hms(64), hms(128)],
        compiler_params=_cparams(("arbitrary", "arbitrary")),
        name="attn_prep",
    )(za, zd, ca, cb, cd, sd, qnw.reshape(1, -1), kvnw.reshape(1, -1), wqa, wqb, wka, wv)


def _cache_kv_kernel(ckv_ref, kpe_ref, wka_ref, wv_ref, k_ref, v_ref):
    ckvb = ckv_ref[0, 0].astype(bf16)
    kn = jnp.dot(ckvb, wka_ref[0], preferred_element_type=f32)
    vv = jnp.dot(ckvb, wv_ref[0], preferred_element_type=f32)
    kpe = kpe_ref[0, 0]
    ones_col = _ones_col(kpe.shape[0])
    for h in range(MLA_H):
        sl = slice(128 * h, 128 * (h + 1))
        k_ref[0, h, 0] = (kn[:, sl] + kpe).astype(bf16)
        v_ref[0, h, 0] = (vv[:, sl] + ones_col).astype(bf16)


def _cache_kv(cache_ckv, cache_kpe_pad, wka, wv):
    bl, depth, p, _ = cache_ckv.shape
    return pl.pallas_call(
        _cache_kv_kernel,
        grid=(depth, bl),
        in_specs=[pl.BlockSpec((1, 1, p, 128), lambda l, b: (b, l, 0, 0)),
                  pl.BlockSpec((1, 1, p, 128), lambda l, b: (b, l, 0, 0)),
                  pl.BlockSpec((1,) + wka.shape[1:], lambda l, b: (l, 0, 0)),
                  pl.BlockSpec((1,) + wv.shape[1:], lambda l, b: (l, 0, 0))],
        out_specs=[pl.BlockSpec((1, 4, 1, p, 128), lambda l, b: (l, 0, b, 0, 0)),
                   pl.BlockSpec((1, 4, 1, p, 128), lambda l, b: (l, 0, b, 0, 0))],
        out_shape=[jax.ShapeDtypeStruct((depth, 4, bl, p, 128), bf16),
                   jax.ShapeDtypeStruct((depth, 4, bl, p, 128), bf16)],
        compiler_params=_cparams(("arbitrary", "arbitrary")),
        name="cache_kv",
    )(cache_ckv, cache_kpe_pad, wka, wv)


def _attn_scan(q_refs, k_ref, v_ref, cache_refs, m_sc, acc_sc, kc):
    nh = k_ref.shape[0]
    lk = k_ref.shape[2]
    m_sc[...] = jnp.full(m_sc.shape, -jnp.inf, f32)
    acc_sc[...] = jnp.zeros(acc_sc.shape, f32)

    def step(get_k, get_v):
        chains = [(j * nh + h, q_ref, h) for j, q_ref in enumerate(q_refs) for h in range(nh)]
        ss = [lax.dot_general(q_ref[h, 0], get_k(h), (((1,), (1,)), ((), ())), preferred_element_type=f32)
              for _, q_ref, h in chains]
        m_olds = [m_sc[c] for c, _, _ in chains]
        m_news = [jnp.maximum(m_old, jnp.max(s, axis=1, keepdims=True)) for m_old, s in zip(m_olds, ss)]
        ps = [jnp.exp2(s - m_new).astype(bf16) for s, m_new in zip(ss, m_news)]
        pvs = [jnp.dot(p, get_v(h), preferred_element_type=f32) for p, (_, _, h) in zip(ps, chains)]
        for (c, _, _), m_old, m_new, pv in zip(chains, m_olds, m_news, pvs):
            acc_sc[c] = jnp.exp2(m_old - m_new) * acc_sc[c] + pv
            m_sc[c] = m_new

    def body(i, carry):
        rows = pl.ds(pl.multiple_of(i * kc, kc), kc)
        step(lambda h: k_ref[h, 0, rows, :], lambda h: v_ref[h, 0, rows, :])
        return carry

    lax.fori_loop(0, lk // kc, body, 0)
    if cache_refs is not None:
        kc_ref, vc_ref = cache_refs
        step(lambda h: kc_ref[h, 0], lambda h: vc_ref[h, 0])


def _attn_out(acc):
    return acc[:, 0:V_ONE] / acc[:, V_ONE:V_ONE + 1]


def _mla_attn_kernel(*refs, kc, has_cache):
    if has_cache:
        q_ref, k_ref, v_ref, kc_ref, vc_ref, o_ref, m_sc, acc_sc = refs
        cache_refs = (kc_ref, vc_ref)
    else:
        q_ref, k_ref, v_ref, o_ref, m_sc, acc_sc = refs
        cache_refs = None
    _attn_scan((q_ref,), k_ref, v_ref, cache_refs, m_sc, acc_sc, kc)
    for h in range(MLA_H):
        o_ref[0, :, MLA_V * h:MLA_V * (h + 1)] = _attn_out(acc_sc[h])


def _mla_attention(q, k, v, seq_off, nseq, L, tq, kc, cache=None):
    H = q.shape[0]
    kv = lambda rows, off: pl.BlockSpec((H, 1, rows, 128), lambda b, i: (0, b + off, 0, 0))
    in_specs = [pl.BlockSpec((H, 1, tq, 128), lambda b, i: (0, b + seq_off, i, 0)), kv(L, seq_off), kv(L, seq_off)]
    args = [q, k, v]
    if cache is not None:
        in_specs += [kv(cache[0].shape[2], 0), kv(cache[0].shape[2], 0)]
        args += list(cache)
    return pl.pallas_call(
        functools.partial(_mla_attn_kernel, kc=kc, has_cache=cache is not None),
        grid=(nseq, L // tq),
        in_specs=in_specs,
        out_specs=pl.BlockSpec((1, tq, H * MLA_V), lambda b, i: (b, i, 0)),
        out_shape=jax.ShapeDtypeStruct((nseq, L, H * MLA_V), f32),
        scratch_shapes=[pltpu.VMEM((H, tq, 1), f32), pltpu.VMEM((H, tq, 128), f32)],
        compiler_params=_cparams(("arbitrary", "arbitrary")),
        name="mla_attn",
    )(*args)


def _diff_attn_kernel(*refs, kc, has_cache):
    if has_cache:
        q1_ref, q2_ref, k_ref, v_ref, kc_ref, vc_ref, lam_ref, nw_ref, o_ref, m_sc, acc_sc = refs
        cache_refs = (kc_ref, vc_ref)
    else:
        q1_ref, q2_ref, k_ref, v_ref, lam_ref, nw_ref, o_ref, m_sc, acc_sc = refs
        cache_refs = None
    lamv = lam_ref[...]
    lam_init = lamv[4:5, 0:1]
    lam = (jnp.exp(jnp.sum(lamv[0:1] * lamv[1:2], axis=1, keepdims=True))
           - jnp.exp(jnp.sum(lamv[2:3] * lamv[3:4], axis=1, keepdims=True)) + lam_init)
    _attn_scan((q1_ref, q2_ref), k_ref, v_ref, cache_refs, m_sc, acc_sc, kc)
    for h in range(DA_H):
        o = _attn_out(acc_sc[h]) - lam * _attn_out(acc_sc[DA_H + h])
        o_ref[0, :, DA_DV * h:DA_DV * (h + 1)] = _rms(o, nw_ref[...]) * (1.0 - lam_init)


def _diff_attention(q1, q2, k, v, lamv, nw, seq_off, nseq, L, tq, kc, cache=None):
    H = q1.shape[0]
    hm = lambda rows, w, off: pl.BlockSpec((H, 1, rows, w), lambda b, i: (0, b + off, 0, 0))
    qs = pl.BlockSpec((H, 1, tq, 64), lambda b, i: (0, b + seq_off, i, 0))
    in_specs = [qs, qs, hm(L, 64, seq_off), hm(L, 128, seq_off)]
    args = [q1, q2, k, v]
    if cache is not None:
        p = cache[0].shape[2]
        in_specs += [hm(p, 64, 0), hm(p, 128, 0)]
        args += list(cache)
    in_specs += [pl.BlockSpec((8, DA_DK), lambda b, i: (0, 0)), pl.BlockSpec((1, DA_DV), lambda b, i: (0, 0))]
    args += [lamv, nw.reshape(1, DA_DV)]
    return pl.pallas_call(
        functools.partial(_diff_attn_kernel, kc=kc, has_cache=cache is not None),
        grid=(nseq, L // tq),
        in_specs=in_specs,
        out_specs=pl.BlockSpec((1, tq, H * DA_DV), lambda b, i: (b, i, 0)),
        out_shape=jax.ShapeDtypeStruct((nseq, L, H * DA_DV), f32),
        scratch_shapes=[pltpu.VMEM((2 * H, tq, 1), f32), pltpu.VMEM((2 * H, tq, 128), f32)],
        compiler_params=_cparams(("arbitrary", "arbitrary")),
        name="diff_attn",
    )(*args)


GDN_N = GDN_H * GDN_CHUNK
GDN_INV_BASE = 8
(M_BD, M_EYE, M_BASE, M_OFF8, M_OFF16, M_OFF32, M_DIR) = range(7)
GDN_NMASK = M_DIR + 6


def _gdn_fill_masks(msk):
    n, c = GDN_N, GDN_CHUNK
    rr = lax.broadcasted_iota(jnp.int32, (n, n), 0)
    cc = lax.broadcasted_iota(jnp.int32, (n, n), 1)
    blk = lambda x, s: lax.shift_right_logical(x, int(math.log2(s)))
    bd = blk(rr, c) == blk(cc, c)
    ri = rr & (c - 1)
    cj = cc & (c - 1)

    def put(i, cond):
        msk[i] = jnp.where(cond, 1.0, 0.0)

    put(M_BD, bd)
    put(M_EYE, rr == cc)
    put(M_BASE, blk(rr, GDN_INV_BASE) == blk(cc, GDN_INV_BASE))
    for slot, s in ((M_OFF8, 8), (M_OFF16, 16), (M_OFF32, 32)):
        put(slot, (blk(rr, 2 * s) == blk(cc, 2 * s)) & (blk(rr, s) != blk(cc, s)))
    put(M_DIR + 0, bd & (ri >= cj))
    put(M_DIR + 1, bd & (ri > cj))
    put(M_DIR + 2, bd & (ri <= cj))
    put(M_DIR + 3, bd & (ri <= cj))
    put(M_DIR + 4, bd & (ri < cj))
    put(M_DIR + 5, bd & (ri >= cj))


def _gdn_conv_chunk(x_ref, w, r, L):
    c = GDN_CHUNK
    x = x_ref[0, pl.ds(r, c), :]
    prev8 = x_ref[0, pl.ds(pl.multiple_of(jnp.maximum(r - 8, 0), 8), 8), :]
    next8 = x_ref[0, pl.ds(pl.multiple_of(jnp.minimum(r + c, L - 8), 8), 8), :]
    prev = jnp.where(r > 0, prev8[7:8, :], 0.0)
    nxt = jnp.where(r + c < L, next8[0:1, :], 0.0)
    row = lax.broadcasted_iota(jnp.int32, x.shape, 0)
    xp = jnp.where(row == 0, prev, pltpu.roll(x, 1, 0))
    xn = jnp.where(row == c - 1, nxt, pltpu.roll(x, c - 1, 0))
    y = xp * w[0:1] + x * w[1:2] + xn * w[2:3]
    return _silu(y)


def _group_sum(y, bdb):
    hi, lo = _split(y)
    return jnp.dot(hi, bdb, preferred_element_type=f32) + jnp.dot(lo, bdb, preferred_element_type=f32)


def _gdn_chunk(xq, xk, xv, gab, alog, dtb, direction, state, msk):
    n = GDN_N
    bd = msk[M_BD]
    bdb = bd.astype(bf16)
    incl, strict, incl_t = msk[M_DIR + 3 * direction], msk[M_DIR + 3 * direction + 1], msk[M_DIR + 3 * direction + 2]
    q = xq * lax.rsqrt(_group_sum(xq * xq, bdb) + EPS) * (GDN_DK ** -0.5)
    k = xk * lax.rsqrt(_group_sum(xk * xk, bdb) + EPS)
    lane = lax.broadcasted_iota(jnp.int32, gab.shape, 1)
    gfull = -jnp.exp(alog) * _softplus(gab + dtb)
    bfull = _sigmoid(gab)

    def stack(arr, base):
        return jnp.concatenate([jnp.sum(jnp.where(lane == base + h, arr, 0.0), axis=1, keepdims=True)
                                for h in range(GDN_H)], axis=0)

    g_stack = stack(gfull, direction * GDN_H)
    b_stack = stack(bfull, (2 + direction) * GDN_H)
    g_b = jnp.broadcast_to(g_stack, (n, n))
    g_row = jnp.sum(g_b * msk[M_EYE], axis=0, keepdims=True)
    gc_row = jnp.sum(g_b * incl_t, axis=0, keepdims=True)
    g_rb = jnp.broadcast_to(g_row, (n, n))
    gc_col = jnp.sum(g_rb * incl, axis=1, keepdims=True)
    g_last = jnp.sum(g_rb * bd, axis=1, keepdims=True)
    e = jnp.exp(jnp.minimum(gc_col - gc_row, 0.0))
    tile4 = lambda x: jnp.concatenate([x] * GDN_H, axis=0)
    k_bd = tile4(k) * bd
    q_bd = tile4(q) * bd
    v_bd = tile4(xv) * bd
    kb_bd = k_bd * b_stack
    tri = _dot_nt(kb_bd, k_bd) * (e * strict)
    nm = -(tri * msk[M_BASE])
    t = msk[M_EYE] + nm
    p = nm
    for _ in range(2):
        p = _dot(p, p)
        t = t + _dot(t, p)
    for slot in (M_OFF8, M_OFF16, M_OFF32):
        t = t - _dot(_dot(t, tri * msk[slot]), t)
    egc = jnp.exp(gc_col)
    u = _dot(t, v_bd * b_stack)
    w = _dot(t, kb_bd * egc)
    intra = _dot_nt(q_bd, k_bd) * (e * incl)
    v_new = u - _dot(w, state)
    o_bd = _dot(q_bd * egc, state) + _dot(intra, v_new)
    state = state * jnp.exp(g_last) + _dot_tn(k_bd * jnp.exp(g_last - gc_col), v_new)
    c = GDN_CHUNK
    o = o_bd[0:c] + o_bd[c:2 * c] + o_bd[2 * c:3 * c] + o_bd[3 * c:4 * c]
    return o, state


def _gdn_kernel(*refs, has_s0):
    if has_s0:
        (q_ref, k_ref, v_ref, z_ref, cwq_ref, cwk_ref, cwv_ref, gab_ref, alog_ref, dtb_ref, nw_ref, s0_ref,
         o_ref, of, ob, msk) = refs
        sfin_ref = None
    else:
        (q_ref, k_ref, v_ref, z_ref, cwq_ref, cwk_ref, cwv_ref, gab_ref, alog_ref, dtb_ref, nw_ref,
         o_ref, sfin_ref, of, ob, msk) = refs
    c = GDN_CHUNK
    L = q_ref.shape[1]
    n = L // c

    @pl.when(pl.program_id(0) == 0)
    def _():
        _gdn_fill_masks(msk)

    alog = alog_ref[...]
    dtb = dtb_ref[...]
    cwq, cwk, cwv = cwq_ref[...], cwk_ref[...], cwv_ref[...]

    def body(i, states):
        new = []
        for direction, out in ((0, of), (1, ob)):
            ci = i if direction == 0 else n - 1 - i
            r = pl.multiple_of(ci * c, c)
            rows = pl.ds(r, c)
            o, s = _gdn_chunk(_gdn_conv_chunk(q_ref, cwq, r, L), _gdn_conv_chunk(k_ref, cwk, r, L),
                              _gdn_conv_chunk(v_ref, cwv, r, L), gab_ref[0, rows, :], alog, dtb, direction,
                              states[direction], msk)
            out[rows, :] = o
            new.append(s)
        return tuple(new)

    if has_s0:
        init = (s0_ref[0, 0, 0], s0_ref[0, 0, 1])
    else:
        init = (jnp.zeros((GDN_N, GDN_N), f32), jnp.zeros((GDN_N, GDN_N), f32))
    finals = lax.fori_loop(0, n, body, init)
    if sfin_ref is not None:
        for direction in range(2):
            for h in range(GDN_H):
                sfin_ref[0, direction, h] = finals[direction][c * h:c * (h + 1), c * h:c * (h + 1)]

    bdb = msk[M_BD].astype(bf16)
    nw = nw_ref[...]

    def norm_gate(i, carry):
        rows = pl.ds(pl.multiple_of(i * c, c), c)
        o = of[rows, :] + ob[rows, :]
        ms = _group_sum(o * o, bdb) * (1.0 / GDN_DV)
        o_ref[0, rows, :] = o * lax.rsqrt(ms + EPS) * nw * _silu(z_ref[0, rows, :])
        return carry

    lax.fori_loop(0, n, norm_gate, 0)


def _gdn(zg, zab, cw, alog_row, dtb_row, nw, seq_off, nseq, L, s0=None, layer=0):
    H = GDN_H
    w = H * GDN_DK
    one = pl.Buffered(1)
    tok = lambda k: pl.BlockSpec((1, L, w), lambda s, k=k: (s + seq_off, 0, k), pipeline_mode=one)
    cws = lambda k: pl.BlockSpec((3, w), lambda s, k=k: (0, k))
    full = lambda shape: pl.BlockSpec(shape, lambda s: (0,) * len(shape))
    in_specs = [tok(0), tok(1), tok(2), tok(3), cws(0), cws(1), cws(2),
                pl.BlockSpec((1, L, 128), lambda s: (s + seq_off, 0, 0)),
                full((1, 128)), full((1, 128)), full((1, w))]
    args = [zg, zg, zg, zg, cw, cw, cw, zab, alog_row, dtb_row, jnp.tile(nw.reshape(1, GDN_DV), (1, H))]
    out_specs = [pl.BlockSpec((1, L, w), lambda s: (s, 0, 0))]
    out_shape = [jax.ShapeDtypeStruct((nseq, L, w), f32)]
    if s0 is not None:
        in_specs.append(pl.BlockSpec((1, 1, 2, GDN_N, GDN_N), lambda s: (s, layer, 0, 0, 0)))
        args.append(s0)
    else:
        out_specs.append(pl.BlockSpec((1, 2, H, GDN_DK, GDN_DV), lambda s: (s, 0, 0, 0, 0)))
        out_shape.append(jax.ShapeDtypeStruct((nseq, 2, H, GDN_DK, GDN_DV), f32))
    return pl.pallas_call(
        functools.partial(_gdn_kernel, has_s0=s0 is not None),
        grid=(nseq,),
        in_specs=in_specs,
        out_specs=out_specs,
        out_shape=out_shape,
        scratch_shapes=[pltpu.VMEM((L, w), f32), pltpu.VMEM((L, w), f32), pltpu.VMEM((GDN_NMASK, GDN_N, GDN_N), f32)],
        compiler_params=_cparams(("arbitrary",)),
        name="gdn",
    )(*args)


def _bitrev(p, bits):
    r = 0
    for _ in range(bits):
        r = (r << 1) | (p & 1)
        p >>= 1
    return r


@functools.lru_cache(maxsize=None)
def _fft_tables(L):
    n = 2 * L
    n2 = DFT_N2
    n1 = n // n2
    bits = n1.bit_length() - 1
    npair = max(n1 // 2, 1)
    sta = np.zeros((max(bits, 1) * npair, 2 * n2), np.float64)
    stb = np.zeros_like(sta)
    for s in range(bits):
        half = n1 >> (s + 1)
        for p in range(npair):
            j = p % half
            ang = -2.0 * np.pi * j / (2 * half)
            wr, wi = np.cos(ang), np.sin(ang)
            sta[s * npair + p, :] = wr
            stb[s * npair + p, :n2] = -wi
            stb[s * npair + p, n2:] = wi
    twa = np.zeros((n1, 2 * n2), np.float64)
    twb = np.zeros_like(twa)
    lanes = np.arange(n2)
    for p in range(n1):
        ang = -2.0 * np.pi * lanes * _bitrev(p, bits) / n
        twa[p, :n2] = np.cos(ang)
        twa[p, n2:] = np.cos(ang)
        twb[p, :n2] = -np.sin(ang)
        twb[p, n2:] = np.sin(ang)
    kn = np.outer(lanes, lanes) * (-2.0 * np.pi / n2)
    fr, fi = np.cos(kn), np.sin(kn)
    fwd = np.block([[fr, fi], [-fi, fr]])
    inv = np.block([[fr, -fi], [fi, fr]])
    as32 = lambda a: np.asarray(a, np.float32)
    return dict(n1=n1, bits=bits, npair=npair, sta=as32(sta), stb=as32(stb), twa=as32(twa), twb=as32(twb),
                fwd=as32(fwd), inv=as32(inv))


@functools.lru_cache(maxsize=None)
def _hyena_pos_table(L):
    n = 2 * L
    idx = np.arange(n)
    pos = np.where(idx < L, idx, n - idx).astype(np.float64)
    pos[L] = 0.0
    t = pos / max(L - 1, 1)
    bands = np.linspace(1e-4, HY_BANDS - 1, HY_BANDS).astype(np.float32).astype(np.float64)
    ang = (2.0 * math.pi * pos / L)[None, :] * bands[:, None]
    z = np.zeros((LANES, n), np.float64)
    z[0] = t
    z[1:1 + HY_BANDS] = np.cos(ang)
    z[1 + HY_BANDS:1 + 2 * HY_BANDS] = -np.sin(ang)
    deltas = np.abs(np.linspace(HY_SLOW_DECAY, HY_FAST_DECAY, HY_W)).reshape(HY_W, 1)
    return np.asarray(z, np.float32), np.asarray(deltas, np.float32)


def _swap_halves(x):
    n2 = x.shape[1] // 2
    return jnp.concatenate([x[:, n2:], x[:, :n2]], axis=1)


def _fft_forward(X, sta_ref, stb_ref, twa_ref, twb_ref, fh_ref, fl_ref, n1, bits, npair, ct, mrows):
    for s in range(bits):
        half = n1 >> (s + 1)

        def pair(p, carry, s=s, half=half):
            grp = p // half
            j = p - grp * half
            a = grp * 2 * half + j
            ra = pl.ds(pl.multiple_of(a * ct, ct), ct)
            rb = pl.ds(pl.multiple_of((a + half) * ct, ct), ct)
            xa = X[ra, :]
            xb = X[rb, :]
            X[ra, :] = xa + xb
            d = xa - xb
            X[rb, :] = d * sta_ref[pl.ds(s * npair + p, 1), :] + _swap_halves(d) * stb_ref[pl.ds(s * npair + p, 1), :]
            return carry

        lax.fori_loop(0, npair, pair, 0)

    def blk(p, carry):
        r = pl.ds(pl.multiple_of(p * ct, ct), ct)
        y = X[r, :]
        X[r, :] = y * twa_ref[pl.ds(p, 1), :] + _swap_halves(y) * twb_ref[pl.ds(p, 1), :]
        return carry

    lax.fori_loop(0, n1, blk, 0)

    def mm(i, carry):
        r = pl.ds(pl.multiple_of(i * mrows, mrows), mrows)
        X[r, :] = _dot3_w(X[r, :], fh_ref[...], fl_ref[...])
        return carry

    lax.fori_loop(0, n1 * ct // mrows, mm, 0)


def _fft_inverse(X, sta_ref, stb_ref, twa_ref, twb_ref, fh_ref, fl_ref, n1, bits, npair, ct, mrows):
    def mm(i, carry):
        r = pl.ds(pl.multiple_of(i * mrows, mrows), mrows)
        X[r, :] = _dot3_w(X[r, :], fh_ref[...], fl_ref[...])
        return carry

    lax.fori_loop(0, n1 * ct // mrows, mm, 0)

    def blk(p, carry):
        r = pl.ds(pl.multiple_of(p * ct, ct), ct)
        y = X[r, :]
        X[r, :] = y * twa_ref[pl.ds(p, 1), :] - _swap_halves(y) * twb_ref[pl.ds(p, 1), :]
        return carry

    lax.fori_loop(0, n1, blk, 0)

    for s in reversed(range(bits)):
        half = n1 >> (s + 1)

        def pair(p, carry, s=s, half=half):
            grp = p // half
            j = p - grp * half
            a = grp * 2 * half + j
            ra = pl.ds(pl.multiple_of(a * ct, ct), ct)
            rb = pl.ds(pl.multiple_of((a + half) * ct, ct), ct)
            xa = X[ra, :]
            xb = X[rb, :]
            tw = xb * sta_ref[pl.ds(s * npair + p, 1), :] - _swap_halves(xb) * stb_ref[pl.ds(s * npair + p, 1), :]
            X[ra, :] = xa + tw
            X[rb, :] = xa - tw
            return carry

        lax.fori_loop(0, npair, pair, 0)


def _conv3_lanes(x, w, b):
    L = x.shape[1]
    lane = lax.broadcasted_iota(jnp.int32, x.shape, 1)
    xp = jnp.where(lane == 0, 0.0, pltpu.roll(x, 1, 1))
    xn = jnp.where(lane == L - 1, 0.0, pltpu.roll(x, L - 1, 1))
    return xp * w[:, 0:1] + x * w[:, 1:2] + xn * w[:, 2:3] + b


def _hyena_kernel(x0a_ref, x1a_ref, va_ref, x0b_ref, x1b_ref, vb_ref, cw0_ref, cw1_ref, cw2_ref,
                  cb0_ref, cb1_ref, cb2_ref, d_ref, zt_ref, dl_ref, w1_ref, b1_ref, w2_ref, b2_ref,
                  w3f_ref, w3b_ref, fr_ref, sta_ref, stb_ref, twa_ref, twb_ref,
                  ffh_ref, ffl_ref, fih_ref, fil_ref, o_ref, X, HA, HB, *, L, n1, bits, npair, ct, mrows):
    n2 = DFT_N2
    n = 2 * L
    fft_args = (sta_ref, stb_ref, twa_ref, twb_ref)

    @pl.when(pl.program_id(1) == 0)
    def _():
        zt = zt_ref[...]
        fr = fr_ref[...]
        h = jnp.sin(fr * (_dot3(w1_ref[...], zt) + b1_ref[...]))
        h = jnp.sin(fr * (_dot3(w2_ref[...], h) + b2_ref[...]))
        hf = _dot3(w3f_ref[...], h)
        hb = _dot3(w3b_ref[...], h)
        lane = lax.broadcasted_iota(jnp.int32, hf.shape, 1)
        dec = jnp.exp(-zt[0:1, :] * dl_ref[...])
        hc = jnp.where(lane < L, hf, jnp.where(lane > L, hb, 0.0)) * dec
        for b in range(n1):
            X[b * ct:(b + 1) * ct, 0:n2] = hc[:, b * n2:(b + 1) * n2]
            X[b * ct:(b + 1) * ct, n2:2 * n2] = jnp.zeros((ct, n2), f32)
        _fft_forward(X, *fft_args, ffh_ref, ffl_ref, n1, bits, npair, ct, mrows)
        hs = X[...]
        HA[...] = jnp.concatenate([hs[:, :n2], hs[:, :n2]], axis=1)
        HB[...] = jnp.concatenate([-hs[:, n2:], hs[:, n2:]], axis=1)

    x0a = _conv3_lanes(x0a_ref[0], cw0_ref[...], cb0_ref[...])
    x0b = _conv3_lanes(x0b_ref[0], cw0_ref[...], cb0_ref[...])
    vva = _conv3_lanes(va_ref[0], cw2_ref[...], cb2_ref[...]) * _conv3_lanes(x1a_ref[0], cw1_ref[...], cb1_ref[...])
    vvb = _conv3_lanes(vb_ref[0], cw2_ref[...], cb2_ref[...]) * _conv3_lanes(x1b_ref[0], cw1_ref[...], cb1_ref[...])
    nb = L // n2
    for b in range(nb):
        X[b * ct:(b + 1) * ct, 0:n2] = vva[:, b * n2:(b + 1) * n2]
        X[b * ct:(b + 1) * ct, n2:2 * n2] = vvb[:, b * n2:(b + 1) * n2]
    X[nb * ct:n1 * ct, :] = jnp.zeros(((n1 - nb) * ct, 2 * n2), f32)
    _fft_forward(X, *fft_args, ffh_ref, ffl_ref, n1, bits, npair, ct, mrows)

    def spec(i, carry):
        r = pl.ds(pl.multiple_of(i * ct, ct), ct)
        x = X[r, :]
        X[r, :] = x * HA[r, :] + _swap_halves(x) * HB[r, :]
        return carry

    lax.fori_loop(0, n1, spec, 0)
    _fft_inverse(X, *fft_args, fih_ref, fil_ref, n1, bits, npair, ct, mrows)
    inv_n = 1.0 / n
    ya = jnp.concatenate([X[b * ct:(b + 1) * ct, 0:n2] for b in range(nb)], axis=1) * inv_n
    yb = jnp.concatenate([X[b * ct:(b + 1) * ct, n2:2 * n2] for b in range(nb)], axis=1) * inv_n
    dcol = d_ref[...]
    o_ref[0] = (ya + vva * dcol) * x0a
    o_ref[1] = (yb + vvb * dcol) * x0b


def _hyena(zhT, grp_a, grp_b, lane_a, lane_b, npairs, L, ct, hw):
    tabs = _fft_tables(L)
    n1, bits, npair = tabs["n1"], tabs["bits"], tabs["npair"]
    n = 2 * L
    ntile = HY_W // ct
    mrows = min(512, n1 * ct)
    zt, deltas = _hyena_pos_table(L)
    fwd = jnp.asarray(tabs["fwd"])
    inv = jnp.asarray(tabs["inv"])
    ffh = fwd.astype(bf16)
    ffl = (fwd - ffh.astype(f32)).astype(bf16)
    fih = inv.astype(bf16)
    fil = (inv - fih.astype(f32)).astype(bf16)
    xin = lambda k, grp, ln: pl.BlockSpec((1, ct, L), lambda j, p, k=k: (grp(p), j + ntile * k, ln(p)))
    chan = lambda k, w: pl.BlockSpec((ct, w), lambda j, p, k=k: (j + ntile * k, 0))
    full = lambda a: pl.BlockSpec(a.shape, lambda j, p: (0,) * a.ndim)
    consts = [jnp.asarray(zt), jnp.asarray(deltas)]
    in_specs = ([xin(0, grp_a, lane_a), xin(1, grp_a, lane_a), xin(2, grp_a, lane_a),
                 xin(0, grp_b, lane_b), xin(1, grp_b, lane_b), xin(2, grp_b, lane_b),
                 chan(0, 3), chan(1, 3), chan(2, 3), chan(0, 1), chan(1, 1), chan(2, 1), chan(0, 1),
                 full(consts[0]), chan(0, 1),
                 full(hw["w1T"]), full(hw["b1"]), full(hw["w2T"]), full(hw["b2"]),
                 chan(0, HY_FH), chan(1, HY_FH), full(hw["freq"])]
                + [full(jnp.asarray(tabs[k])) for k in ("sta", "stb", "twa", "twb")]
                + [full(ffh), full(ffl), full(fih), full(fil)])
    args = ([zhT] * 6 + [hw["cwT"]] * 3 + [hw["cb"]] * 3 + [hw["d"], consts[0], consts[1],
            hw["w1T"], hw["b1"], hw["w2T"], hw["b2"], hw["w3T"], hw["w3T"], hw["freq"]]
            + [jnp.asarray(tabs[k]) for k in ("sta", "stb", "twa", "twb")] + [ffh, ffl, fih, fil])
    return pl.pallas_call(
        functools.partial(_hyena_kernel, L=L, n1=n1, bits=bits, npair=npair, ct=ct, mrows=mrows),
        grid=(ntile, npairs),
        in_specs=in_specs,
        out_specs=pl.BlockSpec((2, ct, L), lambda j, p: (p, j, 0)),
        out_shape=jax.ShapeDtypeStruct((2 * npairs, HY_W, L), f32),
        scratch_shapes=[pltpu.VMEM((n1 * ct, 2 * DFT_N2), f32)] * 3,
        compiler_params=_cparams(("arbitrary", "arbitrary")),
        name="hyena",
    )(*args)


def _merge_kernel(x_ref, mod_ref, nw_ref, oa_ref, ob_ref, oc_ref, od_ref, wg_ref, wb_ref, wo_ref, xo_ref):
    g = pl.program_id(0)
    d = D_MODEL
    x = x_ref[0]
    mod = mod_ref[pl.ds(g, 1), :]
    h = _norm_mod(x, nw_ref[...], mod[:, d:2 * d], mod[:, 0:d]).astype(bf16)
    pa = _dot(oa_ref[0], wb_ref[0])
    pb = _dot(ob_ref[0], wb_ref[1])
    pc = _dot_tn(oc_ref[0], wb_ref[2])
    pd = _dot(od_ref[0], wb_ref[3])
    acc = jnp.zeros_like(x)
    for nbr, proj in enumerate((pa, pb, pc, pd)):
        gate = _sigmoid(jnp.dot(h, wg_ref[:, nbr * d:(nbr + 1) * d], preferred_element_type=f32))
        acc = acc + gate * proj
    xo_ref[0] = x + mod[:, 2 * d:3 * d] * _dot(acc, wo_ref[...])


def _merge(x, mod_l, nw, oa, ob, oc, od, wg, wb, wo, tm):
    G, T, d = x.shape
    full = lambda shape: pl.BlockSpec(shape, lambda g, i: (0,) * len(shape))
    tok = lambda w: pl.BlockSpec((1, tm, w), lambda g, i: (g, i, 0))
    return pl.pallas_call(
        _merge_kernel,
        grid=(G, T // tm),
        in_specs=[tok(d), full(mod_l.shape), full((1, d)), tok(256), tok(256),
                  pl.BlockSpec((1, HY_W, tm), lambda g, i: (g, 0, i)), tok(256),
                  full(wg.shape), full(wb.shape), full(wo.shape)],
        out_specs=tok(d),
        out_shape=jax.ShapeDtypeStruct((G, T, d), f32),
        compiler_params=_cparams(("arbitrary", "arbitrary")),
        name="merge",
    )(x, mod_l, nw.reshape(1, d), oa, ob, oc, od, wg, wb, wo)


def _mlp_kernel(x_ref, mod_ref, nw_ref, w1_ref, w2_ref, xo_ref):
    g = pl.program_id(0)
    d = D_MODEL
    x = x_ref[0]
    mod = mod_ref[pl.ds(g, 1), :]
    h = _norm_mod(x, nw_ref[...], mod[:, 4 * d:5 * d], mod[:, 3 * d:4 * d]).astype(bf16)
    acc = jnp.zeros_like(x)
    for c in range(D_FF // d):
        a = jnp.maximum(jnp.dot(h, w1_ref[:, c * d:(c + 1) * d], preferred_element_type=f32), 0.0)
        acc = acc + _dot(a * a, w2_ref[c * d:(c + 1) * d, :])
    xo_ref[0] = x + mod[:, 5 * d:6 * d] * acc


def _mlp(x, mod_l, nw, w1, w2, tm):
    G, T, d = x.shape
    full = lambda shape: pl.BlockSpec(shape, lambda g, i: (0,) * len(shape))
    tok = pl.BlockSpec((1, tm, d), lambda g, i: (g, i, 0))
    return pl.pallas_call(
        _mlp_kernel,
        grid=(G, T // tm),
        in_specs=[tok, full(mod_l.shape), full((1, d)), full(w1.shape), full(w2.shape)],
        out_specs=tok,
        out_shape=jax.ShapeDtypeStruct((G, T, d), f32),
        compiler_params=_cparams(("arbitrary", "arbitrary")),
        name="mlp",
    )(x, mod_l, nw.reshape(1, d), w1, w2)


def _final_kernel(x_ref, nw_ref, o_ref):
    o_ref[0] = _rms(x_ref[0], nw_ref[...])


def _final_norm(x, nw, tm):
    G, T, d = x.shape
    tok = pl.BlockSpec((1, tm, d), lambda g, i: (g, i, 0))
    return pl.pallas_call(
        _final_kernel,
        grid=(G, T // tm),
        in_specs=[tok, pl.BlockSpec((1, d), lambda g, i: (0, 0))],
        out_specs=tok,
        out_shape=jax.ShapeDtypeStruct((G, T, d), f32),
        compiler_params=_cparams(("arbitrary", "arbitrary")),
        name="final_norm",
    )(x, nw.reshape(1, d))


@functools.lru_cache(maxsize=None)
def _rope_tables(T):
    m = MLA_ROPE // 4
    inv = ROPE_BASE ** (-np.arange(m, dtype=np.float64) / m)
    rows = T // GRID_W
    row_pos = np.repeat(np.arange(rows), GRID_W)[:, None] * inv
    col_pos = np.tile(np.arange(GRID_W), rows)[:, None] * inv
    cos32 = np.concatenate([np.cos(row_pos), np.cos(row_pos), np.cos(col_pos), np.cos(col_pos)], axis=1)
    sin32 = np.concatenate([-np.sin(row_pos), np.sin(row_pos), -np.sin(col_pos), np.sin(col_pos)], axis=1)
    ca = np.zeros((2, T, 128))
    cb = np.zeros((2, T, 128))
    ca[:, :, 0:96] = 1.0
    ca[1, :, 64:96] = cos32
    cb[1, :, 64:96] = sin32
    cd = np.ones((2, T, 256))
    sd = np.zeros((2, T, 256))
    cd[1] = np.tile(cos32, (1, 8))
    sd[1] = np.tile(sin32, (1, 8))
    return tuple(np.asarray(a, np.float32) for a in (ca, cb, cd, sd))


def _swap_perm(width):
    base = np.concatenate([np.arange(8, 16), np.arange(0, 8), np.arange(24, 32), np.arange(16, 24)])
    return np.concatenate([base + 32 * s for s in range(width // 32)])


def _pack_weights(w_in, mla_w_uq, mla_w_ukv, gdn_conv_w, gdn_a_log, gdn_dt_bias, hy_conv_w, hy_conv_b,
                  hy_f_w1, hy_f_b1, hy_f_w2, hy_f_b2, hy_f_w3, hy_f_freq, hy_d):
    depth = w_in.shape[0]
    offs = [0] + [int(s) for s in np.cumsum(IN_SPLITS)]
    seg = lambda i: w_in[:, :, offs[i]:offs[i + 1]]
    zeros = lambda n: jnp.zeros((depth, D_MODEL, n), w_in.dtype)
    kpe = seg(2)
    kpe_sw = kpe[:, :, _swap_perm(MLA_ROPE)]
    wa = jnp.concatenate([seg(0), seg(1), zeros(64), kpe, zeros(32), zeros(64), kpe_sw, zeros(32)], axis=2)
    dq, dk, dv = seg(10), seg(11), seg(12)
    perm = _swap_perm(256)
    dv_slots = jnp.concatenate([dv.reshape(depth, D_MODEL, DA_H, DA_DV),
                                jnp.zeros((depth, D_MODEL, DA_H, LANES - DA_DV), dv.dtype)],
                               axis=3).reshape(depth, D_MODEL, DA_H * LANES)
    wd = jnp.concatenate([dq, dq[:, :, perm], dk, dk[:, :, perm], dv_slots], axis=2)
    wg = jnp.concatenate([seg(3), seg(4), seg(5), seg(6)], axis=2)
    wab = jnp.concatenate([seg(7), seg(8), zeros(128 - 4 * GDN_H)], axis=2)
    whT = jnp.swapaxes(seg(9), 1, 2)
    wgate = seg(13)
    uq = mla_w_uq.reshape(depth, MLA_Q_LORA, MLA_H, MLA_NOPE + MLA_ROPE)
    z32 = jnp.zeros((depth, MLA_Q_LORA, MLA_H, 32), uq.dtype)
    z64 = jnp.zeros((depth, MLA_Q_LORA, MLA_H, 64), uq.dtype)
    rope_sw = uq[..., MLA_NOPE:][..., _swap_perm(MLA_ROPE)]
    wqa = jnp.concatenate([uq, z32], axis=3).reshape(depth, MLA_Q_LORA, MLA_H * 128)
    wqb = jnp.concatenate([z64, rope_sw, z32], axis=3).reshape(depth, MLA_Q_LORA, MLA_H * 128)
    ukv = mla_w_ukv.reshape(depth, MLA_KV_LORA, MLA_H, MLA_NOPE + MLA_V)
    wka = jnp.concatenate([ukv[..., :MLA_NOPE], jnp.zeros((depth, MLA_KV_LORA, MLA_H, 64), ukv.dtype)],
                          axis=3).reshape(depth, MLA_KV_LORA, MLA_H * 128)
    wv = jnp.concatenate([ukv[..., MLA_NOPE:], jnp.zeros((depth, MLA_KV_LORA, MLA_H, LANES - MLA_V), ukv.dtype)],
                         axis=3).reshape(depth, MLA_KV_LORA, MLA_H * LANES)
    cast = lambda a: a.astype(bf16)
    pad128 = lambda a: jnp.pad(a.reshape(depth, 1, -1), ((0, 0), (0, 0), (0, 128 - a.shape[1] * a.shape[2])))
    hy = dict(
        cwT=jnp.swapaxes(hy_conv_w, 1, 2),
        cb=hy_conv_b.reshape(depth, -1, 1),
        d=hy_d.reshape(depth, HY_W, 1),
        w1T=jnp.pad(jnp.swapaxes(hy_f_w1, 1, 2), ((0, 0), (0, 0), (0, LANES - HY_EMB))),
        b1=hy_f_b1.reshape(depth, HY_FH, 1),
        w2T=jnp.swapaxes(hy_f_w2, 1, 2),
        b2=hy_f_b2.reshape(depth, HY_FH, 1),
        w3T=jnp.swapaxes(hy_f_w3, 1, 2),
        freq=hy_f_freq.reshape(depth, HY_FH, 1),
    )
    return dict(wa=cast(wa), wd=cast(wd), wg=cast(wg), wab=cast(wab), whT=cast(whT), wgate=cast(wgate),
                wqa=cast(wqa), wqb=cast(wqb), wka=cast(wka), wv=cast(wv),
                alog=pad128(gdn_a_log), dtb=pad128(gdn_dt_bias), hy=hy)


def kernel(x_prompt, x_sample, cache_mla_ckv, cache_mla_kpe, cache_diff_k, cache_diff_v, state_gdn, c, c_ctx, w_ada, b_ada, norm_mix_w, norm_mlp_w, w_in, mla_q_norm_w, mla_w_uq, mla_kv_norm_w, mla_w_ukv, gdn_conv_w, gdn_a_log, gdn_dt_bias, gdn_norm_w, hy_conv_w, hy_conv_b, hy_f_w1, hy_f_b1, hy_f_w2, hy_f_b2, hy_f_w3, hy_f_freq, hy_d, da_lq1, da_lk1, da_lq2, da_lk2, da_norm_w, w_branch, w_out, mlp_w1, mlp_w2, final_norm_w):
    depth = w_in.shape[0]
    bc, lc, d = x_prompt.shape
    bl, ll, _ = x_sample.shape
    T = ll
    assert bc * lc == T and d == D_MODEL and bc % 2 == 0 and bl == 2
    G = 1 + bl
    past = cache_mla_ckv.shape[2]
    tm = min(512, T)
    tq = min(256, lc)
    tq_lat, kc_lat = min(256, ll), min(512, ll)

    pk = _pack_weights(w_in, mla_w_uq, mla_w_ukv, gdn_conv_w, gdn_a_log, gdn_dt_bias, hy_conv_w, hy_conv_b,
                       hy_f_w1, hy_f_b1, hy_f_w2, hy_f_b2, hy_f_w3, hy_f_freq, hy_d)
    wb_bf = w_branch.astype(bf16)
    wo_bf = w_out.astype(bf16)
    w1_bf = mlp_w1.astype(bf16)
    w2_bf = mlp_w2.astype(bf16)
    ca, cb, cd, sd = (jnp.asarray(t) for t in _rope_tables(T))

    cond8 = jnp.concatenate([c_ctx.reshape(1, d), c, jnp.zeros((8 - G, d), f32)], axis=0)
    mod = _modulation(cond8, w_ada, b_ada)

    kpe_pad = jnp.pad(cache_mla_kpe, ((0, 0), (0, 0), (0, 0), (MLA_NOPE, 128 - MLA_NOPE - MLA_ROPE)))
    kc_mla, vc_mla = _cache_kv(cache_mla_ckv, kpe_pad, pk["wka"], pk["wv"])
    kc_da = jnp.transpose(cache_diff_k.reshape(bl, depth, past, DA_H, 2 * DA_DK), (1, 3, 0, 2, 4)).astype(bf16)
    vc_da = jnp.transpose(cache_diff_v, (1, 3, 0, 2, 4))
    vc_da = jnp.concatenate([vc_da, jnp.ones(vc_da.shape[:-1] + (1,), f32),
                             jnp.zeros(vc_da.shape[:-1] + (LANES - DA_DV - 1,), f32)], axis=-1).astype(bf16)
    s0_bd = jnp.einsum('bldhij,hg->bldhigj', state_gdn, jnp.eye(GDN_H, dtype=f32)).reshape(
        bl, depth, 2, GDN_N, GDN_N)

    x = jnp.concatenate([x_prompt.reshape(1, T, d), x_sample], axis=0)
    new_ckv, new_kpe, new_dk, new_dv, new_state = [], [], [], [], []
    for l in range(depth):
        lam_init = 0.8 - 0.6 * math.exp(-0.3 * l)
        za, zd, zg, zab, zhT = _in_proj(x, mod[l], norm_mix_w[l], pk["wa"][l], pk["wd"][l], pk["wg"][l],
                                        pk["wab"][l], pk["whT"][l], tm)
        q, k, v, ckv, dq1, dq2, dk, dv = _prep(za, zd, ca, cb, cd, sd, mla_q_norm_w[l], mla_kv_norm_w[l],
                                               pk["wqa"][l], pk["wqb"][l], pk["wka"][l], pk["wv"][l], tm)
        new_ckv.append(ckv[0].reshape(bc, lc, MLA_KV_LORA))
        new_kpe.append(za[0, :, 384 + MLA_NOPE:384 + MLA_NOPE + MLA_ROPE].reshape(bc, lc, MLA_ROPE))
        new_dk.append(zd[0, :, 512:768].reshape(bc, lc, DA_H, 2, DA_DK))
        new_dv.append(zd[0, :, 1024:].reshape(bc, lc, DA_H, LANES)[..., :DA_DV])

        ctx_view = lambda a: a.reshape(a.shape[0], G * bc, lc, a.shape[3])
        oa_c = _mla_attention(ctx_view(q), ctx_view(k), ctx_view(v), 0, bc, lc, tq, lc)
        oa_l = _mla_attention(q, k, v, 1, bl, ll, tq_lat, kc_lat, cache=(kc_mla[l], vc_mla[l]))
        oa = jnp.concatenate([oa_c.reshape(1, T, -1), oa_l], axis=0)

        lamv = jnp.concatenate([da_lq1[l][None], da_lk1[l][None], da_lq2[l][None], da_lk2[l][None],
                                jnp.full((1, DA_DK), lam_init, f32), jnp.zeros((3, DA_DK), f32)], axis=0)
        od_c = _diff_attention(ctx_view(dq1), ctx_view(dq2), ctx_view(dk), ctx_view(dv), lamv, da_norm_w[l],
                               0, bc, lc, tq, lc)
        od_l = _diff_attention(dq1, dq2, dk, dv, lamv, da_norm_w[l], 1, bl, ll, tq_lat, kc_lat,
                               cache=(kc_da[l], vc_da[l]))
        od = jnp.concatenate([od_c.reshape(1, T, -1), od_l], axis=0)

        zg_ctx = zg.reshape(G * bc, lc, zg.shape[2])
        zab_ctx = zab.reshape(G * bc, lc, 128)
        ob_c, s_gdn = _gdn(zg_ctx, zab_ctx, gdn_conv_w[l], pk["alog"][l], pk["dtb"][l], gdn_norm_w[l], 0, bc, lc)
        ob_l = _gdn(zg, zab, gdn_conv_w[l], pk["alog"][l], pk["dtb"][l], gdn_norm_w[l], 1, bl, ll,
                    s0=s0_bd, layer=l)[0]
        ob = jnp.concatenate([ob_c.reshape(1, T, -1), ob_l], axis=0)
        new_state.append(s_gdn)

        hw = {name: val[l] for name, val in pk["hy"].items()}
        oc_c = _hyena(zhT, lambda p: 0, lambda p: 0, lambda p: 2 * p, lambda p: 2 * p + 1, bc // 2, lc, 128, hw)
        oc_l = _hyena(zhT, lambda p: 1, lambda p: 2, lambda p: 0, lambda p: 0, 1, ll, 64, hw)
        oc_c = jnp.transpose(oc_c, (1, 0, 2)).reshape(1, HY_W, T)
        oc = jnp.concatenate([oc_c, oc_l], axis=0)

        x = _merge(x, mod[l], norm_mix_w[l], oa, ob, oc, od, pk["wgate"][l], wb_bf[l], wo_bf[l], tm)
        x = _mlp(x, mod[l], norm_mlp_w[l], w1_bf[l], w2_bf[l], tm)

    y = _final_norm(x, final_norm_w, tm)
    y_prompt = y[0].reshape(bc, lc, d)
    y_sample = y[1:]
    return (y_prompt, y_sample, jnp.stack(new_ckv, axis=1), jnp.stack(new_kpe, axis=1), jnp.stack(new_dk, axis=1),
            jnp.stack(new_dv, axis=1), jnp.stack(new_state, axis=1))
```

```python
import functools
import math

import numpy as np
import jax
import jax.numpy as jnp
from jax import lax
from jax.experimental import pallas as pl
from jax.experimental.pallas import tpu as pltpu

f32 = jnp.float32
bf16 = jnp.bfloat16

D_MODEL = 1024
GRID_W = 64
N_BRANCH = 4
MLA_H = 4
MLA_NOPE = 64
MLA_ROPE = 32
MLA_V = 64
MLA_Q_LORA = 256
MLA_KV_LORA = 128
GDN_H = 4
GDN_DK = 64
GDN_DV = 64
GDN_CHUNK = 64
HY_W = 256
HY_BANDS = 16
HY_EMB = 1 + 2 * HY_BANDS
HY_FH = 64
HY_SLOW_DECAY = math.log(1e-2) / 1.5
HY_FAST_DECAY = math.log(1e-2) / 0.3
DA_H = 4
DA_DK = 32
DA_DV = 64
D_FF = 4 * D_MODEL
ROPE_BASE = 10000.0
EPS = 1e-6
IN_SPLITS = (MLA_Q_LORA, MLA_KV_LORA, MLA_ROPE,
             GDN_H * GDN_DK, GDN_H * GDN_DK, GDN_H * GDN_DV, GDN_H * GDN_DV, 2 * GDN_H, 2 * GDN_H,
             3 * HY_W,
             DA_H * 2 * DA_DK, DA_H * 2 * DA_DK, DA_H * DA_DV,
             N_BRANCH * D_MODEL)

LOG2E = math.log2(math.e)
LANES = 128
V_ONE = 64
DFT_N2 = 256
VMEM_LIMIT = 56 * 1024 * 1024


def _cparams(sem):
    return pltpu.CompilerParams(dimension_semantics=sem, vmem_limit_bytes=VMEM_LIMIT)


def _dot(a, b):
    return jnp.dot(a.astype(bf16), b.astype(bf16), preferred_element_type=f32)


def _dot_nt(a, b):
    return lax.dot_general(a.astype(bf16), b.astype(bf16), (((1,), (1,)), ((), ())), preferred_element_type=f32)


def _dot_tn(a, b):
    return lax.dot_general(a.astype(bf16), b.astype(bf16), (((0,), (0,)), ((), ())), preferred_element_type=f32)


def _split(x):
    hi = x.astype(bf16)
    lo = (x - hi.astype(f32)).astype(bf16)
    return hi, lo


def _dot3(a, b):
    ah, al = _split(a)
    bh, bl = _split(b)
    return (jnp.dot(ah, bh, preferred_element_type=f32) + jnp.dot(ah, bl, preferred_element_type=f32)
            + jnp.dot(al, bh, preferred_element_type=f32))


def _dot3_w(a, bh, bl):
    ah, al = _split(a)
    return (jnp.dot(ah, bh, preferred_element_type=f32) + jnp.dot(ah, bl, preferred_element_type=f32)
            + jnp.dot(al, bh, preferred_element_type=f32))


def _sigmoid(x):
    return 1.0 / (1.0 + jnp.exp(-x))


def _silu(x):
    return x * _sigmoid(x)


def _softplus(x):
    return jnp.maximum(x, 0.0) + jnp.log(1.0 + jnp.exp(-jnp.abs(x)))


def _rms(x, w):
    return x * lax.rsqrt(jnp.mean(x * x, axis=-1, keepdims=True) + EPS) * w


def _mod_kernel(c_ref, w_ref, b_ref, o_ref):
    c = _silu(c_ref[...])
    o_ref[0] = _dot3(c, w_ref[0]) + b_ref[0]


def _modulation(cond8, w_ada, b_ada):
    depth, d, n6 = w_ada.shape
    tn = 1536
    return pl.pallas_call(
        _mod_kernel,
        grid=(depth, n6 // tn),
        in_specs=[pl.BlockSpec((8, d), lambda l, j: (0, 0)),
                  pl.BlockSpec((1, d, tn), lambda l, j: (l, 0, j)),
                  pl.BlockSpec((1, 1, tn), lambda l, j: (l, 0, j))],
        out_specs=pl.BlockSpec((1, 8, tn), lambda l, j: (l, 0, j)),
        out_shape=jax.ShapeDtypeStruct((depth, 8, n6), f32),
        compiler_params=_cparams(("arbitrary", "arbitrary")),
        name="modulation",
    )(cond8, w_ada, b_ada.reshape(depth, 1, n6))


def _norm_mod(x, nw, scale, shift):
    return _rms(x, nw) * (1.0 + scale) + shift


def _in_kernel(x_ref, mod_ref, nw_ref, wa_ref, wd_ref, wg_ref, wab_ref, whT_ref,
               za_ref, zd_ref, zg_ref, zab_ref, zhT_ref):
    g = pl.program_id(0)
    d = D_MODEL
    mod = mod_ref[pl.ds(g, 1), :]
    h = _norm_mod(x_ref[0], nw_ref[...], mod[:, d:2 * d], mod[:, 0:d]).astype(bf16)
    za_ref[0] = jnp.dot(h, wa_ref[...], preferred_element_type=f32)
    zd_ref[0] = jnp.dot(h, wd_ref[...], preferred_element_type=f32)
    zg_ref[0] = jnp.dot(h, wg_ref[...], preferred_element_type=f32)
    zab_ref[0] = jnp.dot(h, wab_ref[...], preferred_element_type=f32)
    zhT_ref[0] = lax.dot_general(whT_ref[...], h, (((1,), (1,)), ((), ())), preferred_element_type=f32)


def _in_proj(x, mod_l, nw, wa, wd, wg, wab, whT, tm):
    G, T, d = x.shape
    na, nd, ng, nab, nh = wa.shape[1], wd.shape[1], wg.shape[1], wab.shape[1], whT.shape[0]
    full = lambda shape: pl.BlockSpec(shape, lambda g, i: (0,) * len(shape))
    return pl.pallas_call(
        _in_kernel,
        grid=(G, T // tm),
        in_specs=[pl.BlockSpec((1, tm, d), lambda g, i: (g, i, 0)),
                  full(mod_l.shape), full((1, d)), full(wa.shape), full(wd.shape), full(wg.shape),
                  full(wab.shape), full(whT.shape)],
        out_specs=[pl.BlockSpec((1, tm, na), lambda g, i: (g, i, 0)),
                   pl.BlockSpec((1, tm, nd), lambda g, i: (g, i, 0)),
                   pl.BlockSpec((1, tm, ng), lambda g, i: (g, i, 0)),
                   pl.BlockSpec((1, tm, nab), lambda g, i: (g, i, 0)),
                   pl.BlockSpec((1, nh, tm), lambda g, i: (g, 0, i))],
        out_shape=[jax.ShapeDtypeStruct((G, T, na), f32), jax.ShapeDtypeStruct((G, T, nd), f32),
                   jax.ShapeDtypeStruct((G, T, ng), f32), jax.ShapeDtypeStruct((G, T, nab), f32),
                   jax.ShapeDtypeStruct((G, nh, T), f32)],
        compiler_params=_cparams(("arbitrary", "arbitrary")),
        name="in_proj",
    )(x, mod_l, nw.reshape(1, d), wa, wd, wg, wab, whT)


def _ones_col(rows):
    return jnp.where(lax.broadcasted_iota(jnp.int32, (rows, LANES), 1) == V_ONE, 1.0, 0.0)


def _prep_kernel(za_ref, zd_ref, ca_ref, cb_ref, cd_ref, sd_ref, qnw_ref, kvnw_ref, wqa_ref, wqb_ref, wka_ref, wv_ref,
                 q_ref, k_ref, v_ref, ckv_ref, dq1_ref, dq2_ref, dk_ref, dv_ref):
    za = za_ref[0]
    cqn = _rms(za[:, 0:256], qnw_ref[...]).astype(bf16)
    ckv = _rms(za[:, 256:384], kvnw_ref[...])
    ckv_ref[0] = ckv
    ckvb = ckv.astype(bf16)
    ca = ca_ref[0]
    cb = cb_ref[0]
    qa = jnp.dot(cqn, wqa_ref[...], preferred_element_type=f32)
    qb = jnp.dot(cqn, wqb_ref[...], preferred_element_type=f32)
    kn = jnp.dot(ckvb, wka_ref[...], preferred_element_type=f32)
    vv = jnp.dot(ckvb, wv_ref[...], preferred_element_type=f32)
    krope = za[:, 384:512] * ca + za[:, 512:640] * cb
    qs = (MLA_NOPE + MLA_ROPE) ** -0.5 * LOG2E
    ones_col = _ones_col(za.shape[0])
    for h in range(MLA_H):
        sl = slice(128 * h, 128 * (h + 1))
        q_ref[h, 0] = ((qa[:, sl] * ca + qb[:, sl] * cb) * qs).astype(bf16)
        k_ref[h, 0] = (kn[:, sl] + krope).astype(bf16)
        v_ref[h, 0] = (vv[:, sl] + ones_col).astype(bf16)
    zd = zd_ref[0]
    cd = cd_ref[0]
    sd = sd_ref[0]
    dqs = DA_DK ** -0.5 * LOG2E
    dq = (zd[:, 0:256] * cd + zd[:, 256:512] * sd) * dqs
    dk = zd[:, 512:768] * cd + zd[:, 768:1024] * sd
    first = (lax.broadcasted_iota(jnp.int32, dq.shape, 1) & (2 * DA_DK - 1)) < DA_DK
    dq1 = jnp.where(first, dq, 0.0)
    dq2 = jnp.where(first, 0.0, dq)
    for h in range(DA_H):
        sl = slice(64 * h, 64 * (h + 1))
        dq1_ref[h, 0] = dq1[:, sl].astype(bf16)
        dq2_ref[h, 0] = dq2[:, sl].astype(bf16)
        dk_ref[h, 0] = dk[:, sl].astype(bf16)
        dv_ref[h, 0] = (zd[:, 1024 + 128 * h:1024 + 128 * (h + 1)] + ones_col).astype(bf16)


def _prep(za, zd, ca, cb, cd, sd, qnw, kvnw, wqa, wqb, wka, wv, tm):
    G, T, na = za.shape
    nd = zd.shape[2]
    full = lambda shape: pl.BlockSpec(shape, lambda g, i: (0,) * len(shape))
    tab = lambda w: pl.BlockSpec((1, tm, w), lambda g, i: (jnp.minimum(g, 1), i, 0))
    hm = lambda w: pl.BlockSpec((4, 1, tm, w), lambda g, i: (0, g, i, 0))
    hms = lambda w: jax.ShapeDtypeStruct((4, G, T, w), bf16)
    return pl.pallas_call(
        _prep_kernel,
        grid=(G, T // tm),
        in_specs=[pl.BlockSpec((1, tm, na), lambda g, i: (g, i, 0)),
                  pl.BlockSpec((1, tm, nd), lambda g, i: (g, i, 0)),
                  tab(128), tab(128), tab(256), tab(256),
                  full((1, 256)), full((1, 128)), full(wqa.shape), full(wqb.shape), full(wka.shape), full(wv.shape)],
        out_specs=[hm(128), hm(128), hm(128), pl.BlockSpec((1, tm, 128), lambda g, i: (g, i, 0)),
                   hm(64), hm(64), hm(64), hm(128)],
        out_shape=[hms(128), hms(128), hms(128), jax.ShapeDtypeStruct((G, T, 128), f32),
                   hms(64), hms(64), hms(64), hms(128)],
        compiler_params=_cparams(("arbitrary", "arbitrary")),
        name="attn_prep",
    )(za, zd, ca, cb, cd, sd, qnw.reshape(1, -1), kvnw.reshape(1, -1), wqa, wqb, wka, wv)


def _cache_kv_kernel(ckv_ref, kpe_ref, wka_ref, wv_ref, k_ref, v_ref):
    ckvb = ckv_ref[0, 0].astype(bf16)
    kn = jnp.dot(ckvb, wka_ref[0], preferred_element_type=f32)
    vv = jnp.dot(ckvb, wv_ref[0], preferred_element_type=f32)
    kpe = kpe_ref[0, 0]
    ones_col = _ones_col(kpe.shape[0])
    for h in range(MLA_H):
        sl = slice(128 * h, 128 * (h + 1))
        k_ref[0, h, 0] = (kn[:, sl] + kpe).astype(bf16)
        v_ref[0, h, 0] = (vv[:, sl] + ones_col).astype(bf16)


def _cache_kv(cache_ckv, cache_kpe_pad, wka, wv):
    bl, depth, p, _ = cache_ckv.shape
    return pl.pallas_call(
        _cache_kv_kernel,
        grid=(depth, bl),
        in_specs=[pl.BlockSpec((1, 1, p, 128), lambda l, b: (b, l, 0, 0)),
                  pl.BlockSpec((1, 1, p, 128), lambda l, b: (b, l, 0, 0)),
                  pl.BlockSpec((1,) + wka.shape[1:], lambda l, b: (l, 0, 0)),
                  pl.BlockSpec((1,) + wv.shape[1:], lambda l, b: (l, 0, 0))],
        out_specs=[pl.BlockSpec((1, 4, 1, p, 128), lambda l, b: (l, 0, b, 0, 0)),
                   pl.BlockSpec((1, 4, 1, p, 128), lambda l, b: (l, 0, b, 0, 0))],
        out_shape=[jax.ShapeDtypeStruct((depth, 4, bl, p, 128), bf16),
                   jax.ShapeDtypeStruct((depth, 4, bl, p, 128), bf16)],
        compiler_params=_cparams(("arbitrary", "arbitrary")),
        name="cache_kv",
    )(cache_ckv, cache_kpe_pad, wka, wv)


ATT_RB = 64


def _attn_scratch(nchains, tq, kc):
    return [pltpu.VMEM((nchains, tq, 1), f32), pltpu.VMEM((nchains, tq, 1), f32),
            pltpu.VMEM((nchains, tq, LANES), f32), pltpu.VMEM((nchains, tq, kc), f32),
            pltpu.VMEM((nchains, tq, kc), bf16)]


def _attn_scan(q_refs, k_ref, v_ref, cache_refs, scratch, kc):
    m_sc, al_sc, acc_sc, s_sc, p_sc = scratch
    nh = k_ref.shape[0]
    lk = k_ref.shape[2]
    m_sc[...] = jnp.full(m_sc.shape, -jnp.inf, f32)
    acc_sc[...] = jnp.zeros(acc_sc.shape, f32)

    tq = m_sc.shape[1]
    chains = [(j * nh + h, q_ref, h) for j, q_ref in enumerate(q_refs) for h in range(nh)]

    def step(get_k, get_v, kw):
        for c, q_ref, h in chains:
            s_sc[c, :, 0:kw] = lax.dot_general(q_ref[h, 0], get_k(h), (((1,), (1,)), ((), ())),
                                               preferred_element_type=f32)
        for c, _, _ in chains:
            for rb in range(tq // ATT_RB):
                rows = slice(rb * ATT_RB, (rb + 1) * ATT_RB)
                s = s_sc[c, rows, 0:kw]
                m_old = m_sc[c, rows, :]
                m_new = jnp.maximum(m_old, jnp.max(s, axis=1, keepdims=True))
                p_sc[c, rows, 0:kw] = jnp.exp2(s - m_new).astype(bf16)
                al_sc[c, rows, :] = jnp.exp2(m_old - m_new)
                m_sc[c, rows, :] = m_new
        for c, _, h in chains:
            acc_sc[c] = al_sc[c] * acc_sc[c] + jnp.dot(p_sc[c, :, 0:kw], get_v(h), preferred_element_type=f32)

    def body(i, carry):
        rows = pl.ds(pl.multiple_of(i * kc, kc), kc)
        step(lambda h: k_ref[h, 0, rows, :], lambda h: v_ref[h, 0, rows, :], kc)
        return carry

    lax.fori_loop(0, lk // kc, body, 0)
    if cache_refs is not None:
        kc_ref, vc_ref = cache_refs
        step(lambda h: kc_ref[h, 0], lambda h: vc_ref[h, 0], kc_ref.shape[2])


def _attn_out(acc):
    return acc[:, 0:V_ONE] / acc[:, V_ONE:V_ONE + 1]


def _mla_attn_kernel(*refs, kc, has_cache):
    if has_cache:
        q_ref, k_ref, v_ref, kc_ref, vc_ref, o_ref, *scratch = refs
        cache_refs = (kc_ref, vc_ref)
    else:
        q_ref, k_ref, v_ref, o_ref, *scratch = refs
        cache_refs = None
    _attn_scan((q_ref,), k_ref, v_ref, cache_refs, scratch, kc)
    acc_sc = scratch[2]
    for h in range(MLA_H):
        o_ref[0, :, MLA_V * h:MLA_V * (h + 1)] = _attn_out(acc_sc[h])


def _mla_attention(q, k, v, seq_off, nseq, L, tq, kc, cache=None):
    H = q.shape[0]
    kv = lambda rows, off: pl.BlockSpec((H, 1, rows, 128), lambda b, i: (0, b + off, 0, 0))
    in_specs = [pl.BlockSpec((H, 1, tq, 128), lambda b, i: (0, b + seq_off, i, 0)), kv(L, seq_off), kv(L, seq_off)]
    args = [q, k, v]
    if cache is not None:
        in_specs += [kv(cache[0].shape[2], 0), kv(cache[0].shape[2], 0)]
        args += list(cache)
    return pl.pallas_call(
        functools.partial(_mla_attn_kernel, kc=kc, has_cache=cache is not None),
        grid=(nseq, L // tq),
        in_specs=in_specs,
        out_specs=pl.BlockSpec((1, tq, H * MLA_V), lambda b, i: (b, i, 0)),
        out_shape=jax.ShapeDtypeStruct((nseq, L, H * MLA_V), f32),
        scratch_shapes=_attn_scratch(H, tq, kc),
        compiler_params=_cparams(("arbitrary", "arbitrary")),
        name="mla_attn",
    )(*args)


def _diff_attn_kernel(*refs, kc, has_cache):
    if has_cache:
        q1_ref, q2_ref, k_ref, v_ref, kc_ref, vc_ref, lam_ref, nw_ref, o_ref, *scratch = refs
        cache_refs = (kc_ref, vc_ref)
    else:
        q1_ref, q2_ref, k_ref, v_ref, lam_ref, nw_ref, o_ref, *scratch = refs
        cache_refs = None
    acc_sc = scratch[2]
    lamv = lam_ref[...]
    lam_init = lamv[4:5, 0:1]
    lam = (jnp.exp(jnp.sum(lamv[0:1] * lamv[1:2], axis=1, keepdims=True))
           - jnp.exp(jnp.sum(lamv[2:3] * lamv[3:4], axis=1, keepdims=True)) + lam_init)
    _attn_scan((q1_ref, q2_ref), k_ref, v_ref, cache_refs, scratch, kc)
    for h in range(DA_H):
        o = _attn_out(acc_sc[h]) - lam * _attn_out(acc_sc[DA_H + h])
        o_ref[0, :, DA_DV * h:DA_DV * (h + 1)] = _rms(o, nw_ref[...]) * (1.0 - lam_init)


def _diff_attention(q1, q2, k, v, lamv, nw, seq_off, nseq, L, tq, kc, cache=None):
    H = q1.shape[0]
    hm = lambda rows, w, off: pl.BlockSpec((H, 1, rows, w), lambda b, i: (0, b + off, 0, 0))
    qs = pl.BlockSpec((H, 1, tq, 64), lambda b, i: (0, b + seq_off, i, 0))
    in_specs = [qs, qs, hm(L, 64, seq_off), hm(L, 128, seq_off)]
    args = [q1, q2, k, v]
    if cache is not None:
        p = cache[0].shape[2]
        in_specs += [hm(p, 64, 0), hm(p, 128, 0)]
        args += list(cache)
    in_specs += [pl.BlockSpec((8, DA_DK), lambda b, i: (0, 0)), pl.BlockSpec((1, DA_DV), lambda b, i: (0, 0))]
    args += [lamv, nw.reshape(1, DA_DV)]
    return pl.pallas_call(
        functools.partial(_diff_attn_kernel, kc=kc, has_cache=cache is not None),
        grid=(nseq, L // tq),
        in_specs=in_specs,
        out_specs=pl.BlockSpec((1, tq, H * DA_DV), lambda b, i: (b, i, 0)),
        out_shape=jax.ShapeDtypeStruct((nseq, L, H * DA_DV), f32),
        scratch_shapes=_attn_scratch(2 * H, tq, kc),
        compiler_params=_cparams(("arbitrary", "arbitrary")),
        name="diff_attn",
    )(*args)


GDN_N = GDN_H * GDN_CHUNK
GDN_INV_BASE = 8
(M_BD, M_EYE, M_BASE, M_OFF8, M_OFF16, M_OFF32, M_DIR) = range(7)
GDN_NMASK = M_DIR + 6


def _gdn_fill_masks(msk):
    n, c = GDN_N, GDN_CHUNK
    rr = lax.broadcasted_iota(jnp.int32, (n, n), 0)
    cc = lax.broadcasted_iota(jnp.int32, (n, n), 1)
    blk = lambda x, s: lax.shift_right_logical(x, int(math.log2(s)))
    bd = blk(rr, c) == blk(cc, c)
    ri = rr & (c - 1)
    cj = cc & (c - 1)

    def put(i, cond):
        msk[i] = jnp.where(cond, 1.0, 0.0)

    put(M_BD, bd)
    put(M_EYE, rr == cc)
    put(M_BASE, blk(rr, GDN_INV_BASE) == blk(cc, GDN_INV_BASE))
    for slot, s in ((M_OFF8, 8), (M_OFF16, 16), (M_OFF32, 32)):
        put(slot, (blk(rr, 2 * s) == blk(cc, 2 * s)) & (blk(rr, s) != blk(cc, s)))
    put(M_DIR + 0, bd & (ri >= cj))
    put(M_DIR + 1, bd & (ri > cj))
    put(M_DIR + 2, bd & (ri <= cj))
    put(M_DIR + 3, bd & (ri <= cj))
    put(M_DIR + 4, bd & (ri < cj))
    put(M_DIR + 5, bd & (ri >= cj))


def _gdn_conv_chunk(x_ref, w, r, L):
    c = GDN_CHUNK
    x = x_ref[0, pl.ds(r, c), :]
    prev8 = x_ref[0, pl.ds(pl.multiple_of(jnp.maximum(r - 8, 0), 8), 8), :]
    next8 = x_ref[0, pl.ds(pl.multiple_of(jnp.minimum(r + c, L - 8), 8), 8), :]
    prev = jnp.where(r > 0, prev8[7:8, :], 0.0)
    nxt = jnp.where(r + c < L, next8[0:1, :], 0.0)
    row = lax.broadcasted_iota(jnp.int32, x.shape, 0)
    xp = jnp.where(row == 0, prev, pltpu.roll(x, 1, 0))
    xn = jnp.where(row == c - 1, nxt, pltpu.roll(x, c - 1, 0))
    y = xp * w[0:1] + x * w[1:2] + xn * w[2:3]
    return _silu(y)


def _group_sum(y, bdb):
    hi, lo = _split(y)
    return jnp.dot(hi, bdb, preferred_element_type=f32) + jnp.dot(lo, bdb, preferred_element_type=f32)


(B_K, B_Q, B_KB, B_RU, B_E, B_TRI, B_T, B_P, B_X, B_U, B_W, B_IN, B_VN) = range(13)
GDN_NBUF = 13


def _gdn_chunk_pair(xs, gabs, alog, dtb, S, pools, msk):
    n, c = GDN_N, GDN_CHUNK
    bdb = msk[M_BD].astype(bf16)
    small = ({}, {})

    def stage(fn):
        for d in (0, 1):
            fn(d, pools[d], small[d])

    def prep(d, B, sm):
        xq, xk, xv = xs[d]
        bd = msk[M_BD]
        q = xq * lax.rsqrt(_group_sum(xq * xq, bdb) + EPS) * (GDN_DK ** -0.5)
        k = xk * lax.rsqrt(_group_sum(xk * xk, bdb) + EPS)
        gab = gabs[d]
        lane = lax.broadcasted_iota(jnp.int32, gab.shape, 1)
        gfull = -jnp.exp(alog) * _softplus(gab + dtb)
        bfull = _sigmoid(gab)

        def stack(arr, base):
            return jnp.concatenate([jnp.sum(jnp.where(lane == base + h, arr, 0.0), axis=1, keepdims=True)
                                    for h in range(GDN_H)], axis=0)

        g_stack = stack(gfull, d * GDN_H)
        b_stack = stack(bfull, (2 + d) * GDN_H)
        g_b = jnp.broadcast_to(g_stack, (n, n))
        g_row = jnp.sum(g_b * msk[M_EYE], axis=0, keepdims=True)
        gc_row = jnp.sum(g_b * msk[M_DIR + 3 * d + 2], axis=0, keepdims=True)
        g_rb = jnp.broadcast_to(g_row, (n, n))
        gc_col = jnp.sum(g_rb * msk[M_DIR + 3 * d], axis=1, keepdims=True)
        g_last = jnp.sum(g_rb * bd, axis=1, keepdims=True)
        B[B_E] = jnp.exp(jnp.minimum(gc_col - gc_row, 0.0))
        tile4 = lambda x: jnp.concatenate([x] * GDN_H, axis=0)
        k_bd = tile4(k) * bd
        B[B_K] = k_bd
        B[B_KB] = k_bd * b_stack
        B[B_Q] = tile4(q) * bd
        B[B_RU] = tile4(xv) * (bd * b_stack)
        sm.update(gc_col=gc_col, egc=jnp.exp(gc_col), g_last=g_last)

    def tri(d, B, sm):
        t = _dot_nt(B[B_KB], B[B_K]) * (B[B_E] * msk[M_DIR + 3 * d + 1])
        B[B_TRI] = t
        nm = -(t * msk[M_BASE])
        B[B_P] = nm
        B[B_T] = msk[M_EYE] + nm

    def intra(d, B, sm):
        B[B_IN] = _dot_nt(B[B_Q], B[B_K]) * (B[B_E] * msk[M_DIR + 3 * d])

    def square(d, B, sm):
        B[B_P] = _dot(B[B_P], B[B_P])

    def extend(d, B, sm):
        B[B_T] = B[B_T] + _dot(B[B_T], B[B_P])

    stage(prep)
    stage(tri)
    stage(intra)
    for _ in range(2):
        stage(square)
        stage(extend)
    for slot in (M_OFF8, M_OFF16, M_OFF32):
        def cross(d, B, sm, slot=slot):
            B[B_X] = _dot(B[B_T], B[B_TRI] * msk[slot])

        def merge(d, B, sm):
            B[B_T] = B[B_T] - _dot(B[B_X], B[B_T])

        stage(cross)
        stage(merge)

    def solve_u(d, B, sm):
        B[B_U] = _dot(B[B_T], B[B_RU])

    def solve_w(d, B, sm):
        B[B_W] = _dot(B[B_T], B[B_KB] * sm["egc"])

    def v_new(d, B, sm):
        B[B_VN] = B[B_U] - _dot(B[B_W], S[d])

    def out(d, B, sm):
        o_bd = _dot(B[B_Q] * sm["egc"], S[d]) + _dot(B[B_IN], B[B_VN])
        sm["o"] = o_bd[0:c] + o_bd[c:2 * c] + o_bd[2 * c:3 * c] + o_bd[3 * c:4 * c]

    def update(d, B, sm):
        S[d] = (S[d] * jnp.exp(sm["g_last"])
                + _dot_tn(B[B_K] * jnp.exp(sm["g_last"] - sm["gc_col"]), B[B_VN]))

    for fn in (solve_u, solve_w, v_new, out, update):
        stage(fn)
    return small[0]["o"], small[1]["o"]


def _gdn_kernel(*refs, has_s0):
    if has_s0:
        (q_ref, k_ref, v_ref, z_ref, cwq_ref, cwk_ref, cwv_ref, gab_ref, alog_ref, dtb_ref, nw_ref, s0_ref,
         o_ref, of, ob, msk, S, pool_f, pool_b) = refs
        sfin_ref = None
    else:
        (q_ref, k_ref, v_ref, z_ref, cwq_ref, cwk_ref, cwv_ref, gab_ref, alog_ref, dtb_ref, nw_ref,
         o_ref, sfin_ref, of, ob, msk, S, pool_f, pool_b) = refs
    c = GDN_CHUNK
    L = q_ref.shape[1]
    n = L // c

    @pl.when(pl.program_id(0) == 0)
    def _():
        _gdn_fill_masks(msk)

    alog = alog_ref[...]
    dtb = dtb_ref[...]
    cwq, cwk, cwv = cwq_ref[...], cwk_ref[...], cwv_ref[...]
    if has_s0:
        S[...] = s0_ref[0, 0]
    else:
        S[...] = jnp.zeros(S.shape, f32)

    def body(i, carry):
        starts = (pl.multiple_of(i * c, c), pl.multiple_of((n - 1 - i) * c, c))
        xs = [(_gdn_conv_chunk(q_ref, cwq, r, L), _gdn_conv_chunk(k_ref, cwk, r, L),
               _gdn_conv_chunk(v_ref, cwv, r, L)) for r in starts]
        gabs = [gab_ref[0, pl.ds(r, c), :] for r in starts]
        o_f, o_b = _gdn_chunk_pair(xs, gabs, alog, dtb, S, (pool_f, pool_b), msk)
        of[pl.ds(starts[0], c), :] = o_f
        ob[pl.ds(starts[1], c), :] = o_b
        return carry

    lax.fori_loop(0, n, body, 0)
    if sfin_ref is not None:
        for direction in range(2):
            for h in range(GDN_H):
                sfin_ref[0, direction, h] = S[direction, c * h:c * (h + 1), c * h:c * (h + 1)]

    bdb = msk[M_BD].astype(bf16)
    nw = nw_ref[...]

    def norm_gate(i, carry):
        rows = pl.ds(pl.multiple_of(i * c, c), c)
        o = of[rows, :] + ob[rows, :]
        ms = _group_sum(o * o, bdb) * (1.0 / GDN_DV)
        o_ref[0, rows, :] = o * lax.rsqrt(ms + EPS) * nw * _silu(z_ref[0, rows, :])
        return carry

    lax.fori_loop(0, n, norm_gate, 0)


def _gdn(zg, zab, cw, alog_row, dtb_row, nw, seq_off, nseq, L, s0=None, layer=0):
    H = GDN_H
    w = H * GDN_DK
    one = pl.Buffered(1)
    tok = lambda k: pl.BlockSpec((1, L, w), lambda s, k=k: (s + seq_off, 0, k), pipeline_mode=one)
    cws = lambda k: pl.BlockSpec((3, w), lambda s, k=k: (0, k))
    full = lambda shape: pl.BlockSpec(shape, lambda s: (0,) * len(shape))
    in_specs = [tok(0), tok(1), tok(2), tok(3), cws(0), cws(1), cws(2),
                pl.BlockSpec((1, L, 128), lambda s: (s + seq_off, 0, 0)),
                full((1, 128)), full((1, 128)), full((1, w))]
    args = [zg, zg, zg, zg, cw, cw, cw, zab, alog_row, dtb_row, jnp.tile(nw.reshape(1, GDN_DV), (1, H))]
    out_specs = [pl.BlockSpec((1, L, w), lambda s: (s, 0, 0))]
    out_shape = [jax.ShapeDtypeStruct((nseq, L, w), f32)]
    if s0 is not None:
        in_specs.append(pl.BlockSpec((1, 1, 2, GDN_N, GDN_N), lambda s: (s, layer, 0, 0, 0)))
        args.append(s0)
    else:
        out_specs.append(pl.BlockSpec((1, 2, H, GDN_DK, GDN_DV), lambda s: (s, 0, 0, 0, 0)))
        out_shape.append(jax.ShapeDtypeStruct((nseq, 2, H, GDN_DK, GDN_DV), f32))
    return pl.pallas_call(
        functools.partial(_gdn_kernel, has_s0=s0 is not None),
        grid=(nseq,),
        in_specs=in_specs,
        out_specs=out_specs,
        out_shape=out_shape,
        scratch_shapes=[pltpu.VMEM((L, w), f32), pltpu.VMEM((L, w), f32), pltpu.VMEM((GDN_NMASK, GDN_N, GDN_N), f32),
                        pltpu.VMEM((2, GDN_N, GDN_N), f32), pltpu.VMEM((GDN_NBUF, GDN_N, GDN_N), f32),
                        pltpu.VMEM((GDN_NBUF, GDN_N, GDN_N), f32)],
        compiler_params=_cparams(("arbitrary",)),
        name="gdn",
    )(*args)


def _bitrev(p, bits):
    r = 0
    for _ in range(bits):
        r = (r << 1) | (p & 1)
        p >>= 1
    return r


@functools.lru_cache(maxsize=None)
def _fft_tables(L):
    n = 2 * L
    n2 = DFT_N2
    n1 = n // n2
    bits = n1.bit_length() - 1
    npair = max(n1 // 2, 1)
    sta = np.zeros((max(bits, 1) * npair, 2 * n2), np.float64)
    stb = np.zeros_like(sta)
    for s in range(bits):
        half = n1 >> (s + 1)
        for p in range(npair):
            j = p % half
            ang = -2.0 * np.pi * j / (2 * half)
            wr, wi = np.cos(ang), np.sin(ang)
            sta[s * npair + p, :] = wr
            stb[s * npair + p, :n2] = -wi
            stb[s * npair + p, n2:] = wi
    twa = np.zeros((n1, 2 * n2), np.float64)
    twb = np.zeros_like(twa)
    lanes = np.arange(n2)
    for p in range(n1):
        ang = -2.0 * np.pi * lanes * _bitrev(p, bits) / n
        twa[p, :n2] = np.cos(ang)
        twa[p, n2:] = np.cos(ang)
        twb[p, :n2] = -np.sin(ang)
        twb[p, n2:] = np.sin(ang)
    kn = np.outer(lanes, lanes) * (-2.0 * np.pi / n2)
    fr, fi = np.cos(kn), np.sin(kn)
    fwd = np.block([[fr, fi], [-fi, fr]])
    inv = np.block([[fr, -fi], [fi, fr]])
    as32 = lambda a: np.asarray(a, np.float32)
    return dict(n1=n1, bits=bits, npair=npair, sta=as32(sta), stb=as32(stb), twa=as32(twa), twb=as32(twb),
                fwd=as32(fwd), inv=as32(inv))


@functools.lru_cache(maxsize=None)
def _hyena_pos_table(L):
    n = 2 * L
    idx = np.arange(n)
    pos = np.where(idx < L, idx, n - idx).astype(np.float64)
    pos[L] = 0.0
    t = pos / max(L - 1, 1)
    bands = np.linspace(1e-4, HY_BANDS - 1, HY_BANDS).astype(np.float32).astype(np.float64)
    ang = (2.0 * math.pi * pos / L)[None, :] * bands[:, None]
    z = np.zeros((LANES, n), np.float64)
    z[0] = t
    z[1:1 + HY_BANDS] = np.cos(ang)
    z[1 + HY_BANDS:1 + 2 * HY_BANDS] = -np.sin(ang)
    deltas = np.abs(np.linspace(HY_SLOW_DECAY, HY_FAST_DECAY, HY_W)).reshape(HY_W, 1)
    return np.asarray(z, np.float32), np.asarray(deltas, np.float32)


def _swap_halves(x):
    n2 = x.shape[1] // 2
    return jnp.concatenate([x[:, n2:], x[:, :n2]], axis=1)


def _fft_forward(X, sta_ref, stb_ref, twa_ref, twb_ref, fh_ref, fl_ref, n1, bits, npair, ct, mrows):
    for s in range(bits):
        half = n1 >> (s + 1)

        def pair(p, carry, s=s, half=half):
            grp = p // half
            j = p - grp * half
            a = grp * 2 * half + j
            ra = pl.ds(pl.multiple_of(a * ct, ct), ct)
            rb = pl.ds(pl.multiple_of((a + half) * ct, ct), ct)
            xa = X[ra, :]
            xb = X[rb, :]
            X[ra, :] = xa + xb
            d = xa - xb
            X[rb, :] = d * sta_ref[pl.ds(s * npair + p, 1), :] + _swap_halves(d) * stb_ref[pl.ds(s * npair + p, 1), :]
            return carry

        lax.fori_loop(0, npair, pair, 0)

    def blk(p, carry):
        r = pl.ds(pl.multiple_of(p * ct, ct), ct)
        y = X[r, :]
        X[r, :] = y * twa_ref[pl.ds(p, 1), :] + _swap_halves(y) * twb_ref[pl.ds(p, 1), :]
        return carry

    lax.fori_loop(0, n1, blk, 0)

    def mm(i, carry):
        r = pl.ds(pl.multiple_of(i * mrows, mrows), mrows)
        X[r, :] = _dot3_w(X[r, :], fh_ref[...], fl_ref[...])
        return carry

    lax.fori_loop(0, n1 * ct // mrows, mm, 0)


def _fft_inverse(X, sta_ref, stb_ref, twa_ref, twb_ref, fh_ref, fl_ref, n1, bits, npair, ct, mrows):
    def mm(i, carry):
        r = pl.ds(pl.multiple_of(i * mrows, mrows), mrows)
        X[r, :] = _dot3_w(X[r, :], fh_ref[...], fl_ref[...])
        return carry

    lax.fori_loop(0, n1 * ct // mrows, mm, 0)

    def blk(p, carry):
        r = pl.ds(pl.multiple_of(p * ct, ct), ct)
        y = X[r, :]
        X[r, :] = y * twa_ref[pl.ds(p, 1), :] - _swap_halves(y) * twb_ref[pl.ds(p, 1), :]
        return carry

    lax.fori_loop(0, n1, blk, 0)

    for s in reversed(range(bits)):
        half = n1 >> (s + 1)

        def pair(p, carry, s=s, half=half):
            grp = p // half
            j = p - grp * half
            a = grp * 2 * half + j
            ra = pl.ds(pl.multiple_of(a * ct, ct), ct)
            rb = pl.ds(pl.multiple_of((a + half) * ct, ct), ct)
            xa = X[ra, :]
            xb = X[rb, :]
            tw = xb * sta_ref[pl.ds(s * npair + p, 1), :] - _swap_halves(xb) * stb_ref[pl.ds(s * npair + p, 1), :]
            X[ra, :] = xa + tw
            X[rb, :] = xa - tw
            return carry

        lax.fori_loop(0, npair, pair, 0)


def _conv3_lanes(x, w, b):
    L = x.shape[1]
    lane = lax.broadcasted_iota(jnp.int32, x.shape, 1)
    xp = jnp.where(lane == 0, 0.0, pltpu.roll(x, 1, 1))
    xn = jnp.where(lane == L - 1, 0.0, pltpu.roll(x, L - 1, 1))
    return xp * w[:, 0:1] + x * w[:, 1:2] + xn * w[:, 2:3] + b


def _hyena_kernel(x0a_ref, x1a_ref, va_ref, x0b_ref, x1b_ref, vb_ref, cw0_ref, cw1_ref, cw2_ref,
                  cb0_ref, cb1_ref, cb2_ref, d_ref, zt_ref, dl_ref, w1_ref, b1_ref, w2_ref, b2_ref,
                  w3f_ref, w3b_ref, fr_ref, sta_ref, stb_ref, twa_ref, twb_ref,
                  ffh_ref, ffl_ref, fih_ref, fil_ref, o_ref, X, HA, HB, *, L, n1, bits, npair, ct, mrows):
    n2 = DFT_N2
    n = 2 * L
    fft_args = (sta_ref, stb_ref, twa_ref, twb_ref)

    @pl.when(pl.program_id(1) == 0)
    def _():
        zt = zt_ref[...]
        fr = fr_ref[...]
        h = jnp.sin(fr * (_dot3(w1_ref[...], zt) + b1_ref[...]))
        h = jnp.sin(fr * (_dot3(w2_ref[...], h) + b2_ref[...]))
        hf = _dot3(w3f_ref[...], h)
        hb = _dot3(w3b_ref[...], h)
        lane = lax.broadcasted_iota(jnp.int32, hf.shape, 1)
        dec = jnp.exp(-zt[0:1, :] * dl_ref[...])
        hc = jnp.where(lane < L, hf, jnp.where(lane > L, hb, 0.0)) * dec
        for b in range(n1):
            X[b * ct:(b + 1) * ct, 0:n2] = hc[:, b * n2:(b + 1) * n2]
            X[b * ct:(b + 1) * ct, n2:2 * n2] = jnp.zeros((ct, n2), f32)
        _fft_forward(X, *fft_args, ffh_ref, ffl_ref, n1, bits, npair, ct, mrows)
        hs = X[...]
        HA[...] = jnp.concatenate([hs[:, :n2], hs[:, :n2]], axis=1)
        HB[...] = jnp.concatenate([-hs[:, n2:], hs[:, n2:]], axis=1)

    x0a = _conv3_lanes(x0a_ref[0], cw0_ref[...], cb0_ref[...])
    x0b = _conv3_lanes(x0b_ref[0], cw0_ref[...], cb0_ref[...])
    vva = _conv3_lanes(va_ref[0], cw2_ref[...], cb2_ref[...]) * _conv3_lanes(x1a_ref[0], cw1_ref[...], cb1_ref[...])
    vvb = _conv3_lanes(vb_ref[0], cw2_ref[...], cb2_ref[...]) * _conv3_lanes(x1b_ref[0], cw1_ref[...], cb1_ref[...])
    nb = L // n2
    for b in range(nb):
        X[b * ct:(b + 1) * ct, 0:n2] = vva[:, b * n2:(b + 1) * n2]
        X[b * ct:(b + 1) * ct, n2:2 * n2] = vvb[:, b * n2:(b + 1) * n2]
    X[nb * ct:n1 * ct, :] = jnp.zeros(((n1 - nb) * ct, 2 * n2), f32)
    _fft_forward(X, *fft_args, ffh_ref, ffl_ref, n1, bits, npair, ct, mrows)

    def spec(i, carry):
        r = pl.ds(pl.multiple_of(i * ct, ct), ct)
        x = X[r, :]
        X[r, :] = x * HA[r, :] + _swap_halves(x) * HB[r, :]
        return carry

    lax.fori_loop(0, n1, spec, 0)
    _fft_inverse(X, *fft_args, fih_ref, fil_ref, n1, bits, npair, ct, mrows)
    inv_n = 1.0 / n
    ya = jnp.concatenate([X[b * ct:(b + 1) * ct, 0:n2] for b in range(nb)], axis=1) * inv_n
    yb = jnp.concatenate([X[b * ct:(b + 1) * ct, n2:2 * n2] for b in range(nb)], axis=1) * inv_n
    dcol = d_ref[...]
    o_ref[0] = (ya + vva * dcol) * x0a
    o_ref[1] = (yb + vvb * dcol) * x0b


def _hyena(zhT, grp_a, grp_b, lane_a, lane_b, npairs, L, ct, hw):
    tabs = _fft_tables(L)
    n1, bits, npair = tabs["n1"], tabs["bits"], tabs["npair"]
    n = 2 * L
    ntile = HY_W // ct
    mrows = min(512, n1 * ct)
    zt, deltas = _hyena_pos_table(L)
    fwd = jnp.asarray(tabs["fwd"])
    inv = jnp.asarray(tabs["inv"])
    ffh = fwd.astype(bf16)
    ffl = (fwd - ffh.astype(f32)).astype(bf16)
    fih = inv.astype(bf16)
    fil = (inv - fih.astype(f32)).astype(bf16)
    xin = lambda k, grp, ln: pl.BlockSpec((1, ct, L), lambda j, p, k=k: (grp(p), j + ntile * k, ln(p)))
    chan = lambda k, w: pl.BlockSpec((ct, w), lambda j, p, k=k: (j + ntile * k, 0))
    full = lambda a: pl.BlockSpec(a.shape, lambda j, p: (0,) * a.ndim)
    consts = [jnp.asarray(zt), jnp.asarray(deltas)]
    in_specs = ([xin(0, grp_a, lane_a), xin(1, grp_a, lane_a), xin(2, grp_a, lane_a),
                 xin(0, grp_b, lane_b), xin(1, grp_b, lane_b), xin(2, grp_b, lane_b),
                 chan(0, 3), chan(1, 3), chan(2, 3), chan(0, 1), chan(1, 1), chan(2, 1), chan(0, 1),
                 full(consts[0]), chan(0, 1),
                 full(hw["w1T"]), full(hw["b1"]), full(hw["w2T"]), full(hw["b2"]),
                 chan(0, HY_FH), chan(1, HY_FH), full(hw["freq"])]
                + [full(jnp.asarray(tabs[k])) for k in ("sta", "stb", "twa", "twb")]
                + [full(ffh), full(ffl), full(fih), full(fil)])
    args = ([zhT] * 6 + [hw["cwT"]] * 3 + [hw["cb"]] * 3 + [hw["d"], consts[0], consts[1],
            hw["w1T"], hw["b1"], hw["w2T"], hw["b2"], hw["w3T"], hw["w3T"], hw["freq"]]
            + [jnp.asarray(tabs[k]) for k in ("sta", "stb", "twa", "twb")] + [ffh, ffl, fih, fil])
    return pl.pallas_call(
        functools.partial(_hyena_kernel, L=L, n1=n1, bits=bits, npair=npair, ct=ct, mrows=mrows),
        grid=(ntile, npairs),
        in_specs=in_specs,
        out_specs=pl.BlockSpec((2, ct, L), lambda j, p: (p, j, 0)),
        out_shape=jax.ShapeDtypeStruct((2 * npairs, HY_W, L), f32),
        scratch_shapes=[pltpu.VMEM((n1 * ct, 2 * DFT_N2), f32)] * 3,
        compiler_params=_cparams(("arbitrary", "arbitrary")),
        name="hyena",
    )(*args)


def _merge_kernel(x_ref, mod_ref, nw_ref, oa_ref, ob_ref, oc_ref, od_ref, wg_ref, wb_ref, wo_ref, xo_ref):
    g = pl.program_id(0)
    d = D_MODEL
    x = x_ref[0]
    mod = mod_ref[pl.ds(g, 1), :]
    h = _norm_mod(x, nw_ref[...], mod[:, d:2 * d], mod[:, 0:d]).astype(bf16)
    pa = _dot(oa_ref[0], wb_ref[0])
    pb = _dot(ob_ref[0], wb_ref[1])
    pc = _dot_tn(oc_ref[0], wb_ref[2])
    pd = _dot(od_ref[0], wb_ref[3])
    acc = jnp.zeros_like(x)
    for nbr, proj in enumerate((pa, pb, pc, pd)):
        gate = _sigmoid(jnp.dot(h, wg_ref[:, nbr * d:(nbr + 1) * d], preferred_element_type=f32))
        acc = acc + gate * proj
    xo_ref[0] = x + mod[:, 2 * d:3 * d] * _dot(acc, wo_ref[...])


def _merge(x, mod_l, nw, oa, ob, oc, od, wg, wb, wo, tm):
    G, T, d = x.shape
    full = lambda shape: pl.BlockSpec(shape, lambda g, i: (0,) * len(shape))
    tok = lambda w: pl.BlockSpec((1, tm, w), lambda g, i: (g, i, 0))
    return pl.pallas_call(
        _merge_kernel,
        grid=(G, T // tm),
        in_specs=[tok(d), full(mod_l.shape), full((1, d)), tok(256), tok(256),
                  pl.BlockSpec((1, HY_W, tm), lambda g, i: (g, 0, i)), tok(256),
                  full(wg.shape), full(wb.shape), full(wo.shape)],
        out_specs=tok(d),
        out_shape=jax.ShapeDtypeStruct((G, T, d), f32),
        compiler_params=_cparams(("arbitrary", "arbitrary")),
        name="merge",
    )(x, mod_l, nw.reshape(1, d), oa, ob, oc, od, wg, wb, wo)


def _mlp_kernel(x_ref, mod_ref, nw_ref, w1_ref, w2_ref, xo_ref):
    g = pl.program_id(0)
    d = D_MODEL
    x = x_ref[0]
    mod = mod_ref[pl.ds(g, 1), :]
    h = _norm_mod(x, nw_ref[...], mod[:, 4 * d:5 * d], mod[:, 3 * d:4 * d]).astype(bf16)
    acc = jnp.zeros_like(x)
    for c in range(D_FF // d):
        a = jnp.maximum(jnp.dot(h, w1_ref[:, c * d:(c + 1) * d], preferred_element_type=f32), 0.0)
        acc = acc + _dot(a * a, w2_ref[c * d:(c + 1) * d, :])
    xo_ref[0] = x + mod[:, 5 * d:6 * d] * acc


def _mlp(x, mod_l, nw, w1, w2, tm):
    G, T, d = x.shape
    full = lambda shape: pl.BlockSpec(shape, lambda g, i: (0,) * len(shape))
    tok = pl.BlockSpec((1, tm, d), lambda g, i: (g, i, 0))
    return pl.pallas_call(
        _mlp_kernel,
        grid=(G, T // tm),
        in_specs=[tok, full(mod_l.shape), full((1, d)), full(w1.shape), full(w2.shape)],
        out_specs=tok,
        out_shape=jax.ShapeDtypeStruct((G, T, d), f32),
        compiler_params=_cparams(("arbitrary", "arbitrary")),
        name="mlp",
    )(x, mod_l, nw.reshape(1, d), w1, w2)


def _final_kernel(x_ref, nw_ref, o_ref):
    o_ref[0] = _rms(x_ref[0], nw_ref[...])


def _final_norm(x, nw, tm):
    G, T, d = x.shape
    tok = pl.BlockSpec((1, tm, d), lambda g, i: (g, i, 0))
    return pl.pallas_call(
        _final_kernel,
        grid=(G, T // tm),
        in_specs=[tok, pl.BlockSpec((1, d), lambda g, i: (0, 0))],
        out_specs=tok,
        out_shape=jax.ShapeDtypeStruct((G, T, d), f32),
        compiler_params=_cparams(("arbitrary", "arbitrary")),
        name="final_norm",
    )(x, nw.reshape(1, d))


@functools.lru_cache(maxsize=None)
def _rope_tables(T):
    m = MLA_ROPE // 4
    inv = ROPE_BASE ** (-np.arange(m, dtype=np.float64) / m)
    rows = T // GRID_W
    row_pos = np.repeat(np.arange(rows), GRID_W)[:, None] * inv
    col_pos = np.tile(np.arange(GRID_W), rows)[:, None] * inv
    cos32 = np.concatenate([np.cos(row_pos), np.cos(row_pos), np.cos(col_pos), np.cos(col_pos)], axis=1)
    sin32 = np.concatenate([-np.sin(row_pos), np.sin(row_pos), -np.sin(col_pos), np.sin(col_pos)], axis=1)
    ca = np.zeros((2, T, 128))
    cb = np.zeros((2, T, 128))
    ca[:, :, 0:96] = 1.0
    ca[1, :, 64:96] = cos32
    cb[1, :, 64:96] = sin32
    cd = np.ones((2, T, 256))
    sd = np.zeros((2, T, 256))
    cd[1] = np.tile(cos32, (1, 8))
    sd[1] = np.tile(sin32, (1, 8))
    return tuple(np.asarray(a, np.float32) for a in (ca, cb, cd, sd))


def _swap_perm(width):
    base = np.concatenate([np.arange(8, 16), np.arange(0, 8), np.arange(24, 32), np.arange(16, 24)])
    return np.concatenate([base + 32 * s for s in range(width // 32)])


def _pack_weights(w_in, mla_w_uq, mla_w_ukv, gdn_conv_w, gdn_a_log, gdn_dt_bias, hy_conv_w, hy_conv_b,
                  hy_f_w1, hy_f_b1, hy_f_w2, hy_f_b2, hy_f_w3, hy_f_freq, hy_d):
    depth = w_in.shape[0]
    offs = [0] + [int(s) for s in np.cumsum(IN_SPLITS)]
    seg = lambda i: w_in[:, :, offs[i]:offs[i + 1]]
    zeros = lambda n: jnp.zeros((depth, D_MODEL, n), w_in.dtype)
    kpe = seg(2)
    kpe_sw = kpe[:, :, _swap_perm(MLA_ROPE)]
    wa = jnp.concatenate([seg(0), seg(1), zeros(64), kpe, zeros(32), zeros(64), kpe_sw, zeros(32)], axis=2)
    dq, dk, dv = seg(10), seg(11), seg(12)
    perm = _swap_perm(256)
    dv_slots = jnp.concatenate([dv.reshape(depth, D_MODEL, DA_H, DA_DV),
                                jnp.zeros((depth, D_MODEL, DA_H, LANES - DA_DV), dv.dtype)],
                               axis=3).reshape(depth, D_MODEL, DA_H * LANES)
    wd = jnp.concatenate([dq, dq[:, :, perm], dk, dk[:, :, perm], dv_slots], axis=2)
    wg = jnp.concatenate([seg(3), seg(4), seg(5), seg(6)], axis=2)
    wab = jnp.concatenate([seg(7), seg(8), zeros(128 - 4 * GDN_H)], axis=2)
    whT = jnp.swapaxes(seg(9), 1, 2)
    wgate = seg(13)
    uq = mla_w_uq.reshape(depth, MLA_Q_LORA, MLA_H, MLA_NOPE + MLA_ROPE)
    z32 = jnp.zeros((depth, MLA_Q_LORA, MLA_H, 32), uq.dtype)
    z64 = jnp.zeros((depth, MLA_Q_LORA, MLA_H, 64), uq.dtype)
    rope_sw = uq[..., MLA_NOPE:][..., _swap_perm(MLA_ROPE)]
    wqa = jnp.concatenate([uq, z32], axis=3).reshape(depth, MLA_Q_LORA, MLA_H * 128)
    wqb = jnp.concatenate([z64, rope_sw, z32], axis=3).reshape(depth, MLA_Q_LORA, MLA_H * 128)
    ukv = mla_w_ukv.reshape(depth, MLA_KV_LORA, MLA_H, MLA_NOPE + MLA_V)
    wka = jnp.concatenate([ukv[..., :MLA_NOPE], jnp.zeros((depth, MLA_KV_LORA, MLA_H, 64), ukv.dtype)],
                          axis=3).reshape(depth, MLA_KV_LORA, MLA_H * 128)
    wv = jnp.concatenate([ukv[..., MLA_NOPE:], jnp.zeros((depth, MLA_KV_LORA, MLA_H, LANES - MLA_V), ukv.dtype)],
                         axis=3).reshape(depth, MLA_KV_LORA, MLA_H * LANES)
    cast = lambda a: a.astype(bf16)
    pad128 = lambda a: jnp.pad(a.reshape(depth, 1, -1), ((0, 0), (0, 0), (0, 128 - a.shape[1] * a.shape[2])))
    hy = dict(
        cwT=jnp.swapaxes(hy_conv_w, 1, 2),
        cb=hy_conv_b.reshape(depth, -1, 1),
        d=hy_d.reshape(depth, HY_W, 1),
        w1T=jnp.pad(jnp.swapaxes(hy_f_w1, 1, 2), ((0, 0), (0, 0), (0, LANES - HY_EMB))),
        b1=hy_f_b1.reshape(depth, HY_FH, 1),
        w2T=jnp.swapaxes(hy_f_w2, 1, 2),
        b2=hy_f_b2.reshape(depth, HY_FH, 1),
        w3T=jnp.swapaxes(hy_f_w3, 1, 2),
        freq=hy_f_freq.reshape(depth, HY_FH, 1),
    )
    return dict(wa=cast(wa), wd=cast(wd), wg=cast(wg), wab=cast(wab), whT=cast(whT), wgate=cast(wgate),
                wqa=cast(wqa), wqb=cast(wqb), wka=cast(wka), wv=cast(wv),
                alog=pad128(gdn_a_log), dtb=pad128(gdn_dt_bias), hy=hy)


def kernel(x_prompt, x_sample, cache_mla_ckv, cache_mla_kpe, cache_diff_k, cache_diff_v, state_gdn, c, c_ctx, w_ada, b_ada, norm_mix_w, norm_mlp_w, w_in, mla_q_norm_w, mla_w_uq, mla_kv_norm_w, mla_w_ukv, gdn_conv_w, gdn_a_log, gdn_dt_bias, gdn_norm_w, hy_conv_w, hy_conv_b, hy_f_w1, hy_f_b1, hy_f_w2, hy_f_b2, hy_f_w3, hy_f_freq, hy_d, da_lq1, da_lk1, da_lq2, da_lk2, da_norm_w, w_branch, w_out, mlp_w1, mlp_w2, final_norm_w):
    depth = w_in.shape[0]
    bc, lc, d = x_prompt.shape
    bl, ll, _ = x_sample.shape
    T = ll
    assert bc * lc == T and d == D_MODEL and bc % 2 == 0 and bl == 2
    G = 1 + bl
    past = cache_mla_ckv.shape[2]
    tm = min(512, T)
    tq = min(256, lc)
    tq_lat, kc_lat = min(512, ll), min(512, ll)

    pk = _pack_weights(w_in, mla_w_uq, mla_w_ukv, gdn_conv_w, gdn_a_log, gdn_dt_bias, hy_conv_w, hy_conv_b,
                       hy_f_w1, hy_f_b1, hy_f_w2, hy_f_b2, hy_f_w3, hy_f_freq, hy_d)
    wb_bf = w_branch.astype(bf16)
    wo_bf = w_out.astype(bf16)
    w1_bf = mlp_w1.astype(bf16)
    w2_bf = mlp_w2.astype(bf16)
    ca, cb, cd, sd = (jnp.asarray(t) for t in _rope_tables(T))

    cond8 = jnp.concatenate([c_ctx.reshape(1, d), c, jnp.zeros((8 - G, d), f32)], axis=0)
    mod = _modulation(cond8, w_ada, b_ada)

    kpe_pad = jnp.pad(cache_mla_kpe, ((0, 0), (0, 0), (0, 0), (MLA_NOPE, 128 - MLA_NOPE - MLA_ROPE)))
    kc_mla, vc_mla = _cache_kv(cache_mla_ckv, kpe_pad, pk["wka"], pk["wv"])
    kc_da = jnp.transpose(cache_diff_k.reshape(bl, depth, past, DA_H, 2 * DA_DK), (1, 3, 0, 2, 4)).astype(bf16)
    vc_da = jnp.transpose(cache_diff_v, (1, 3, 0, 2, 4))
    vc_da = jnp.concatenate([vc_da, jnp.ones(vc_da.shape[:-1] + (1,), f32),
                             jnp.zeros(vc_da.shape[:-1] + (LANES - DA_DV - 1,), f32)], axis=-1).astype(bf16)
    s0_bd = jnp.einsum('bldhij,hg->bldhigj', state_gdn, jnp.eye(GDN_H, dtype=f32)).reshape(
        bl, depth, 2, GDN_N, GDN_N)

    x = jnp.concatenate([x_prompt.reshape(1, T, d), x_sample], axis=0)
    new_ckv, new_kpe, new_dk, new_dv, new_state = [], [], [], [], []
    for l in range(depth):
        lam_init = 0.8 - 0.6 * math.exp(-0.3 * l)
        za, zd, zg, zab, zhT = _in_proj(x, mod[l], norm_mix_w[l], pk["wa"][l], pk["wd"][l], pk["wg"][l],
                                        pk["wab"][l], pk["whT"][l], tm)
        q, k, v, ckv, dq1, dq2, dk, dv = _prep(za, zd, ca, cb, cd, sd, mla_q_norm_w[l], mla_kv_norm_w[l],
                                               pk["wqa"][l], pk["wqb"][l], pk["wka"][l], pk["wv"][l], tm)
        new_ckv.append(ckv[0].reshape(bc, lc, MLA_KV_LORA))
        new_kpe.append(za[0, :, 384 + MLA_NOPE:384 + MLA_NOPE + MLA_ROPE].reshape(bc, lc, MLA_ROPE))
        new_dk.append(zd[0, :, 512:768].reshape(bc, lc, DA_H, 2, DA_DK))
        new_dv.append(zd[0, :, 1024:].reshape(bc, lc, DA_H, LANES)[..., :DA_DV])

        ctx_view = lambda a: a.reshape(a.shape[0], G * bc, lc, a.shape[3])
        oa_c = _mla_attention(ctx_view(q), ctx_view(k), ctx_view(v), 0, bc, lc, tq, lc)
        oa_l = _mla_attention(q, k, v, 1, bl, ll, tq_lat, kc_lat, cache=(kc_mla[l], vc_mla[l]))
        oa = jnp.concatenate([oa_c.reshape(1, T, -1), oa_l], axis=0)

        lamv = jnp.concatenate([da_lq1[l][None], da_lk1[l][None], da_lq2[l][None], da_lk2[l][None],
                                jnp.full((1, DA_DK), lam_init, f32), jnp.zeros((3, DA_DK), f32)], axis=0)
        od_c = _diff_attention(ctx_view(dq1), ctx_view(dq2), ctx_view(dk), ctx_view(dv), lamv, da_norm_w[l],
                               0, bc, lc, tq, lc)
        od_l = _diff_attention(dq1, dq2, dk, dv, lamv, da_norm_w[l], 1, bl, ll, tq_lat, kc_lat,
                               cache=(kc_da[l], vc_da[l]))
        od = jnp.concatenate([od_c.reshape(1, T, -1), od_l], axis=0)

        zg_ctx = zg.reshape(G * bc, lc, zg.shape[2])
        zab_ctx = zab.reshape(G * bc, lc, 128)
        ob_c, s_gdn = _gdn(zg_ctx, zab_ctx, gdn_conv_w[l], pk["alog"][l], pk["dtb"][l], gdn_norm_w[l], 0, bc, lc)
        ob_l = _gdn(zg, zab, gdn_conv_w[l], pk["alog"][l], pk["dtb"][l], gdn_norm_w[l], 1, bl, ll,
                    s0=s0_bd, layer=l)[0]
        ob = jnp.concatenate([ob_c.reshape(1, T, -1), ob_l], axis=0)
        new_state.append(s_gdn)

        hw = {name: val[l] for name, val in pk["hy"].items()}
        oc_c = _hyena(zhT, lambda p: 0, lambda p: 0, lambda p: 2 * p, lambda p: 2 * p + 1, bc // 2, lc, 128, hw)
        oc_l = _hyena(zhT, lambda p: 1, lambda p: 2, lambda p: 0, lambda p: 0, 1, ll, 64, hw)
        oc_c = jnp.transpose(oc_c, (1, 0, 2)).reshape(1, HY_W, T)
        oc = jnp.concatenate([oc_c, oc_l], axis=0)

        x = _merge(x, mod[l], norm_mix_w[l], oa, ob, oc, od, pk["wgate"][l], wb_bf[l], wo_bf[l], tm)
        x = _mlp(x, mod[l], norm_mlp_w[l], w1_bf[l], w2_bf[l], tm)

    y = _final_norm(x, final_norm_w, tm)
    y_prompt = y[0].reshape(bc, lc, d)
    y_sample = y[1:]
    return (y_prompt, y_sample, jnp.stack(new_ckv, axis=1), jnp.stack(new_kpe, axis=1), jnp.stack(new_dk, axis=1),
            jnp.stack(new_dv, axis=1), jnp.stack(new_state, axis=1))
```

```python
import functools
import math

import numpy as np
import jax
import jax.numpy as jnp
from jax import lax
from jax.experimental import pallas as pl
from jax.experimental.pallas import tpu as pltpu

f32 = jnp.float32
bf16 = jnp.bfloat16

D_MODEL = 1024
GRID_W = 64
N_BRANCH = 4
MLA_H = 4
MLA_NOPE = 64
MLA_ROPE = 32
MLA_V = 64
MLA_Q_LORA = 256
MLA_KV_LORA = 128
GDN_H = 4
GDN_DK = 64
GDN_DV = 64
GDN_CHUNK = 64
HY_W = 256
HY_BANDS = 16
HY_EMB = 1 + 2 * HY_BANDS
HY_FH = 64
HY_SLOW_DECAY = math.log(1e-2) / 1.5
HY_FAST_DECAY = math.log(1e-2) / 0.3
DA_H = 4
DA_DK = 32
DA_DV = 64
D_FF = 4 * D_MODEL
ROPE_BASE = 10000.0
EPS = 1e-6
IN_SPLITS = (MLA_Q_LORA, MLA_KV_LORA, MLA_ROPE,
             GDN_H * GDN_DK, GDN_H * GDN_DK, GDN_H * GDN_DV, GDN_H * GDN_DV, 2 * GDN_H, 2 * GDN_H,
             3 * HY_W,
             DA_H * 2 * DA_DK, DA_H * 2 * DA_DK, DA_H * DA_DV,
             N_BRANCH * D_MODEL)

LOG2E = math.log2(math.e)
LANES = 128
V_ONE = 64
DFT_N2 = 256
VMEM_LIMIT = 56 * 1024 * 1024


def _cparams(sem):
    return pltpu.CompilerParams(dimension_semantics=sem, vmem_limit_bytes=VMEM_LIMIT)


def _dot(a, b):
    return jnp.dot(a.astype(bf16), b.astype(bf16), preferred_element_type=f32)


def _dot_nt(a, b):
    return lax.dot_general(a.astype(bf16), b.astype(bf16), (((1,), (1,)), ((), ())), preferred_element_type=f32)


def _dot_tn(a, b):
    return lax.dot_general(a.astype(bf16), b.astype(bf16), (((0,), (0,)), ((), ())), preferred_element_type=f32)


def _split(x):
    hi = x.astype(bf16)
    lo = (x - hi.astype(f32)).astype(bf16)
    return hi, lo


def _dot3(a, b):
    ah, al = _split(a)
    bh, bl = _split(b)
    return (jnp.dot(ah, bh, preferred_element_type=f32) + jnp.dot(ah, bl, preferred_element_type=f32)
            + jnp.dot(al, bh, preferred_element_type=f32))


def _dot3_w(a, bh, bl):
    ah, al = _split(a)
    return (jnp.dot(ah, bh, preferred_element_type=f32) + jnp.dot(ah, bl, preferred_element_type=f32)
            + jnp.dot(al, bh, preferred_element_type=f32))


def _sigmoid(x):
    return 1.0 / (1.0 + jnp.exp(-x))


def _silu(x):
    return x * _sigmoid(x)


def _softplus(x):
    return jnp.maximum(x, 0.0) + jnp.log(1.0 + jnp.exp(-jnp.abs(x)))


def _rms(x, w):
    return x * lax.rsqrt(jnp.mean(x * x, axis=-1, keepdims=True) + EPS) * w


def _mod_kernel(c_ref, w_ref, b_ref, o_ref):
    c = _silu(c_ref[...])
    o_ref[0] = _dot3(c, w_ref[0]) + b_ref[0]


def _modulation(cond8, w_ada, b_ada):
    depth, d, n6 = w_ada.shape
    tn = 1536
    return pl.pallas_call(
        _mod_kernel,
        grid=(depth, n6 // tn),
        in_specs=[pl.BlockSpec((8, d), lambda l, j: (0, 0)),
                  pl.BlockSpec((1, d, tn), lambda l, j: (l, 0, j)),
                  pl.BlockSpec((1, 1, tn), lambda l, j: (l, 0, j))],
        out_specs=pl.BlockSpec((1, 8, tn), lambda l, j: (l, 0, j)),
        out_shape=jax.ShapeDtypeStruct((depth, 8, n6), f32),
        compiler_params=_cparams(("arbitrary", "arbitrary")),
        name="modulation",
    )(cond8, w_ada, b_ada.reshape(depth, 1, n6))


def _norm_mod(x, nw, scale, shift):
    return _rms(x, nw) * (1.0 + scale) + shift


def _in_kernel(x_ref, mod_ref, nw_ref, wa_ref, wd_ref, wg_ref, wab_ref, whT_ref,
               za_ref, zd_ref, zg_ref, zab_ref, zhT_ref):
    g = pl.program_id(0)
    d = D_MODEL
    mod = mod_ref[pl.ds(g, 1), :]
    h = _norm_mod(x_ref[0], nw_ref[...], mod[:, d:2 * d], mod[:, 0:d]).astype(bf16)
    za_ref[0] = jnp.dot(h, wa_ref[...], preferred_element_type=f32)
    zd_ref[0] = jnp.dot(h, wd_ref[...], preferred_element_type=f32)
    zg_ref[0] = jnp.dot(h, wg_ref[...], preferred_element_type=f32)
    zab_ref[0] = jnp.dot(h, wab_ref[...], preferred_element_type=f32)
    zhT_ref[0] = lax.dot_general(whT_ref[...], h, (((1,), (1,)), ((), ())), preferred_element_type=f32)


def _layer_spec(arr, layer):
    nd = arr.ndim - 1
    return pl.BlockSpec((None,) + arr.shape[1:], lambda *_: (layer,) + (0,) * nd)


def _in_proj(x, mod, nw, wa, wd, wg, wab, whT, layer, tm):
    G, T, d = x.shape
    na, nd, ng, nab, nh = wa.shape[2], wd.shape[2], wg.shape[2], wab.shape[2], whT.shape[1]
    lay = lambda a: _layer_spec(a, layer)
    return pl.pallas_call(
        _in_kernel,
        grid=(G, T // tm),
        in_specs=[pl.BlockSpec((1, tm, d), lambda g, i: (g, i, 0)),
                  lay(mod), pl.BlockSpec((1, d), lambda g, i: (0, 0)), lay(wa), lay(wd), lay(wg), lay(wab), lay(whT)],
        out_specs=[pl.BlockSpec((1, tm, na), lambda g, i: (g, i, 0)),
                   pl.BlockSpec((1, tm, nd), lambda g, i: (g, i, 0)),
                   pl.BlockSpec((1, tm, ng), lambda g, i: (g, i, 0)),
                   pl.BlockSpec((1, tm, nab), lambda g, i: (g, i, 0)),
                   pl.BlockSpec((1, nh, tm), lambda g, i: (g, 0, i))],
        out_shape=[jax.ShapeDtypeStruct((G, T, na), f32), jax.ShapeDtypeStruct((G, T, nd), f32),
                   jax.ShapeDtypeStruct((G, T, ng), f32), jax.ShapeDtypeStruct((G, T, nab), f32),
                   jax.ShapeDtypeStruct((G, nh, T), f32)],
        compiler_params=_cparams(("arbitrary", "arbitrary")),
        name="in_proj",
    )(x, mod, nw.reshape(1, d), wa, wd, wg, wab, whT)


def _ones_col(rows):
    return jnp.where(lax.broadcasted_iota(jnp.int32, (rows, LANES), 1) == V_ONE, 1.0, 0.0)


def _prep_kernel(za_ref, zd_ref, ca_ref, cb_ref, cd_ref, sd_ref, qnw_ref, kvnw_ref, wqa_ref, wqb_ref, wka_ref, wv_ref,
                 q_ref, k_ref, v_ref, ckv_ref, dq1_ref, dq2_ref, dk_ref, dv_ref):
    za = za_ref[0]
    cqn = _rms(za[:, 0:256], qnw_ref[...]).astype(bf16)
    ckv = _rms(za[:, 256:384], kvnw_ref[...])
    ckv_ref[0] = ckv
    ckvb = ckv.astype(bf16)
    ca = ca_ref[0]
    cb = cb_ref[0]
    qa = jnp.dot(cqn, wqa_ref[...], preferred_element_type=f32)
    qb = jnp.dot(cqn, wqb_ref[...], preferred_element_type=f32)
    kn = jnp.dot(ckvb, wka_ref[...], preferred_element_type=f32)
    vv = jnp.dot(ckvb, wv_ref[...], preferred_element_type=f32)
    krope = za[:, 384:512] * ca + za[:, 512:640] * cb
    qs = (MLA_NOPE + MLA_ROPE) ** -0.5 * LOG2E
    ones_col = _ones_col(za.shape[0])
    for h in range(MLA_H):
        sl = slice(128 * h, 128 * (h + 1))
        q_ref[h, 0] = ((qa[:, sl] * ca + qb[:, sl] * cb) * qs).astype(bf16)
        k_ref[h, 0] = (kn[:, sl] + krope).astype(bf16)
        v_ref[h, 0] = (vv[:, sl] + ones_col).astype(bf16)
    zd = zd_ref[0]
    cd = cd_ref[0]
    sd = sd_ref[0]
    dqs = DA_DK ** -0.5 * LOG2E
    dq = (zd[:, 0:256] * cd + zd[:, 256:512] * sd) * dqs
    dk = zd[:, 512:768] * cd + zd[:, 768:1024] * sd
    first = (lax.broadcasted_iota(jnp.int32, dq.shape, 1) & (2 * DA_DK - 1)) < DA_DK
    dq1 = jnp.where(first, dq, 0.0)
    dq2 = jnp.where(first, 0.0, dq)
    for h in range(DA_H):
        sl = slice(64 * h, 64 * (h + 1))
        dq1_ref[h, 0] = dq1[:, sl].astype(bf16)
        dq2_ref[h, 0] = dq2[:, sl].astype(bf16)
        dk_ref[h, 0] = dk[:, sl].astype(bf16)
        dv_ref[h, 0] = (zd[:, 1024 + 128 * h:1024 + 128 * (h + 1)] + ones_col).astype(bf16)


def _prep(za, zd, ca, cb, cd, sd, qnw, kvnw, wqa, wqb, wka, wv, layer, tm):
    G, T, na = za.shape
    nd = zd.shape[2]
    full = lambda shape: pl.BlockSpec(shape, lambda g, i: (0,) * len(shape))
    lay = lambda a: _layer_spec(a, layer)
    tab = lambda w: pl.BlockSpec((1, tm, w), lambda g, i: (jnp.minimum(g, 1), i, 0))
    hm = lambda w: pl.BlockSpec((4, 1, tm, w), lambda g, i: (0, g, i, 0))
    hms = lambda w: jax.ShapeDtypeStruct((4, G, T, w), bf16)
    return pl.pallas_call(
        _prep_kernel,
        grid=(G, T // tm),
        in_specs=[pl.BlockSpec((1, tm, na), lambda g, i: (g, i, 0)),
                  pl.BlockSpec((1, tm, nd), lambda g, i: (g, i, 0)),
                  tab(128), tab(128), tab(256), tab(256),
                  full((1, 256)), full((1, 128)), lay(wqa), lay(wqb), lay(wka), lay(wv)],
        out_specs=[hm(128), hm(128), hm(128), pl.BlockSpec((1, tm, 128), lambda g, i: (g, i, 0)),
                   hm(64), hm(64), hm(64), hm(128)],
        out_shape=[hms(128), hms(128), hms(128), jax.ShapeDtypeStruct((G, T, 128), f32),
                   hms(64), hms(64), hms(64), hms(128)],
        compiler_params=_cparams(("arbitrary", "arbitrary")),
        name="attn_prep",
    )(za, zd, ca, cb, cd, sd, qnw.reshape(1, -1), kvnw.reshape(1, -1), wqa, wqb, wka, wv)


def _cache_kv_kernel(ckv_ref, kpe_ref, wka_ref, wv_ref, k_ref, v_ref):
    ckvb = ckv_ref[0, 0].astype(bf16)
    kn = jnp.dot(ckvb, wka_ref[0], preferred_element_type=f32)
    vv = jnp.dot(ckvb, wv_ref[0], preferred_element_type=f32)
    kpe = kpe_ref[0, 0]
    ones_col = _ones_col(kpe.shape[0])
    for h in range(MLA_H):
        sl = slice(128 * h, 128 * (h + 1))
        k_ref[0, h, 0] = (kn[:, sl] + kpe).astype(bf16)
        v_ref[0, h, 0] = (vv[:, sl] + ones_col).astype(bf16)


def _cache_kv(cache_ckv, cache_kpe_pad, wka, wv):
    bl, depth, p, _ = cache_ckv.shape
    return pl.pallas_call(
        _cache_kv_kernel,
        grid=(depth, bl),
        in_specs=[pl.BlockSpec((1, 1, p, 128), lambda l, b: (b, l, 0, 0)),
                  pl.BlockSpec((1, 1, p, 128), lambda l, b: (b, l, 0, 0)),
                  pl.BlockSpec((1,) + wka.shape[1:], lambda l, b: (l, 0, 0)),
                  pl.BlockSpec((1,) + wv.shape[1:], lambda l, b: (l, 0, 0))],
        out_specs=[pl.BlockSpec((1, 4, 1, p, 128), lambda l, b: (l, 0, b, 0, 0)),
                   pl.BlockSpec((1, 4, 1, p, 128), lambda l, b: (l, 0, b, 0, 0))],
        out_shape=[jax.ShapeDtypeStruct((depth, 4, bl, p, 128), bf16),
                   jax.ShapeDtypeStruct((depth, 4, bl, p, 128), bf16)],
        compiler_params=_cparams(("arbitrary", "arbitrary")),
        name="cache_kv",
    )(cache_ckv, cache_kpe_pad, wka, wv)


ATT_RB = 64


def _attn_scratch(nchains, tq, kc):
    return [pltpu.VMEM((nchains, tq, 1), f32), pltpu.VMEM((nchains, tq, 1), f32),
            pltpu.VMEM((nchains, tq, LANES), f32), pltpu.VMEM((nchains, tq, kc), f32),
            pltpu.VMEM((nchains, tq, kc), bf16)]


def _attn_scan(q_refs, k_ref, v_ref, cache_refs, scratch, kc):
    m_sc, al_sc, acc_sc, s_sc, p_sc = scratch
    nh = k_ref.shape[0]
    lk = k_ref.shape[2]
    m_sc[...] = jnp.full(m_sc.shape, -jnp.inf, f32)
    acc_sc[...] = jnp.zeros(acc_sc.shape, f32)

    tq = m_sc.shape[1]
    chains = [(j * nh + h, q_ref, h) for j, q_ref in enumerate(q_refs) for h in range(nh)]

    def step(get_k, get_v, kw):
        for c, q_ref, h in chains:
            s_sc[c, :, 0:kw] = lax.dot_general(q_ref[h, 0], get_k(h), (((1,), (1,)), ((), ())),
                                               preferred_element_type=f32)
        for c, _, _ in chains:
            for rb in range(tq // ATT_RB):
                rows = slice(rb * ATT_RB, (rb + 1) * ATT_RB)
                s = s_sc[c, rows, 0:kw]
                m_old = m_sc[c, rows, :]
                m_new = jnp.maximum(m_old, jnp.max(s, axis=1, keepdims=True))
                p_sc[c, rows, 0:kw] = jnp.exp2(s - m_new).astype(bf16)
                al_sc[c, rows, :] = jnp.exp2(m_old - m_new)
                m_sc[c, rows, :] = m_new
        for c, _, h in chains:
            acc_sc[c] = al_sc[c] * acc_sc[c] + jnp.dot(p_sc[c, :, 0:kw], get_v(h), preferred_element_type=f32)

    def body(i, carry):
        rows = pl.ds(pl.multiple_of(i * kc, kc), kc)
        step(lambda h: k_ref[h, 0, rows, :], lambda h: v_ref[h, 0, rows, :], kc)
        return carry

    lax.fori_loop(0, lk // kc, body, 0)
    if cache_refs is not None:
        kc_ref, vc_ref = cache_refs
        step(lambda h: kc_ref[h, 0], lambda h: vc_ref[h, 0], kc_ref.shape[2])


def _attn_out(acc):
    return acc[:, 0:V_ONE] / acc[:, V_ONE:V_ONE + 1]


def _mla_attn_kernel(*refs, kc, has_cache):
    if has_cache:
        q_ref, k_ref, v_ref, kc_ref, vc_ref, o_ref, *scratch = refs
        cache_refs = (kc_ref, vc_ref)
    else:
        q_ref, k_ref, v_ref, o_ref, *scratch = refs
        cache_refs = None
    _attn_scan((q_ref,), k_ref, v_ref, cache_refs, scratch, kc)
    acc_sc = scratch[2]
    for h in range(MLA_H):
        o_ref[0, :, MLA_V * h:MLA_V * (h + 1)] = _attn_out(acc_sc[h])


def _mla_attention(q, k, v, seq_off, nseq, L, tq, kc, cache=None):
    H = q.shape[0]
    kv = lambda rows, off: pl.BlockSpec((H, 1, rows, 128), lambda b, i: (0, b + off, 0, 0))
    in_specs = [pl.BlockSpec((H, 1, tq, 128), lambda b, i: (0, b + seq_off, i, 0)), kv(L, seq_off), kv(L, seq_off)]
    args = [q, k, v]
    if cache is not None:
        in_specs += [kv(cache[0].shape[2], 0), kv(cache[0].shape[2], 0)]
        args += list(cache)
    return pl.pallas_call(
        functools.partial(_mla_attn_kernel, kc=kc, has_cache=cache is not None),
        grid=(nseq, L // tq),
        in_specs=in_specs,
        out_specs=pl.BlockSpec((1, tq, H * MLA_V), lambda b, i: (b, i, 0)),
        out_shape=jax.ShapeDtypeStruct((nseq, L, H * MLA_V), f32),
        scratch_shapes=_attn_scratch(H, tq, kc),
        compiler_params=_cparams(("arbitrary", "arbitrary")),
        name="mla_attn",
    )(*args)


def _diff_attn_kernel(*refs, kc, has_cache):
    if has_cache:
        q1_ref, q2_ref, k_ref, v_ref, kc_ref, vc_ref, lam_ref, nw_ref, o_ref, *scratch = refs
        cache_refs = (kc_ref, vc_ref)
    else:
        q1_ref, q2_ref, k_ref, v_ref, lam_ref, nw_ref, o_ref, *scratch = refs
        cache_refs = None
    acc_sc = scratch[2]
    lamv = lam_ref[...]
    lam_init = lamv[4:5, 0:1]
    lam = (jnp.exp(jnp.sum(lamv[0:1] * lamv[1:2], axis=1, keepdims=True))
           - jnp.exp(jnp.sum(lamv[2:3] * lamv[3:4], axis=1, keepdims=True)) + lam_init)
    _attn_scan((q1_ref, q2_ref), k_ref, v_ref, cache_refs, scratch, kc)
    for h in range(DA_H):
        o = _attn_out(acc_sc[h]) - lam * _attn_out(acc_sc[DA_H + h])
        o_ref[0, :, DA_DV * h:DA_DV * (h + 1)] = _rms(o, nw_ref[...]) * (1.0 - lam_init)


def _diff_attention(q1, q2, k, v, lamv, nw, seq_off, nseq, L, tq, kc, cache=None):
    H = q1.shape[0]
    hm = lambda rows, w, off: pl.BlockSpec((H, 1, rows, w), lambda b, i: (0, b + off, 0, 0))
    qs = pl.BlockSpec((H, 1, tq, 64), lambda b, i: (0, b + seq_off, i, 0))
    in_specs = [qs, qs, hm(L, 64, seq_off), hm(L, 128, seq_off)]
    args = [q1, q2, k, v]
    if cache is not None:
        p = cache[0].shape[2]
        in_specs += [hm(p, 64, 0), hm(p, 128, 0)]
        args += list(cache)
    in_specs += [pl.BlockSpec((8, DA_DK), lambda b, i: (0, 0)), pl.BlockSpec((1, DA_DV), lambda b, i: (0, 0))]
    args += [lamv, nw.reshape(1, DA_DV)]
    return pl.pallas_call(
        functools.partial(_diff_attn_kernel, kc=kc, has_cache=cache is not None),
        grid=(nseq, L // tq),
        in_specs=in_specs,
        out_specs=pl.BlockSpec((1, tq, H * DA_DV), lambda b, i: (b, i, 0)),
        out_shape=jax.ShapeDtypeStruct((nseq, L, H * DA_DV), f32),
        scratch_shapes=_attn_scratch(2 * H, tq, kc),
        compiler_params=_cparams(("arbitrary", "arbitrary")),
        name="diff_attn",
    )(*args)


GDN_N = GDN_H * GDN_CHUNK
GDN_INV_BASE = 8
(M_BD, M_EYE, M_BASE, M_OFF8, M_OFF16, M_OFF32, M_DIR) = range(7)
GDN_NMASK = M_DIR + 6


def _gdn_fill_masks(msk):
    n, c = GDN_N, GDN_CHUNK
    rr = lax.broadcasted_iota(jnp.int32, (n, n), 0)
    cc = lax.broadcasted_iota(jnp.int32, (n, n), 1)
    blk = lambda x, s: lax.shift_right_logical(x, int(math.log2(s)))
    bd = blk(rr, c) == blk(cc, c)
    ri = rr & (c - 1)
    cj = cc & (c - 1)

    def put(i, cond):
        msk[i] = jnp.where(cond, 1.0, 0.0)

    put(M_BD, bd)
    put(M_EYE, rr == cc)
    put(M_BASE, blk(rr, GDN_INV_BASE) == blk(cc, GDN_INV_BASE))
    for slot, s in ((M_OFF8, 8), (M_OFF16, 16), (M_OFF32, 32)):
        put(slot, (blk(rr, 2 * s) == blk(cc, 2 * s)) & (blk(rr, s) != blk(cc, s)))
    put(M_DIR + 0, bd & (ri >= cj))
    put(M_DIR + 1, bd & (ri > cj))
    put(M_DIR + 2, bd & (ri <= cj))
    put(M_DIR + 3, bd & (ri <= cj))
    put(M_DIR + 4, bd & (ri < cj))
    put(M_DIR + 5, bd & (ri >= cj))


def _gdn_conv_chunk(x_ref, s, w, r, L):
    c = GDN_CHUNK
    x = x_ref[s, pl.ds(r, c), :]
    prev8 = x_ref[s, pl.ds(pl.multiple_of(jnp.maximum(r - 8, 0), 8), 8), :]
    next8 = x_ref[s, pl.ds(pl.multiple_of(jnp.minimum(r + c, L - 8), 8), 8), :]
    prev = jnp.where(r > 0, prev8[7:8, :], 0.0)
    nxt = jnp.where(r + c < L, next8[0:1, :], 0.0)
    row = lax.broadcasted_iota(jnp.int32, x.shape, 0)
    xp = jnp.where(row == 0, prev, pltpu.roll(x, 1, 0))
    xn = jnp.where(row == c - 1, nxt, pltpu.roll(x, c - 1, 0))
    y = xp * w[0:1] + x * w[1:2] + xn * w[2:3]
    return _silu(y)


def _group_sum(y, bdb):
    hi, lo = _split(y)
    return jnp.dot(hi, bdb, preferred_element_type=f32) + jnp.dot(lo, bdb, preferred_element_type=f32)


(B_K, B_Q, B_KB, B_RU, B_E, B_TRI, B_T, B_P, B_X, B_U, B_W, B_IN, B_VN) = range(13)
GDN_NBUF = 13


def _gdn_chunk_chains(xs, gabs, alog, dtb, S, pools, msk):
    n, c = GDN_N, GDN_CHUNK
    bdb = msk[M_BD].astype(bf16)
    small = tuple({} for _ in xs)

    def stage(fn):
        for ch in range(len(xs)):
            fn(ch, pools[ch], small[ch])

    def prep(ch, B, sm):
        d = ch % 2
        xq, xk, xv = xs[ch]
        bd = msk[M_BD]
        q = xq * lax.rsqrt(_group_sum(xq * xq, bdb) + EPS) * (GDN_DK ** -0.5)
        k = xk * lax.rsqrt(_group_sum(xk * xk, bdb) + EPS)
        gab = gabs[ch]
        lane = lax.broadcasted_iota(jnp.int32, gab.shape, 1)
        gfull = -jnp.exp(alog) * _softplus(gab + dtb)
        bfull = _sigmoid(gab)

        def stack(arr, base):
            return jnp.concatenate([jnp.sum(jnp.where(lane == base + h, arr, 0.0), axis=1, keepdims=True)
                                    for h in range(GDN_H)], axis=0)

        g_stack = stack(gfull, d * GDN_H)
        b_stack = stack(bfull, (2 + d) * GDN_H)
        g_b = jnp.broadcast_to(g_stack, (n, n))
        g_row = jnp.sum(g_b * msk[M_EYE], axis=0, keepdims=True)
        gc_row = jnp.sum(g_b * msk[M_DIR + 3 * d + 2], axis=0, keepdims=True)
        g_rb = jnp.broadcast_to(g_row, (n, n))
        gc_col = jnp.sum(g_rb * msk[M_DIR + 3 * d], axis=1, keepdims=True)
        g_last = jnp.sum(g_rb * bd, axis=1, keepdims=True)
        B[B_E] = jnp.exp(jnp.minimum(gc_col - gc_row, 0.0))
        tile4 = lambda x: jnp.concatenate([x] * GDN_H, axis=0)
        k_bd = tile4(k) * bd
        B[B_K] = k_bd
        B[B_KB] = k_bd * b_stack
        B[B_Q] = tile4(q) * bd
        B[B_RU] = tile4(xv) * (bd * b_stack)
        sm.update(gc_col=gc_col, egc=jnp.exp(gc_col), g_last=g_last)

    def tri(ch, B, sm):
        t = _dot_nt(B[B_KB], B[B_K]) * (B[B_E] * msk[M_DIR + 3 * (ch % 2) + 1])
        B[B_TRI] = t
        nm = -(t * msk[M_BASE])
        B[B_P] = nm
        B[B_T] = msk[M_EYE] + nm

    def intra(ch, B, sm):
        B[B_IN] = _dot_nt(B[B_Q], B[B_K]) * (B[B_E] * msk[M_DIR + 3 * (ch % 2)])

    def square(d, B, sm):
        B[B_P] = _dot(B[B_P], B[B_P])

    def extend(d, B, sm):
        B[B_T] = B[B_T] + _dot(B[B_T], B[B_P])

    stage(prep)
    stage(tri)
    stage(intra)
    for _ in range(2):
        stage(square)
        stage(extend)
    for slot in (M_OFF8, M_OFF16, M_OFF32):
        def cross(d, B, sm, slot=slot):
            B[B_X] = _dot(B[B_T], B[B_TRI] * msk[slot])

        def merge(d, B, sm):
            B[B_T] = B[B_T] - _dot(B[B_X], B[B_T])

        stage(cross)
        stage(merge)

    def solve_u(d, B, sm):
        B[B_U] = _dot(B[B_T], B[B_RU])

    def solve_w(d, B, sm):
        B[B_W] = _dot(B[B_T], B[B_KB] * sm["egc"])

    def v_new(d, B, sm):
        B[B_VN] = B[B_U] - _dot(B[B_W], S[d])

    def out(d, B, sm):
        o_bd = _dot(B[B_Q] * sm["egc"], S[d]) + _dot(B[B_IN], B[B_VN])
        sm["o"] = o_bd[0:c] + o_bd[c:2 * c] + o_bd[2 * c:3 * c] + o_bd[3 * c:4 * c]

    def update(d, B, sm):
        S[d] = (S[d] * jnp.exp(sm["g_last"])
                + _dot_tn(B[B_K] * jnp.exp(sm["g_last"] - sm["gc_col"]), B[B_VN]))

    for fn in (solve_u, solve_w, v_new, out, update):
        stage(fn)
    return [sm["o"] for sm in small]


def _gdn_kernel(*refs, has_s0):
    if has_s0:
        (q_ref, k_ref, v_ref, z_ref, cwq_ref, cwk_ref, cwv_ref, gab_ref, alog_ref, dtb_ref, nw_ref, s0_ref,
         o_ref, of, ob, msk, S, *pools) = refs
        sfin_ref = None
    else:
        (q_ref, k_ref, v_ref, z_ref, cwq_ref, cwk_ref, cwv_ref, gab_ref, alog_ref, dtb_ref, nw_ref,
         o_ref, sfin_ref, of, ob, msk, S, *pools) = refs
    c = GDN_CHUNK
    ns, L = q_ref.shape[0], q_ref.shape[1]
    n = L // c

    @pl.when(pl.program_id(0) == 0)
    def _():
        _gdn_fill_masks(msk)

    alog = alog_ref[...]
    dtb = dtb_ref[...]
    cwq, cwk, cwv = cwq_ref[...], cwk_ref[...], cwv_ref[...]
    if has_s0:
        for s in range(ns):
            S[2 * s:2 * s + 2] = s0_ref[s, 0]
    else:
        S[...] = jnp.zeros(S.shape, f32)

    def body(i, carry):
        starts = (pl.multiple_of(i * c, c), pl.multiple_of((n - 1 - i) * c, c))
        chains = [(s, r) for s in range(ns) for r in starts]
        xs = [(_gdn_conv_chunk(q_ref, s, cwq, r, L), _gdn_conv_chunk(k_ref, s, cwk, r, L),
               _gdn_conv_chunk(v_ref, s, cwv, r, L)) for s, r in chains]
        gabs = [gab_ref[s, pl.ds(r, c), :] for s, r in chains]
        outs = _gdn_chunk_chains(xs, gabs, alog, dtb, S, pools, msk)
        for ch, (s, r) in enumerate(chains):
            (of, ob)[ch % 2][s, pl.ds(r, c), :] = outs[ch]
        return carry

    lax.fori_loop(0, n, body, 0)
    if sfin_ref is not None:
        for s in range(ns):
            for direction in range(2):
                for h in range(GDN_H):
                    sfin_ref[s, direction, h] = S[2 * s + direction, c * h:c * (h + 1), c * h:c * (h + 1)]

    bdb = msk[M_BD].astype(bf16)
    nw = nw_ref[...]

    def norm_gate(i, carry):
        rows = pl.ds(pl.multiple_of(i * c, c), c)
        for s in range(ns):
            o = of[s, rows, :] + ob[s, rows, :]
            ms = _group_sum(o * o, bdb) * (1.0 / GDN_DV)
            o_ref[s, rows, :] = o * lax.rsqrt(ms + EPS) * nw * _silu(z_ref[s, rows, :])
        return carry

    lax.fori_loop(0, n, norm_gate, 0)


def _gdn(zg, zab, cw, alog_row, dtb_row, nw, seq_off, nseq, L, ns, s0=None, layer=0):
    H = GDN_H
    w = H * GDN_DK
    assert seq_off % ns == 0 and nseq % ns == 0
    off = seq_off // ns
    one = pl.Buffered(1)
    tok = lambda k: pl.BlockSpec((ns, L, w), lambda s, k=k: (s + off, 0, k), pipeline_mode=one)
    cws = lambda k: pl.BlockSpec((3, w), lambda s, k=k: (0, k))
    full = lambda shape: pl.BlockSpec(shape, lambda s: (0,) * len(shape))
    in_specs = [tok(0), tok(1), tok(2), tok(3), cws(0), cws(1), cws(2),
                pl.BlockSpec((ns, L, 128), lambda s: (s + off, 0, 0)),
                full((1, 128)), full((1, 128)), full((1, w))]
    args = [zg, zg, zg, zg, cw, cw, cw, zab, alog_row, dtb_row, jnp.tile(nw.reshape(1, GDN_DV), (1, H))]
    out_specs = [pl.BlockSpec((ns, L, w), lambda s: (s, 0, 0))]
    out_shape = [jax.ShapeDtypeStruct((nseq, L, w), f32)]
    if s0 is not None:
        in_specs.append(pl.BlockSpec((ns, 1, 2, GDN_N, GDN_N), lambda s: (s, layer, 0, 0, 0)))
        args.append(s0)
    else:
        out_specs.append(pl.BlockSpec((ns, 2, H, GDN_DK, GDN_DV), lambda s: (s, 0, 0, 0, 0)))
        out_shape.append(jax.ShapeDtypeStruct((nseq, 2, H, GDN_DK, GDN_DV), f32))
    pool = pltpu.VMEM((GDN_NBUF, GDN_N, GDN_N), f32)
    return pl.pallas_call(
        functools.partial(_gdn_kernel, has_s0=s0 is not None),
        grid=(nseq // ns,),
        in_specs=in_specs,
        out_specs=out_specs,
        out_shape=out_shape,
        scratch_shapes=[pltpu.VMEM((ns, L, w), f32), pltpu.VMEM((ns, L, w), f32),
                        pltpu.VMEM((GDN_NMASK, GDN_N, GDN_N), f32), pltpu.VMEM((2 * ns, GDN_N, GDN_N), f32)]
                       + [pool] * (2 * ns),
        compiler_params=_cparams(("arbitrary",)),
        name="gdn",
    )(*args)


def _bitrev(p, bits):
    r = 0
    for _ in range(bits):
        r = (r << 1) | (p & 1)
        p >>= 1
    return r


@functools.lru_cache(maxsize=None)
def _fft_tables(L):
    n = 2 * L
    n2 = DFT_N2
    n1 = n // n2
    bits = n1.bit_length() - 1
    npair = max(n1 // 2, 1)
    sta = np.zeros((max(bits, 1) * npair, 2 * n2), np.float64)
    stb = np.zeros_like(sta)
    for s in range(bits):
        half = n1 >> (s + 1)
        for p in range(npair):
            j = p % half
            ang = -2.0 * np.pi * j / (2 * half)
            wr, wi = np.cos(ang), np.sin(ang)
            sta[s * npair + p, :] = wr
            stb[s * npair + p, :n2] = -wi
            stb[s * npair + p, n2:] = wi
    twa = np.zeros((n1, 2 * n2), np.float64)
    twb = np.zeros_like(twa)
    lanes = np.arange(n2)
    for p in range(n1):
        ang = -2.0 * np.pi * lanes * _bitrev(p, bits) / n
        twa[p, :n2] = np.cos(ang)
        twa[p, n2:] = np.cos(ang)
        twb[p, :n2] = -np.sin(ang)
        twb[p, n2:] = np.sin(ang)
    kn = np.outer(lanes, lanes) * (-2.0 * np.pi / n2)
    fr, fi = np.cos(kn), np.sin(kn)
    fwd = np.block([[fr, fi], [-fi, fr]])
    inv = np.block([[fr, -fi], [fi, fr]])
    as32 = lambda a: np.asarray(a, np.float32)
    return dict(n1=n1, bits=bits, npair=npair, sta=as32(sta), stb=as32(stb), twa=as32(twa), twb=as32(twb),
                fwd=as32(fwd), inv=as32(inv))


@functools.lru_cache(maxsize=None)
def _hyena_pos_table(L):
    n = 2 * L
    idx = np.arange(n)
    pos = np.where(idx < L, idx, n - idx).astype(np.float64)
    pos[L] = 0.0
    t = pos / max(L - 1, 1)
    bands = np.linspace(1e-4, HY_BANDS - 1, HY_BANDS).astype(np.float32).astype(np.float64)
    ang = (2.0 * math.pi * pos / L)[None, :] * bands[:, None]
    z = np.zeros((LANES, n), np.float64)
    z[0] = t
    z[1:1 + HY_BANDS] = np.cos(ang)
    z[1 + HY_BANDS:1 + 2 * HY_BANDS] = -np.sin(ang)
    deltas = np.abs(np.linspace(HY_SLOW_DECAY, HY_FAST_DECAY, HY_W)).reshape(HY_W, 1)
    return np.asarray(z, np.float32), np.asarray(deltas, np.float32)


def _swap_halves(x):
    n2 = x.shape[1] // 2
    return jnp.concatenate([x[:, n2:], x[:, :n2]], axis=1)


def _fft_forward(X, sta_ref, stb_ref, twa_ref, twb_ref, fh_ref, fl_ref, n1, bits, npair, ct, mrows):
    for s in range(bits):
        half = n1 >> (s + 1)

        def pair(p, carry, s=s, half=half):
            grp = p // half
            j = p - grp * half
            a = grp * 2 * half + j
            ra = pl.ds(pl.multiple_of(a * ct, ct), ct)
            rb = pl.ds(pl.multiple_of((a + half) * ct, ct), ct)
            xa = X[ra, :]
            xb = X[rb, :]
            X[ra, :] = xa + xb
            d = xa - xb
            X[rb, :] = d * sta_ref[pl.ds(s * npair + p, 1), :] + _swap_halves(d) * stb_ref[pl.ds(s * npair + p, 1), :]
            return carry

        lax.fori_loop(0, npair, pair, 0)

    def blk(p, carry):
        r = pl.ds(pl.multiple_of(p * ct, ct), ct)
        y = X[r, :]
        X[r, :] = y * twa_ref[pl.ds(p, 1), :] + _swap_halves(y) * twb_ref[pl.ds(p, 1), :]
        return carry

    lax.fori_loop(0, n1, blk, 0)

    def mm(i, carry):
        r = pl.ds(pl.multiple_of(i * mrows, mrows), mrows)
        X[r, :] = _dot3_w(X[r, :], fh_ref[...], fl_ref[...])
        return carry

    lax.fori_loop(0, n1 * ct // mrows, mm, 0)


def _fft_inverse(X, sta_ref, stb_ref, twa_ref, twb_ref, fh_ref, fl_ref, n1, bits, npair, ct, mrows):
    def mm(i, carry):
        r = pl.ds(pl.multiple_of(i * mrows, mrows), mrows)
        X[r, :] = _dot3_w(X[r, :], fh_ref[...], fl_ref[...])
        return carry

    lax.fori_loop(0, n1 * ct // mrows, mm, 0)

    def blk(p, carry):
        r = pl.ds(pl.multiple_of(p * ct, ct), ct)
        y = X[r, :]
        X[r, :] = y * twa_ref[pl.ds(p, 1), :] - _swap_halves(y) * twb_ref[pl.ds(p, 1), :]
        return carry

    lax.fori_loop(0, n1, blk, 0)

    for s in reversed(range(bits)):
        half = n1 >> (s + 1)

        def pair(p, carry, s=s, half=half):
            grp = p // half
            j = p - grp * half
            a = grp * 2 * half + j
            ra = pl.ds(pl.multiple_of(a * ct, ct), ct)
            rb = pl.ds(pl.multiple_of((a + half) * ct, ct), ct)
            xa = X[ra, :]
            xb = X[rb, :]
            tw = xb * sta_ref[pl.ds(s * npair + p, 1), :] - _swap_halves(xb) * stb_ref[pl.ds(s * npair + p, 1), :]
            X[ra, :] = xa + tw
            X[rb, :] = xa - tw
            return carry

        lax.fori_loop(0, npair, pair, 0)


def _conv3_lanes(x, w, b):
    L = x.shape[1]
    lane = lax.broadcasted_iota(jnp.int32, x.shape, 1)
    xp = jnp.where(lane == 0, 0.0, pltpu.roll(x, 1, 1))
    xn = jnp.where(lane == L - 1, 0.0, pltpu.roll(x, L - 1, 1))
    return xp * w[:, 0:1] + x * w[:, 1:2] + xn * w[:, 2:3] + b


def _hyena_hidden_kernel(zt_ref, w1_ref, b1_ref, w2_ref, b2_ref, fr_ref, h_ref):
    fr = fr_ref[...]
    h = jnp.sin(fr * (_dot3(w1_ref[...], zt_ref[...]) + b1_ref[...]))
    h_ref[...] = jnp.sin(fr * (_dot3(w2_ref[...], h) + b2_ref[...]))


def _hyena_hidden(zt, hw):
    nblk = max(zt.shape[1] // 2048, 1)
    tn = zt.shape[1] // nblk
    full = lambda a: pl.BlockSpec(a.shape, lambda j: (0,) * a.ndim)
    names = ("w1T", "b1", "w2T", "b2", "freq")
    return pl.pallas_call(
        _hyena_hidden_kernel,
        grid=(nblk,),
        in_specs=[pl.BlockSpec((zt.shape[0], tn), lambda j: (0, j))] + [full(hw[k]) for k in names],
        out_specs=pl.BlockSpec((HY_FH, tn), lambda j: (0, j)),
        out_shape=jax.ShapeDtypeStruct((HY_FH, zt.shape[1]), f32),
        compiler_params=_cparams(("arbitrary",)),
        name="hyena_hidden",
    )(zt, *[hw[k] for k in names])


def _hyena_kernel(x0a_ref, x1a_ref, va_ref, x0b_ref, x1b_ref, vb_ref, cw0_ref, cw1_ref, cw2_ref,
                  cb0_ref, cb1_ref, cb2_ref, d_ref, t_ref, dl_ref, h_ref,
                  w3f_ref, w3b_ref, sta_ref, stb_ref, twa_ref, twb_ref,
                  ffh_ref, ffl_ref, fih_ref, fil_ref, o_ref, X, HA, HB, *, L, n1, bits, npair, ct, mrows):
    n2 = DFT_N2
    n = 2 * L
    fft_args = (sta_ref, stb_ref, twa_ref, twb_ref)

    @pl.when(pl.program_id(1) == 0)
    def _():
        h = h_ref[...]
        hf = _dot3(w3f_ref[...], h)
        hb = _dot3(w3b_ref[...], h)
        lane = lax.broadcasted_iota(jnp.int32, hf.shape, 1)
        dec = jnp.exp(-t_ref[0:1, :] * dl_ref[...])
        hc = jnp.where(lane < L, hf, jnp.where(lane > L, hb, 0.0)) * dec
        for b in range(n1):
            X[b * ct:(b + 1) * ct, 0:n2] = hc[:, b * n2:(b + 1) * n2]
            X[b * ct:(b + 1) * ct, n2:2 * n2] = jnp.zeros((ct, n2), f32)
        _fft_forward(X, *fft_args, ffh_ref, ffl_ref, n1, bits, npair, ct, mrows)
        hs = X[...]
        HA[...] = jnp.concatenate([hs[:, :n2], hs[:, :n2]], axis=1)
        HB[...] = jnp.concatenate([-hs[:, n2:], hs[:, n2:]], axis=1)

    x0a = _conv3_lanes(x0a_ref[0], cw0_ref[...], cb0_ref[...])
    x0b = _conv3_lanes(x0b_ref[0], cw0_ref[...], cb0_ref[...])
    vva = _conv3_lanes(va_ref[0], cw2_ref[...], cb2_ref[...]) * _conv3_lanes(x1a_ref[0], cw1_ref[...], cb1_ref[...])
    vvb = _conv3_lanes(vb_ref[0], cw2_ref[...], cb2_ref[...]) * _conv3_lanes(x1b_ref[0], cw1_ref[...], cb1_ref[...])
    nb = L // n2
    for b in range(nb):
        X[b * ct:(b + 1) * ct, 0:n2] = vva[:, b * n2:(b + 1) * n2]
        X[b * ct:(b + 1) * ct, n2:2 * n2] = vvb[:, b * n2:(b + 1) * n2]
    X[nb * ct:n1 * ct, :] = jnp.zeros(((n1 - nb) * ct, 2 * n2), f32)
    _fft_forward(X, *fft_args, ffh_ref, ffl_ref, n1, bits, npair, ct, mrows)

    def spec(i, carry):
        r = pl.ds(pl.multiple_of(i * ct, ct), ct)
        x = X[r, :]
        X[r, :] = x * HA[r, :] + _swap_halves(x) * HB[r, :]
        return carry

    lax.fori_loop(0, n1, spec, 0)
    _fft_inverse(X, *fft_args, fih_ref, fil_ref, n1, bits, npair, ct, mrows)
    inv_n = 1.0 / n
    ya = jnp.concatenate([X[b * ct:(b + 1) * ct, 0:n2] for b in range(nb)], axis=1) * inv_n
    yb = jnp.concatenate([X[b * ct:(b + 1) * ct, n2:2 * n2] for b in range(nb)], axis=1) * inv_n
    dcol = d_ref[...]
    o_ref[0] = (ya + vva * dcol) * x0a
    o_ref[1] = (yb + vvb * dcol) * x0b


def _hyena(zhT, grp_a, grp_b, lane_a, lane_b, npairs, L, ct, hw):
    tabs = _fft_tables(L)
    n1, bits, npair = tabs["n1"], tabs["bits"], tabs["npair"]
    n = 2 * L
    ntile = HY_W // ct
    mrows = min(512, n1 * ct)
    zt, deltas = _hyena_pos_table(L)
    fwd = jnp.asarray(tabs["fwd"])
    inv = jnp.asarray(tabs["inv"])
    ffh = fwd.astype(bf16)
    ffl = (fwd - ffh.astype(f32)).astype(bf16)
    fih = inv.astype(bf16)
    fil = (inv - fih.astype(f32)).astype(bf16)
    xin = lambda k, grp, ln: pl.BlockSpec((1, ct, L), lambda j, p, k=k: (grp(p), j + ntile * k, ln(p)))
    chan = lambda k, w: pl.BlockSpec((ct, w), lambda j, p, k=k: (j + ntile * k, 0))
    full = lambda a: pl.BlockSpec(a.shape, lambda j, p: (0,) * a.ndim)
    hidden = _hyena_hidden(jnp.asarray(zt), hw)
    consts = [jnp.asarray(zt[0:8]), jnp.asarray(deltas)]
    in_specs = ([xin(0, grp_a, lane_a), xin(1, grp_a, lane_a), xin(2, grp_a, lane_a),
                 xin(0, grp_b, lane_b), xin(1, grp_b, lane_b), xin(2, grp_b, lane_b),
                 chan(0, 3), chan(1, 3), chan(2, 3), chan(0, 1), chan(1, 1), chan(2, 1), chan(0, 1),
                 full(consts[0]), chan(0, 1), full(hidden),
                 chan(0, HY_FH), chan(1, HY_FH)]
                + [full(jnp.asarray(tabs[k])) for k in ("sta", "stb", "twa", "twb")]
                + [full(ffh), full(ffl), full(fih), full(fil)])
    args = ([zhT] * 6 + [hw["cwT"]] * 3 + [hw["cb"]] * 3 + [hw["d"], consts[0], consts[1], hidden,
            hw["w3T"], hw["w3T"]]
            + [jnp.asarray(tabs[k]) for k in ("sta", "stb", "twa", "twb")] + [ffh, ffl, fih, fil])
    return pl.pallas_call(
        functools.partial(_hyena_kernel, L=L, n1=n1, bits=bits, npair=npair, ct=ct, mrows=mrows),
        grid=(ntile, npairs),
        in_specs=in_specs,
        out_specs=pl.BlockSpec((2, ct, L), lambda j, p: (p, j, 0)),
        out_shape=jax.ShapeDtypeStruct((2 * npairs, HY_W, L), f32),
        scratch_shapes=[pltpu.VMEM((n1 * ct, 2 * DFT_N2), f32)] * 3,
        compiler_params=_cparams(("arbitrary", "arbitrary")),
        name="hyena",
    )(*args)


def _merge_kernel(x_ref, mod_ref, nw_ref, oa_c, oa_l, ob_c, ob_l, oc_c, oc_l, od_c, od_l, wg_ref, wb_ref, wo_ref,
                  xo_ref):
    g = pl.program_id(0)
    d = D_MODEL
    x = x_ref[0]
    mod = mod_ref[pl.ds(g, 1), :]
    h = _norm_mod(x, nw_ref[...], mod[:, d:2 * d], mod[:, 0:d]).astype(bf16)
    ctx = g == 0
    pick = lambda c_ref, l_ref: jnp.where(ctx, c_ref[0], l_ref[0]).astype(bf16)
    oc_ctx = jnp.concatenate([oc_c[s] for s in range(oc_c.shape[0])], axis=1)
    oc = jnp.where(ctx, oc_ctx, oc_l[0]).astype(bf16)
    pa = jnp.dot(pick(oa_c, oa_l), wb_ref[0], preferred_element_type=f32)
    pb = jnp.dot(pick(ob_c, ob_l), wb_ref[1], preferred_element_type=f32)
    pc = lax.dot_general(oc, wb_ref[2], (((0,), (0,)), ((), ())), preferred_element_type=f32)
    pd = jnp.dot(pick(od_c, od_l), wb_ref[3], preferred_element_type=f32)
    acc = jnp.zeros_like(x)
    for nbr, proj in enumerate((pa, pb, pc, pd)):
        gate = _sigmoid(jnp.dot(h, wg_ref[:, nbr * d:(nbr + 1) * d], preferred_element_type=f32))
        acc = acc + gate * proj
    xo_ref[0] = x + mod[:, 2 * d:3 * d] * _dot(acc, wo_ref[...])


def _merge(x, mod, nw, branches, wg, wb, wo, layer, tm):
    G, T, d = x.shape
    (oa_c, oa_l), (ob_c, ob_l), (oc_c, oc_l), (od_c, od_l) = branches
    lc = oc_c.shape[2]
    tok = lambda w: pl.BlockSpec((1, tm, w), lambda g, i: (g, i, 0))
    ctx_i = lambda g, i: jnp.where(g == 0, i, 0)
    lat_g = lambda g: jnp.maximum(g - 1, 0)
    lat_i = lambda g, i: jnp.where(g == 0, 0, i)
    tok_c = pl.BlockSpec((1, tm, 256), lambda g, i: (0, ctx_i(g, i), 0))
    tok_l = pl.BlockSpec((1, tm, 256), lambda g, i: (lat_g(g), lat_i(g, i), 0))
    lay = lambda a: _layer_spec(a, layer)
    return pl.pallas_call(
        _merge_kernel,
        grid=(G, T // tm),
        in_specs=[tok(d), lay(mod), pl.BlockSpec((1, d), lambda g, i: (0, 0)),
                  tok_c, tok_l, tok_c, tok_l,
                  pl.BlockSpec((tm // lc, HY_W, lc), lambda g, i: (ctx_i(g, i), 0, 0)),
                  pl.BlockSpec((1, HY_W, tm), lambda g, i: (lat_g(g), 0, lat_i(g, i))),
                  tok_c, tok_l, lay(wg), lay(wb), lay(wo)],
        out_specs=tok(d),
        out_shape=jax.ShapeDtypeStruct((G, T, d), f32),
        compiler_params=_cparams(("arbitrary", "arbitrary")),
        name="merge",
    )(x, mod, nw.reshape(1, d), oa_c.reshape(1, T, -1), oa_l, ob_c.reshape(1, T, -1), ob_l, oc_c, oc_l,
      od_c.reshape(1, T, -1), od_l, wg, wb, wo)


def _mlp_kernel(x_ref, mod_ref, nw_ref, w1_ref, w2_ref, xo_ref):
    g = pl.program_id(0)
    d = D_MODEL
    x = x_ref[0]
    mod = mod_ref[pl.ds(g, 1), :]
    h = _norm_mod(x, nw_ref[...], mod[:, 4 * d:5 * d], mod[:, 3 * d:4 * d]).astype(bf16)
    acc = jnp.zeros_like(x)
    for c in range(D_FF // d):
        a = jnp.maximum(jnp.dot(h, w1_ref[:, c * d:(c + 1) * d], preferred_element_type=f32), 0.0)
        acc = acc + _dot(a * a, w2_ref[c * d:(c + 1) * d, :])
    xo_ref[0] = x + mod[:, 5 * d:6 * d] * acc


def _mlp(x, mod, nw, w1, w2, layer, tm):
    G, T, d = x.shape
    tok = pl.BlockSpec((1, tm, d), lambda g, i: (g, i, 0))
    return pl.pallas_call(
        _mlp_kernel,
        grid=(G, T // tm),
        in_specs=[tok, _layer_spec(mod, layer), pl.BlockSpec((1, d), lambda g, i: (0, 0)),
                  _layer_spec(w1, layer), _layer_spec(w2, layer)],
        out_specs=tok,
        out_shape=jax.ShapeDtypeStruct((G, T, d), f32),
        compiler_params=_cparams(("arbitrary", "arbitrary")),
        name="mlp",
    )(x, mod, nw.reshape(1, d), w1, w2)


def _final_kernel(x_ref, nw_ref, o_ref):
    o_ref[0] = _rms(x_ref[0], nw_ref[...])


def _final_norm(x, nw, tm):
    G, T, d = x.shape
    tok = pl.BlockSpec((1, tm, d), lambda g, i: (g, i, 0))
    return pl.pallas_call(
        _final_kernel,
        grid=(G, T // tm),
        in_specs=[tok, pl.BlockSpec((1, d), lambda g, i: (0, 0))],
        out_specs=tok,
        out_shape=jax.ShapeDtypeStruct((G, T, d), f32),
        compiler_params=_cparams(("arbitrary", "arbitrary")),
        name="final_norm",
    )(x, nw.reshape(1, d))


@functools.lru_cache(maxsize=None)
def _rope_tables(T):
    m = MLA_ROPE // 4
    inv = ROPE_BASE ** (-np.arange(m, dtype=np.float64) / m)
    rows = T // GRID_W
    row_pos = np.repeat(np.arange(rows), GRID_W)[:, None] * inv
    col_pos = np.tile(np.arange(GRID_W), rows)[:, None] * inv
    cos32 = np.concatenate([np.cos(row_pos), np.cos(row_pos), np.cos(col_pos), np.cos(col_pos)], axis=1)
    sin32 = np.concatenate([-np.sin(row_pos), np.sin(row_pos), -np.sin(col_pos), np.sin(col_pos)], axis=1)
    ca = np.zeros((2, T, 128))
    cb = np.zeros((2, T, 128))
    ca[:, :, 0:96] = 1.0
    ca[1, :, 64:96] = cos32
    cb[1, :, 64:96] = sin32
    cd = np.ones((2, T, 256))
    sd = np.zeros((2, T, 256))
    cd[1] = np.tile(cos32, (1, 8))
    sd[1] = np.tile(sin32, (1, 8))
    return tuple(np.asarray(a, np.float32) for a in (ca, cb, cd, sd))


def _swap_perm(width):
    base = np.concatenate([np.arange(8, 16), np.arange(0, 8), np.arange(24, 32), np.arange(16, 24)])
    return np.concatenate([base + 32 * s for s in range(width // 32)])


def _pack_weights(w_in, mla_w_uq, mla_w_ukv, gdn_conv_w, gdn_a_log, gdn_dt_bias, hy_conv_w, hy_conv_b,
                  hy_f_w1, hy_f_b1, hy_f_w2, hy_f_b2, hy_f_w3, hy_f_freq, hy_d):
    depth = w_in.shape[0]
    offs = [0] + [int(s) for s in np.cumsum(IN_SPLITS)]
    seg = lambda i: w_in[:, :, offs[i]:offs[i + 1]]
    zeros = lambda n: jnp.zeros((depth, D_MODEL, n), w_in.dtype)
    kpe = seg(2)
    kpe_sw = kpe[:, :, _swap_perm(MLA_ROPE)]
    wa = jnp.concatenate([seg(0), seg(1), zeros(64), kpe, zeros(32), zeros(64), kpe_sw, zeros(32)], axis=2)
    dq, dk, dv = seg(10), seg(11), seg(12)
    perm = _swap_perm(256)
    dv_slots = jnp.concatenate([dv.reshape(depth, D_MODEL, DA_H, DA_DV),
                                jnp.zeros((depth, D_MODEL, DA_H, LANES - DA_DV), dv.dtype)],
                               axis=3).reshape(depth, D_MODEL, DA_H * LANES)
    wd = jnp.concatenate([dq, dq[:, :, perm], dk, dk[:, :, perm], dv_slots], axis=2)
    wg = jnp.concatenate([seg(3), seg(4), seg(5), seg(6)], axis=2)
    wab = jnp.concatenate([seg(7), seg(8), zeros(128 - 4 * GDN_H)], axis=2)
    whT = jnp.swapaxes(seg(9), 1, 2)
    wgate = seg(13)
    uq = mla_w_uq.reshape(depth, MLA_Q_LORA, MLA_H, MLA_NOPE + MLA_ROPE)
    z32 = jnp.zeros((depth, MLA_Q_LORA, MLA_H, 32), uq.dtype)
    z64 = jnp.zeros((depth, MLA_Q_LORA, MLA_H, 64), uq.dtype)
    rope_sw = uq[..., MLA_NOPE:][..., _swap_perm(MLA_ROPE)]
    wqa = jnp.concatenate([uq, z32], axis=3).reshape(depth, MLA_Q_LORA, MLA_H * 128)
    wqb = jnp.concatenate([z64, rope_sw, z32], axis=3).reshape(depth, MLA_Q_LORA, MLA_H * 128)
    ukv = mla_w_ukv.reshape(depth, MLA_KV_LORA, MLA_H, MLA_NOPE + MLA_V)
    wka = jnp.concatenate([ukv[..., :MLA_NOPE], jnp.zeros((depth, MLA_KV_LORA, MLA_H, 64), ukv.dtype)],
                          axis=3).reshape(depth, MLA_KV_LORA, MLA_H * 128)
    wv = jnp.concatenate([ukv[..., MLA_NOPE:], jnp.zeros((depth, MLA_KV_LORA, MLA_H, LANES - MLA_V), ukv.dtype)],
                         axis=3).reshape(depth, MLA_KV_LORA, MLA_H * LANES)
    cast = lambda a: a.astype(bf16)
    pad128 = lambda a: jnp.pad(a.reshape(depth, 1, -1), ((0, 0), (0, 0), (0, 128 - a.shape[1] * a.shape[2])))
    hy = dict(
        cwT=jnp.swapaxes(hy_conv_w, 1, 2),
        cb=hy_conv_b.reshape(depth, -1, 1),
        d=hy_d.reshape(depth, HY_W, 1),
        w1T=jnp.pad(jnp.swapaxes(hy_f_w1, 1, 2), ((0, 0), (0, 0), (0, LANES - HY_EMB))),
        b1=hy_f_b1.reshape(depth, HY_FH, 1),
        w2T=jnp.swapaxes(hy_f_w2, 1, 2),
        b2=hy_f_b2.reshape(depth, HY_FH, 1),
        w3T=jnp.swapaxes(hy_f_w3, 1, 2),
        freq=hy_f_freq.reshape(depth, HY_FH, 1),
    )
    return dict(wa=cast(wa), wd=cast(wd), wg=cast(wg), wab=cast(wab), whT=cast(whT), wgate=cast(wgate),
                wqa=cast(wqa), wqb=cast(wqb), wka=cast(wka), wv=cast(wv),
                alog=pad128(gdn_a_log), dtb=pad128(gdn_dt_bias), hy=hy)


def kernel(x_prompt, x_sample, cache_mla_ckv, cache_mla_kpe, cache_diff_k, cache_diff_v, state_gdn, c, c_ctx, w_ada, b_ada, norm_mix_w, norm_mlp_w, w_in, mla_q_norm_w, mla_w_uq, mla_kv_norm_w, mla_w_ukv, gdn_conv_w, gdn_a_log, gdn_dt_bias, gdn_norm_w, hy_conv_w, hy_conv_b, hy_f_w1, hy_f_b1, hy_f_w2, hy_f_b2, hy_f_w3, hy_f_freq, hy_d, da_lq1, da_lk1, da_lq2, da_lk2, da_norm_w, w_branch, w_out, mlp_w1, mlp_w2, final_norm_w):
    depth = w_in.shape[0]
    bc, lc, d = x_prompt.shape
    bl, ll, _ = x_sample.shape
    T = ll
    assert bc * lc == T and d == D_MODEL and bc % 2 == 0 and bl == 2
    G = 1 + bl
    past = cache_mla_ckv.shape[2]
    tm = min(512, T)
    tq = min(256, lc)
    tq_lat, kc_lat = min(512, ll), min(512, ll)

    pk = _pack_weights(w_in, mla_w_uq, mla_w_ukv, gdn_conv_w, gdn_a_log, gdn_dt_bias, hy_conv_w, hy_conv_b,
                       hy_f_w1, hy_f_b1, hy_f_w2, hy_f_b2, hy_f_w3, hy_f_freq, hy_d)
    wb_bf = w_branch.astype(bf16)
    wo_bf = w_out.astype(bf16)
    w1_bf = mlp_w1.astype(bf16)
    w2_bf = mlp_w2.astype(bf16)
    ca, cb, cd, sd = (jnp.asarray(t) for t in _rope_tables(T))

    cond8 = jnp.concatenate([c_ctx.reshape(1, d), c, jnp.zeros((8 - G, d), f32)], axis=0)
    mod = _modulation(cond8, w_ada, b_ada)

    kpe_pad = jnp.pad(cache_mla_kpe, ((0, 0), (0, 0), (0, 0), (MLA_NOPE, 128 - MLA_NOPE - MLA_ROPE)))
    kc_mla, vc_mla = _cache_kv(cache_mla_ckv, kpe_pad, pk["wka"], pk["wv"])
    kc_da = jnp.transpose(cache_diff_k.reshape(bl, depth, past, DA_H, 2 * DA_DK), (1, 3, 0, 2, 4)).astype(bf16)
    vc_da = jnp.transpose(cache_diff_v, (1, 3, 0, 2, 4))
    vc_da = jnp.concatenate([vc_da, jnp.ones(vc_da.shape[:-1] + (1,), f32),
                             jnp.zeros(vc_da.shape[:-1] + (LANES - DA_DV - 1,), f32)], axis=-1).astype(bf16)
    s0_bd = jnp.einsum('bldhij,hg->bldhigj', state_gdn, jnp.eye(GDN_H, dtype=f32)).reshape(
        bl, depth, 2, GDN_N, GDN_N)

    x = jnp.concatenate([x_prompt.reshape(1, T, d), x_sample], axis=0)
    new_ckv, new_kpe, new_dk, new_dv, new_state = [], [], [], [], []
    for l in range(depth):
        lam_init = 0.8 - 0.6 * math.exp(-0.3 * l)
        za, zd, zg, zab, zhT = _in_proj(x, mod, norm_mix_w[l], pk["wa"], pk["wd"], pk["wg"], pk["wab"], pk["whT"],
                                        l, tm)
        q, k, v, ckv, dq1, dq2, dk, dv = _prep(za, zd, ca, cb, cd, sd, mla_q_norm_w[l], mla_kv_norm_w[l],
                                               pk["wqa"], pk["wqb"], pk["wka"], pk["wv"], l, tm)
        new_ckv.append(ckv[0].reshape(bc, lc, MLA_KV_LORA))
        new_kpe.append(za[0, :, 384 + MLA_NOPE:384 + MLA_NOPE + MLA_ROPE].reshape(bc, lc, MLA_ROPE))
        new_dk.append(zd[0, :, 512:768].reshape(bc, lc, DA_H, 2, DA_DK))
        new_dv.append(zd[0, :, 1024:].reshape(bc, lc, DA_H, LANES)[..., :DA_DV])

        ctx_view = lambda a: a.reshape(a.shape[0], G * bc, lc, a.shape[3])
        oa_c = _mla_attention(ctx_view(q), ctx_view(k), ctx_view(v), 0, bc, lc, tq, lc)
        oa_l = _mla_attention(q, k, v, 1, bl, ll, tq_lat, kc_lat, cache=(kc_mla[l], vc_mla[l]))

        lamv = jnp.concatenate([da_lq1[l][None], da_lk1[l][None], da_lq2[l][None], da_lk2[l][None],
                                jnp.full((1, DA_DK), lam_init, f32), jnp.zeros((3, DA_DK), f32)], axis=0)
        od_c = _diff_attention(ctx_view(dq1), ctx_view(dq2), ctx_view(dk), ctx_view(dv), lamv, da_norm_w[l],
                               0, bc, lc, tq, lc)
        od_l = _diff_attention(dq1, dq2, dk, dv, lamv, da_norm_w[l], 1, bl, ll, tq_lat, kc_lat,
                               cache=(kc_da[l], vc_da[l]))

        zg_ctx = zg.reshape(G * bc, lc, zg.shape[2])
        zab_ctx = zab.reshape(G * bc, lc, 128)
        ob_c, s_gdn = _gdn(zg_ctx, zab_ctx, gdn_conv_w[l], pk["alog"][l], pk["dtb"][l], gdn_norm_w[l], 0, bc, lc, 2)
        ob_l = _gdn(zg, zab, gdn_conv_w[l], pk["alog"][l], pk["dtb"][l], gdn_norm_w[l], 1, bl, ll, 1,
                    s0=s0_bd, layer=l)[0]
        new_state.append(s_gdn)

        hw = {name: val[l] for name, val in pk["hy"].items()}
        oc_c = _hyena(zhT, lambda p: 0, lambda p: 0, lambda p: 2 * p, lambda p: 2 * p + 1, bc // 2, lc, 128, hw)
        oc_l = _hyena(zhT, lambda p: 1, lambda p: 2, lambda p: 0, lambda p: 0, 1, ll, 64, hw)

        x = _merge(x, mod, norm_mix_w[l], ((oa_c, oa_l), (ob_c, ob_l), (oc_c, oc_l), (od_c, od_l)),
                   pk["wgate"], wb_bf, wo_bf, l, tm)
        x = _mlp(x, mod, norm_mlp_w[l], w1_bf, w2_bf, l, tm)

    y = _final_norm(x, final_norm_w, tm)
    y_prompt = y[0].reshape(bc, lc, d)
    y_sample = y[1:]
    return (y_prompt, y_sample, jnp.stack(new_ckv, axis=1), jnp.stack(new_kpe, axis=1), jnp.stack(new_dk, axis=1),
            jnp.stack(new_dv, axis=1), jnp.stack(new_state, axis=1))
```

```python
import functools
import math

import numpy as np
import jax
import jax.numpy as jnp
from jax import lax
from jax.experimental import pallas as pl
from jax.experimental.pallas import tpu as pltpu

f32 = jnp.float32
bf16 = jnp.bfloat16

D_MODEL = 1024
GRID_W = 64
N_BRANCH = 4
MLA_H = 4
MLA_NOPE = 64
MLA_ROPE = 32
MLA_V = 64
MLA_Q_LORA = 256
MLA_KV_LORA = 128
GDN_H = 4
GDN_DK = 64
GDN_DV = 64
GDN_CHUNK = 64
HY_W = 256
HY_BANDS = 16
HY_EMB = 1 + 2 * HY_BANDS
HY_FH = 64
HY_SLOW_DECAY = math.log(1e-2) / 1.5
HY_FAST_DECAY = math.log(1e-2) / 0.3
DA_H = 4
DA_DK = 32
DA_DV = 64
D_FF = 4 * D_MODEL
ROPE_BASE = 10000.0
EPS = 1e-6
IN_SPLITS = (MLA_Q_LORA, MLA_KV_LORA, MLA_ROPE,
             GDN_H * GDN_DK, GDN_H * GDN_DK, GDN_H * GDN_DV, GDN_H * GDN_DV, 2 * GDN_H, 2 * GDN_H,
             3 * HY_W,
             DA_H * 2 * DA_DK, DA_H * 2 * DA_DK, DA_H * DA_DV,
             N_BRANCH * D_MODEL)

LOG2E = math.log2(math.e)
LANES = 128
V_ONE = 64
DFT_N2 = 256
VMEM_LIMIT = 56 * 1024 * 1024


def _cparams(sem):
    return pltpu.CompilerParams(dimension_semantics=sem, vmem_limit_bytes=VMEM_LIMIT)


def _dot(a, b):
    return jnp.dot(a.astype(bf16), b.astype(bf16), preferred_element_type=f32)


def _dot_nt(a, b):
    return lax.dot_general(a.astype(bf16), b.astype(bf16), (((1,), (1,)), ((), ())), preferred_element_type=f32)


def _dot_tn(a, b):
    return lax.dot_general(a.astype(bf16), b.astype(bf16), (((0,), (0,)), ((), ())), preferred_element_type=f32)


def _split(x):
    hi = x.astype(bf16)
    lo = (x - hi.astype(f32)).astype(bf16)
    return hi, lo


def _dot3(a, b):
    ah, al = _split(a)
    bh, bl = _split(b)
    return (jnp.dot(ah, bh, preferred_element_type=f32) + jnp.dot(ah, bl, preferred_element_type=f32)
            + jnp.dot(al, bh, preferred_element_type=f32))


def _dot3_w(a, bh, bl):
    ah, al = _split(a)
    return (jnp.dot(ah, bh, preferred_element_type=f32) + jnp.dot(ah, bl, preferred_element_type=f32)
            + jnp.dot(al, bh, preferred_element_type=f32))


def _sigmoid(x):
    return 1.0 / (1.0 + jnp.exp(-x))


def _silu(x):
    return x * _sigmoid(x)


def _softplus(x):
    return jnp.maximum(x, 0.0) + jnp.log(1.0 + jnp.exp(-jnp.abs(x)))


def _rms(x, w):
    return x * lax.rsqrt(jnp.mean(x * x, axis=-1, keepdims=True) + EPS) * w


def _mod_kernel(c_ref, w_ref, b_ref, o_ref):
    c = _silu(c_ref[...])
    o_ref[0] = _dot3(c, w_ref[0]) + b_ref[0]


def _modulation(cond8, w_ada, b_ada):
    depth, d, n6 = w_ada.shape
    tn = 1536
    return pl.pallas_call(
        _mod_kernel,
        grid=(depth, n6 // tn),
        in_specs=[pl.BlockSpec((8, d), lambda l, j: (0, 0)),
                  pl.BlockSpec((1, d, tn), lambda l, j: (l, 0, j)),
                  pl.BlockSpec((1, 1, tn), lambda l, j: (l, 0, j))],
        out_specs=pl.BlockSpec((1, 8, tn), lambda l, j: (l, 0, j)),
        out_shape=jax.ShapeDtypeStruct((depth, 8, n6), f32),
        compiler_params=_cparams(("arbitrary", "arbitrary")),
        name="modulation",
    )(cond8, w_ada, b_ada.reshape(depth, 1, n6))


def _norm_mod(x, nw, scale, shift):
    return _rms(x, nw) * (1.0 + scale) + shift


def _in_kernel(x_ref, mod_ref, nw_ref, wa_ref, wd_ref, wg_ref, wab_ref, whT_ref,
               ca_ref, cb_ref, cd_ref, sd_ref, qnw_ref, kvnw_ref, wqa_ref, wqb_ref, wka_ref, wv_ref,
               zg_ref, zab_ref, zhT_ref, kpe_ref, rdk_ref, rdv_ref,
               q_ref, k_ref, v_ref, ckv_ref, dq1_ref, dq2_ref, dk_ref, dv_ref):
    g = pl.program_id(0)
    d = D_MODEL
    mod = mod_ref[pl.ds(g, 1), :]
    h = _norm_mod(x_ref[0], nw_ref[...], mod[:, d:2 * d], mod[:, 0:d]).astype(bf16)
    zg_ref[0] = jnp.dot(h, wg_ref[...], preferred_element_type=f32)
    zab_ref[0] = jnp.dot(h, wab_ref[...], preferred_element_type=f32)
    zhT_ref[0] = lax.dot_general(whT_ref[...], h, (((1,), (1,)), ((), ())), preferred_element_type=f32)
    za = jnp.dot(h, wa_ref[...], preferred_element_type=f32)
    zd = jnp.dot(h, wd_ref[...], preferred_element_type=f32)
    kpe_ref[0] = za[:, 384:512]
    rdk_ref[0] = zd[:, 512:768]
    rdv_ref[0] = zd[:, 1024:1024 + DA_H * LANES]
    _attn_operands(za, zd, ca_ref, cb_ref, cd_ref, sd_ref, qnw_ref, kvnw_ref, wqa_ref, wqb_ref, wka_ref, wv_ref,
                   q_ref, k_ref, v_ref, ckv_ref, dq1_ref, dq2_ref, dk_ref, dv_ref)


def _layer_spec(arr, layer):
    nd = arr.ndim - 1
    return pl.BlockSpec((None,) + arr.shape[1:], lambda *_: (layer,) + (0,) * nd)


def _in_proj(x, mod, nw, wa, wd, wg, wab, whT, ca, cb, cd, sd, qnw, kvnw, wqa, wqb, wka, wv, layer, tm):
    G, T, d = x.shape
    ng, nab, nh = wg.shape[2], wab.shape[2], whT.shape[1]
    lay = lambda a: _layer_spec(a, layer)
    full = lambda shape: pl.BlockSpec(shape, lambda g, i: (0,) * len(shape))
    tok = lambda w: pl.BlockSpec((1, tm, w), lambda g, i: (g, i, 0))
    tab = lambda w: pl.BlockSpec((1, tm, w), lambda g, i: (jnp.minimum(g, 1), i, 0))
    hm = lambda w: pl.BlockSpec((4, 1, tm, w), lambda g, i: (0, g, i, 0))
    hms = lambda w: jax.ShapeDtypeStruct((4, G, T, w), bf16)
    toks = lambda w: jax.ShapeDtypeStruct((G, T, w), f32)
    return pl.pallas_call(
        _in_kernel,
        grid=(G, T // tm),
        in_specs=[tok(d), lay(mod), full((1, d)), lay(wa), lay(wd), lay(wg), lay(wab), lay(whT),
                  tab(128), tab(128), tab(256), tab(256), full((1, 256)), full((1, 128)),
                  lay(wqa), lay(wqb), lay(wka), lay(wv)],
        out_specs=[tok(ng), tok(nab), pl.BlockSpec((1, nh, tm), lambda g, i: (g, 0, i)),
                   tok(128), tok(256), tok(DA_H * LANES),
                   hm(128), hm(128), hm(128), tok(128), hm(64), hm(64), hm(64), hm(128)],
        out_shape=[toks(ng), toks(nab), jax.ShapeDtypeStruct((G, nh, T), f32),
                   toks(128), toks(256), toks(DA_H * LANES),
                   hms(128), hms(128), hms(128), toks(128), hms(64), hms(64), hms(64), hms(128)],
        compiler_params=_cparams(("arbitrary", "arbitrary")),
        name="in_proj",
    )(x, mod, nw.reshape(1, d), wa, wd, wg, wab, whT, ca, cb, cd, sd, qnw.reshape(1, -1), kvnw.reshape(1, -1),
      wqa, wqb, wka, wv)


def _ones_col(rows):
    return jnp.where(lax.broadcasted_iota(jnp.int32, (rows, LANES), 1) == V_ONE, 1.0, 0.0)


def _attn_operands(za, zd, ca_ref, cb_ref, cd_ref, sd_ref, qnw_ref, kvnw_ref, wqa_ref, wqb_ref, wka_ref, wv_ref,
                   q_ref, k_ref, v_ref, ckv_ref, dq1_ref, dq2_ref, dk_ref, dv_ref):
    cqn = _rms(za[:, 0:256], qnw_ref[...]).astype(bf16)
    ckv = _rms(za[:, 256:384], kvnw_ref[...])
    ckv_ref[0] = ckv
    ckvb = ckv.astype(bf16)
    ca = ca_ref[0]
    cb = cb_ref[0]
    qa = jnp.dot(cqn, wqa_ref[...], preferred_element_type=f32)
    qb = jnp.dot(cqn, wqb_ref[...], preferred_element_type=f32)
    kn = jnp.dot(ckvb, wka_ref[...], preferred_element_type=f32)
    vv = jnp.dot(ckvb, wv_ref[...], preferred_element_type=f32)
    krope = za[:, 384:512] * ca + za[:, 512:640] * cb
    qs = (MLA_NOPE + MLA_ROPE) ** -0.5 * LOG2E
    ones_col = _ones_col(za.shape[0])
    for h in range(MLA_H):
        sl = slice(128 * h, 128 * (h + 1))
        q_ref[h, 0] = ((qa[:, sl] * ca + qb[:, sl] * cb) * qs).astype(bf16)
        k_ref[h, 0] = (kn[:, sl] + krope).astype(bf16)
        v_ref[h, 0] = (vv[:, sl] + ones_col).astype(bf16)
    cd = cd_ref[0]
    sd = sd_ref[0]
    dqs = DA_DK ** -0.5 * LOG2E
    dq = (zd[:, 0:256] * cd + zd[:, 256:512] * sd) * dqs
    dk = zd[:, 512:768] * cd + zd[:, 768:1024] * sd
    first = (lax.broadcasted_iota(jnp.int32, dq.shape, 1) & (2 * DA_DK - 1)) < DA_DK
    dq1 = jnp.where(first, dq, 0.0)
    dq2 = jnp.where(first, 0.0, dq)
    for h in range(DA_H):
        sl = slice(64 * h, 64 * (h + 1))
        dq1_ref[h, 0] = dq1[:, sl].astype(bf16)
        dq2_ref[h, 0] = dq2[:, sl].astype(bf16)
        dk_ref[h, 0] = dk[:, sl].astype(bf16)
        dv_ref[h, 0] = (zd[:, 1024 + 128 * h:1024 + 128 * (h + 1)] + ones_col).astype(bf16)


def _cache_kv_kernel(ckv_ref, kpe_ref, wka_ref, wv_ref, k_ref, v_ref):
    ckvb = ckv_ref[0, 0].astype(bf16)
    kn = jnp.dot(ckvb, wka_ref[0], preferred_element_type=f32)
    vv = jnp.dot(ckvb, wv_ref[0], preferred_element_type=f32)
    kpe = kpe_ref[0, 0]
    ones_col = _ones_col(kpe.shape[0])
    for h in range(MLA_H):
        sl = slice(128 * h, 128 * (h + 1))
        k_ref[0, h, 0] = (kn[:, sl] + kpe).astype(bf16)
        v_ref[0, h, 0] = (vv[:, sl] + ones_col).astype(bf16)


def _cache_kv(cache_ckv, cache_kpe_pad, wka, wv):
    bl, depth, p, _ = cache_ckv.shape
    return pl.pallas_call(
        _cache_kv_kernel,
        grid=(depth, bl),
        in_specs=[pl.BlockSpec((1, 1, p, 128), lambda l, b: (b, l, 0, 0)),
                  pl.BlockSpec((1, 1, p, 128), lambda l, b: (b, l, 0, 0)),
                  pl.BlockSpec((1,) + wka.shape[1:], lambda l, b: (l, 0, 0)),
                  pl.BlockSpec((1,) + wv.shape[1:], lambda l, b: (l, 0, 0))],
        out_specs=[pl.BlockSpec((1, 4, 1, p, 128), lambda l, b: (l, 0, b, 0, 0)),
                   pl.BlockSpec((1, 4, 1, p, 128), lambda l, b: (l, 0, b, 0, 0))],
        out_shape=[jax.ShapeDtypeStruct((depth, 4, bl, p, 128), bf16),
                   jax.ShapeDtypeStruct((depth, 4, bl, p, 128), bf16)],
        compiler_params=_cparams(("arbitrary", "arbitrary")),
        name="cache_kv",
    )(cache_ckv, cache_kpe_pad, wka, wv)


ATT_RB = 64


def _attn_scratch(nchains, tq, kc):
    return [pltpu.VMEM((nchains, tq, 1), f32), pltpu.VMEM((nchains, tq, 1), f32),
            pltpu.VMEM((nchains, tq, LANES), f32), pltpu.VMEM((nchains, tq, kc), f32),
            pltpu.VMEM((nchains, tq, kc), bf16)]


def _attn_scan(q_refs, k_ref, v_ref, cache_refs, scratch, kc):
    m_sc, al_sc, acc_sc, s_sc, p_sc = scratch
    nh = k_ref.shape[0]
    lk = k_ref.shape[2]
    m_sc[...] = jnp.full(m_sc.shape, -jnp.inf, f32)
    acc_sc[...] = jnp.zeros(acc_sc.shape, f32)

    tq = m_sc.shape[1]
    chains = [(j * nh + h, q_ref, h) for j, q_ref in enumerate(q_refs) for h in range(nh)]

    def step(get_k, get_v, kw):
        for c, q_ref, h in chains:
            s_sc[c, :, 0:kw] = lax.dot_general(q_ref[h, 0], get_k(h), (((1,), (1,)), ((), ())),
                                               preferred_element_type=f32)
        for c, _, _ in chains:
            for rb in range(tq // ATT_RB):
                rows = slice(rb * ATT_RB, (rb + 1) * ATT_RB)
                s = s_sc[c, rows, 0:kw]
                m_old = m_sc[c, rows, :]
                m_new = jnp.maximum(m_old, jnp.max(s, axis=1, keepdims=True))
                p_sc[c, rows, 0:kw] = jnp.exp2(s - m_new).astype(bf16)
                al_sc[c, rows, :] = jnp.exp2(m_old - m_new)
                m_sc[c, rows, :] = m_new
        for c, _, h in chains:
            acc_sc[c] = al_sc[c] * acc_sc[c] + jnp.dot(p_sc[c, :, 0:kw], get_v(h), preferred_element_type=f32)

    def body(i, carry):
        rows = pl.ds(pl.multiple_of(i * kc, kc), kc)
        step(lambda h: k_ref[h, 0, rows, :], lambda h: v_ref[h, 0, rows, :], kc)
        return carry

    lax.fori_loop(0, lk // kc, body, 0)
    if cache_refs is not None:
        kc_ref, vc_ref = cache_refs
        step(lambda h: kc_ref[h, 0], lambda h: vc_ref[h, 0], kc_ref.shape[2])


def _attn_out(acc):
    return acc[:, 0:V_ONE] / acc[:, V_ONE:V_ONE + 1]


def _mla_attn_kernel(*refs, kc, has_cache):
    if has_cache:
        q_ref, k_ref, v_ref, kc_ref, vc_ref, o_ref, *scratch = refs
        cache_refs = (kc_ref, vc_ref)
    else:
        q_ref, k_ref, v_ref, o_ref, *scratch = refs
        cache_refs = None
    _attn_scan((q_ref,), k_ref, v_ref, cache_refs, scratch, kc)
    acc_sc = scratch[2]
    for h in range(MLA_H):
        o_ref[0, :, MLA_V * h:MLA_V * (h + 1)] = _attn_out(acc_sc[h])


def _mla_attention(q, k, v, seq_off, nseq, L, tq, kc, cache=None):
    H = q.shape[0]
    kv = lambda rows, off: pl.BlockSpec((H, 1, rows, 128), lambda b, i: (0, b + off, 0, 0))
    in_specs = [pl.BlockSpec((H, 1, tq, 128), lambda b, i: (0, b + seq_off, i, 0)), kv(L, seq_off), kv(L, seq_off)]
    args = [q, k, v]
    if cache is not None:
        in_specs += [kv(cache[0].shape[2], 0), kv(cache[0].shape[2], 0)]
        args += list(cache)
    return pl.pallas_call(
        functools.partial(_mla_attn_kernel, kc=kc, has_cache=cache is not None),
        grid=(nseq, L // tq),
        in_specs=in_specs,
        out_specs=pl.BlockSpec((1, tq, H * MLA_V), lambda b, i: (b, i, 0)),
        out_shape=jax.ShapeDtypeStruct((nseq, L, H * MLA_V), f32),
        scratch_shapes=_attn_scratch(H, tq, kc),
        compiler_params=_cparams(("arbitrary", "arbitrary")),
        name="mla_attn",
    )(*args)


def _diff_attn_kernel(*refs, kc, has_cache):
    if has_cache:
        q1_ref, q2_ref, k_ref, v_ref, kc_ref, vc_ref, lam_ref, nw_ref, o_ref, *scratch = refs
        cache_refs = (kc_ref, vc_ref)
    else:
        q1_ref, q2_ref, k_ref, v_ref, lam_ref, nw_ref, o_ref, *scratch = refs
        cache_refs = None
    acc_sc = scratch[2]
    lamv = lam_ref[...]
    lam_init = lamv[4:5, 0:1]
    lam = (jnp.exp(jnp.sum(lamv[0:1] * lamv[1:2], axis=1, keepdims=True))
           - jnp.exp(jnp.sum(lamv[2:3] * lamv[3:4], axis=1, keepdims=True)) + lam_init)
    _attn_scan((q1_ref, q2_ref), k_ref, v_ref, cache_refs, scratch, kc)
    for h in range(DA_H):
        o = _attn_out(acc_sc[h]) - lam * _attn_out(acc_sc[DA_H + h])
        o_ref[0, :, DA_DV * h:DA_DV * (h + 1)] = _rms(o, nw_ref[...]) * (1.0 - lam_init)


def _diff_attention(q1, q2, k, v, lamv, nw, seq_off, nseq, L, tq, kc, cache=None):
    H = q1.shape[0]
    hm = lambda rows, w, off: pl.BlockSpec((H, 1, rows, w), lambda b, i: (0, b + off, 0, 0))
    qs = pl.BlockSpec((H, 1, tq, 64), lambda b, i: (0, b + seq_off, i, 0))
    in_specs = [qs, qs, hm(L, 64, seq_off), hm(L, 128, seq_off)]
    args = [q1, q2, k, v]
    if cache is not None:
        p = cache[0].shape[2]
        in_specs += [hm(p, 64, 0), hm(p, 128, 0)]
        args += list(cache)
    in_specs += [pl.BlockSpec((8, DA_DK), lambda b, i: (0, 0)), pl.BlockSpec((1, DA_DV), lambda b, i: (0, 0))]
    args += [lamv, nw.reshape(1, DA_DV)]
    return pl.pallas_call(
        functools.partial(_diff_attn_kernel, kc=kc, has_cache=cache is not None),
        grid=(nseq, L // tq),
        in_specs=in_specs,
        out_specs=pl.BlockSpec((1, tq, H * DA_DV), lambda b, i: (b, i, 0)),
        out_shape=jax.ShapeDtypeStruct((nseq, L, H * DA_DV), f32),
        scratch_shapes=_attn_scratch(2 * H, tq, kc),
        compiler_params=_cparams(("arbitrary", "arbitrary")),
        name="diff_attn",
    )(*args)


GDN_N = GDN_H * GDN_CHUNK
GDN_INV_BASE = 8
(M_BD, M_EYE, M_BASE, M_OFF8, M_OFF16, M_OFF32, M_DIR) = range(7)
GDN_NMASK = M_DIR + 6


def _gdn_fill_masks(msk):
    n, c = GDN_N, GDN_CHUNK
    rr = lax.broadcasted_iota(jnp.int32, (n, n), 0)
    cc = lax.broadcasted_iota(jnp.int32, (n, n), 1)
    blk = lambda x, s: lax.shift_right_logical(x, int(math.log2(s)))
    bd = blk(rr, c) == blk(cc, c)
    ri = rr & (c - 1)
    cj = cc & (c - 1)

    def put(i, cond):
        msk[i] = jnp.where(cond, 1.0, 0.0)

    put(M_BD, bd)
    put(M_EYE, rr == cc)
    put(M_BASE, blk(rr, GDN_INV_BASE) == blk(cc, GDN_INV_BASE))
    for slot, s in ((M_OFF8, 8), (M_OFF16, 16), (M_OFF32, 32)):
        put(slot, (blk(rr, 2 * s) == blk(cc, 2 * s)) & (blk(rr, s) != blk(cc, s)))
    put(M_DIR + 0, bd & (ri >= cj))
    put(M_DIR + 1, bd & (ri > cj))
    put(M_DIR + 2, bd & (ri <= cj))
    put(M_DIR + 3, bd & (ri <= cj))
    put(M_DIR + 4, bd & (ri < cj))
    put(M_DIR + 5, bd & (ri >= cj))


def _gdn_conv_chunk(x_ref, s, w, r, L):
    c = GDN_CHUNK
    x = x_ref[s, pl.ds(r, c), :]
    prev8 = x_ref[s, pl.ds(pl.multiple_of(jnp.maximum(r - 8, 0), 8), 8), :]
    next8 = x_ref[s, pl.ds(pl.multiple_of(jnp.minimum(r + c, L - 8), 8), 8), :]
    prev = jnp.where(r > 0, prev8[7:8, :], 0.0)
    nxt = jnp.where(r + c < L, next8[0:1, :], 0.0)
    row = lax.broadcasted_iota(jnp.int32, x.shape, 0)
    xp = jnp.where(row == 0, prev, pltpu.roll(x, 1, 0))
    xn = jnp.where(row == c - 1, nxt, pltpu.roll(x, c - 1, 0))
    y = xp * w[0:1] + x * w[1:2] + xn * w[2:3]
    return _silu(y)


def _group_sum(y, bdb):
    hi, lo = _split(y)
    return jnp.dot(hi, bdb, preferred_element_type=f32) + jnp.dot(lo, bdb, preferred_element_type=f32)


(B_K, B_Q, B_KB, B_RU, B_E, B_TRI, B_T, B_P, B_X, B_U, B_W, B_IN, B_VN) = range(13)
GDN_NBUF = 13


def _gdn_chunk_chains(xs, gabs, alog, dtb, S, pools, msk):
    n, c = GDN_N, GDN_CHUNK
    bdb = msk[M_BD].astype(bf16)
    small = tuple({} for _ in xs)

    def stage(fn):
        for ch in range(len(xs)):
            fn(ch, pools[ch], small[ch])

    def prep(ch, B, sm):
        d = ch % 2
        xq, xk, xv = xs[ch]
        bd = msk[M_BD]
        q = xq * lax.rsqrt(_group_sum(xq * xq, bdb) + EPS) * (GDN_DK ** -0.5)
        k = xk * lax.rsqrt(_group_sum(xk * xk, bdb) + EPS)
        gab = gabs[ch]
        lane = lax.broadcasted_iota(jnp.int32, gab.shape, 1)
        gfull = -jnp.exp(alog) * _softplus(gab + dtb)
        bfull = _sigmoid(gab)

        def stack(arr, base):
            return jnp.concatenate([jnp.sum(jnp.where(lane == base + h, arr, 0.0), axis=1, keepdims=True)
                                    for h in range(GDN_H)], axis=0)

        g_stack = stack(gfull, d * GDN_H)
        b_stack = stack(bfull, (2 + d) * GDN_H)
        g_b = jnp.broadcast_to(g_stack, (n, n))
        g_row = jnp.sum(g_b * msk[M_EYE], axis=0, keepdims=True)
        gc_row = jnp.sum(g_b * msk[M_DIR + 3 * d + 2], axis=0, keepdims=True)
        g_rb = jnp.broadcast_to(g_row, (n, n))
        gc_col = jnp.sum(g_rb * msk[M_DIR + 3 * d], axis=1, keepdims=True)
        g_last = jnp.sum(g_rb * bd, axis=1, keepdims=True)
        B[B_E] = jnp.exp(jnp.minimum(gc_col - gc_row, 0.0))
        tile4 = lambda x: jnp.concatenate([x] * GDN_H, axis=0)
        k_bd = tile4(k) * bd
        B[B_K] = k_bd
        B[B_KB] = k_bd * b_stack
        B[B_Q] = tile4(q) * bd
        B[B_RU] = tile4(xv) * (bd * b_stack)
        sm.update(gc_col=gc_col, egc=jnp.exp(gc_col), g_last=g_last)

    def tri(ch, B, sm):
        t = _dot_nt(B[B_KB], B[B_K]) * (B[B_E] * msk[M_DIR + 3 * (ch % 2) + 1])
        B[B_TRI] = t
        nm = -(t * msk[M_BASE])
        B[B_P] = nm
        B[B_T] = msk[M_EYE] + nm

    def intra(ch, B, sm):
        B[B_IN] = _dot_nt(B[B_Q], B[B_K]) * (B[B_E] * msk[M_DIR + 3 * (ch % 2)])

    def square(d, B, sm):
        B[B_P] = _dot(B[B_P], B[B_P])

    def extend(d, B, sm):
        B[B_T] = B[B_T] + _dot(B[B_T], B[B_P])

    stage(prep)
    stage(tri)
    stage(intra)
    for _ in range(2):
        stage(square)
        stage(extend)
    for slot in (M_OFF8, M_OFF16, M_OFF32):
        def cross(d, B, sm, slot=slot):
            B[B_X] = _dot(B[B_T], B[B_TRI] * msk[slot])

        def merge(d, B, sm):
            B[B_T] = B[B_T] - _dot(B[B_X], B[B_T])

        stage(cross)
        stage(merge)

    def solve_u(d, B, sm):
        B[B_U] = _dot(B[B_T], B[B_RU])

    def solve_w(d, B, sm):
        B[B_W] = _dot(B[B_T], B[B_KB] * sm["egc"])

    def v_new(d, B, sm):
        B[B_VN] = B[B_U] - _dot(B[B_W], S[d])

    def out(d, B, sm):
        o_bd = _dot(B[B_Q] * sm["egc"], S[d]) + _dot(B[B_IN], B[B_VN])
        sm["o"] = o_bd[0:c] + o_bd[c:2 * c] + o_bd[2 * c:3 * c] + o_bd[3 * c:4 * c]

    def update(d, B, sm):
        S[d] = (S[d] * jnp.exp(sm["g_last"])
                + _dot_tn(B[B_K] * jnp.exp(sm["g_last"] - sm["gc_col"]), B[B_VN]))

    for fn in (solve_u, solve_w, v_new, out, update):
        stage(fn)
    return [sm["o"] for sm in small]


def _gdn_kernel(*refs, has_s0):
    if has_s0:
        (q_ref, k_ref, v_ref, z_ref, cwq_ref, cwk_ref, cwv_ref, gab_ref, alog_ref, dtb_ref, nw_ref, s0_ref,
         o_ref, of, ob, msk, S, *pools) = refs
        sfin_ref = None
    else:
        (q_ref, k_ref, v_ref, z_ref, cwq_ref, cwk_ref, cwv_ref, gab_ref, alog_ref, dtb_ref, nw_ref,
         o_ref, sfin_ref, of, ob, msk, S, *pools) = refs
    c = GDN_CHUNK
    ns, L = q_ref.shape[0], q_ref.shape[1]
    n = L // c

    @pl.when(pl.program_id(0) == 0)
    def _():
        _gdn_fill_masks(msk)

    alog = alog_ref[...]
    dtb = dtb_ref[...]
    cwq, cwk, cwv = cwq_ref[...], cwk_ref[...], cwv_ref[...]
    if has_s0:
        for s in range(ns):
            S[2 * s:2 * s + 2] = s0_ref[s, 0]
    else:
        S[...] = jnp.zeros(S.shape, f32)

    def body(i, carry):
        starts = (pl.multiple_of(i * c, c), pl.multiple_of((n - 1 - i) * c, c))
        chains = [(s, r) for s in range(ns) for r in starts]
        xs = [(_gdn_conv_chunk(q_ref, s, cwq, r, L), _gdn_conv_chunk(k_ref, s, cwk, r, L),
               _gdn_conv_chunk(v_ref, s, cwv, r, L)) for s, r in chains]
        gabs = [gab_ref[s, pl.ds(r, c), :] for s, r in chains]
        outs = _gdn_chunk_chains(xs, gabs, alog, dtb, S, pools, msk)
        for ch, (s, r) in enumerate(chains):
            (of, ob)[ch % 2][s, pl.ds(r, c), :] = outs[ch]
        return carry

    lax.fori_loop(0, n, body, 0)
    if sfin_ref is not None:
        for s in range(ns):
            for direction in range(2):
                for h in range(GDN_H):
                    sfin_ref[s, direction, h] = S[2 * s + direction, c * h:c * (h + 1), c * h:c * (h + 1)]

    bdb = msk[M_BD].astype(bf16)
    nw = nw_ref[...]

    def norm_gate(i, carry):
        rows = pl.ds(pl.multiple_of(i * c, c), c)
        for s in range(ns):
            o = of[s, rows, :] + ob[s, rows, :]
            ms = _group_sum(o * o, bdb) * (1.0 / GDN_DV)
            o_ref[s, rows, :] = o * lax.rsqrt(ms + EPS) * nw * _silu(z_ref[s, rows, :])
        return carry

    lax.fori_loop(0, n, norm_gate, 0)


def _gdn(zg, zab, cw, alog_row, dtb_row, nw, seq_off, nseq, L, ns, s0=None, layer=0):
    H = GDN_H
    w = H * GDN_DK
    assert seq_off % ns == 0 and nseq % ns == 0
    off = seq_off // ns
    one = pl.Buffered(1)
    tok = lambda k: pl.BlockSpec((ns, L, w), lambda s, k=k: (s + off, 0, k), pipeline_mode=one)
    cws = lambda k: pl.BlockSpec((3, w), lambda s, k=k: (0, k))
    full = lambda shape: pl.BlockSpec(shape, lambda s: (0,) * len(shape))
    in_specs = [tok(0), tok(1), tok(2), tok(3), cws(0), cws(1), cws(2),
                pl.BlockSpec((ns, L, 128), lambda s: (s + off, 0, 0)),
                full((1, 128)), full((1, 128)), full((1, w))]
    args = [zg, zg, zg, zg, cw, cw, cw, zab, alog_row, dtb_row, jnp.tile(nw.reshape(1, GDN_DV), (1, H))]
    out_specs = [pl.BlockSpec((ns, L, w), lambda s: (s, 0, 0))]
    out_shape = [jax.ShapeDtypeStruct((nseq, L, w), f32)]
    if s0 is not None:
        in_specs.append(pl.BlockSpec((ns, 1, 2, GDN_N, GDN_N), lambda s: (s, layer, 0, 0, 0)))
        args.append(s0)
    else:
        out_specs.append(pl.BlockSpec((ns, 2, H, GDN_DK, GDN_DV), lambda s: (s, 0, 0, 0, 0)))
        out_shape.append(jax.ShapeDtypeStruct((nseq, 2, H, GDN_DK, GDN_DV), f32))
    pool = pltpu.VMEM((GDN_NBUF, GDN_N, GDN_N), f32)
    return pl.pallas_call(
        functools.partial(_gdn_kernel, has_s0=s0 is not None),
        grid=(nseq // ns,),
        in_specs=in_specs,
        out_specs=out_specs,
        out_shape=out_shape,
        scratch_shapes=[pltpu.VMEM((ns, L, w), f32), pltpu.VMEM((ns, L, w), f32),
                        pltpu.VMEM((GDN_NMASK, GDN_N, GDN_N), f32), pltpu.VMEM((2 * ns, GDN_N, GDN_N), f32)]
                       + [pool] * (2 * ns),
        compiler_params=_cparams(("arbitrary",)),
        name="gdn",
    )(*args)


def _bitrev(p, bits):
    r = 0
    for _ in range(bits):
        r = (r << 1) | (p & 1)
        p >>= 1
    return r


@functools.lru_cache(maxsize=None)
def _fft_tables(L):
    n = 2 * L
    n2 = DFT_N2
    n1 = n // n2
    bits = n1.bit_length() - 1
    npair = max(n1 // 2, 1)
    sta = np.zeros((max(bits, 1) * npair, 2 * n2), np.float64)
    stb = np.zeros_like(sta)
    for s in range(bits):
        half = n1 >> (s + 1)
        for p in range(npair):
            j = p % half
            ang = -2.0 * np.pi * j / (2 * half)
            wr, wi = np.cos(ang), np.sin(ang)
            sta[s * npair + p, :] = wr
            stb[s * npair + p, :n2] = -wi
            stb[s * npair + p, n2:] = wi
    twa = np.zeros((n1, 2 * n2), np.float64)
    twb = np.zeros_like(twa)
    lanes = np.arange(n2)
    for p in range(n1):
        ang = -2.0 * np.pi * lanes * _bitrev(p, bits) / n
        twa[p, :n2] = np.cos(ang)
        twa[p, n2:] = np.cos(ang)
        twb[p, :n2] = -np.sin(ang)
        twb[p, n2:] = np.sin(ang)
    kn = np.outer(lanes, lanes) * (-2.0 * np.pi / n2)
    fr, fi = np.cos(kn), np.sin(kn)
    fwd = np.block([[fr, fi], [-fi, fr]])
    inv = np.block([[fr, -fi], [fi, fr]])
    as32 = lambda a: np.asarray(a, np.float32)
    return dict(n1=n1, bits=bits, npair=npair, sta=as32(sta), stb=as32(stb), twa=as32(twa), twb=as32(twb),
                fwd=as32(fwd), inv=as32(inv))


@functools.lru_cache(maxsize=None)
def _hyena_pos_table(L):
    n = 2 * L
    idx = np.arange(n)
    pos = np.where(idx < L, idx, n - idx).astype(np.float64)
    pos[L] = 0.0
    t = pos / max(L - 1, 1)
    bands = np.linspace(1e-4, HY_BANDS - 1, HY_BANDS).astype(np.float32).astype(np.float64)
    ang = (2.0 * math.pi * pos / L)[None, :] * bands[:, None]
    z = np.zeros((LANES, n), np.float64)
    z[0] = t
    z[1:1 + HY_BANDS] = np.cos(ang)
    z[1 + HY_BANDS:1 + 2 * HY_BANDS] = -np.sin(ang)
    deltas = np.abs(np.linspace(HY_SLOW_DECAY, HY_FAST_DECAY, HY_W)).reshape(HY_W, 1)
    return np.asarray(z, np.float32), np.asarray(deltas, np.float32)


def _swap_halves(x):
    n2 = x.shape[1] // 2
    return jnp.concatenate([x[:, n2:], x[:, :n2]], axis=1)


def _fft_forward(X, sta_ref, stb_ref, twa_ref, twb_ref, fh_ref, fl_ref, n1, bits, npair, ct, mrows):
    for s in range(bits):
        half = n1 >> (s + 1)

        def pair(p, carry, s=s, half=half):
            grp = p // half
            j = p - grp * half
            a = grp * 2 * half + j
            ra = pl.ds(pl.multiple_of(a * ct, ct), ct)
            rb = pl.ds(pl.multiple_of((a + half) * ct, ct), ct)
            xa = X[ra, :]
            xb = X[rb, :]
            X[ra, :] = xa + xb
            d = xa - xb
            X[rb, :] = d * sta_ref[pl.ds(s * npair + p, 1), :] + _swap_halves(d) * stb_ref[pl.ds(s * npair + p, 1), :]
            return carry

        lax.fori_loop(0, npair, pair, 0)

    def blk(p, carry):
        r = pl.ds(pl.multiple_of(p * ct, ct), ct)
        y = X[r, :]
        X[r, :] = y * twa_ref[pl.ds(p, 1), :] + _swap_halves(y) * twb_ref[pl.ds(p, 1), :]
        return carry

    lax.fori_loop(0, n1, blk, 0)

    def mm(i, carry):
        r = pl.ds(pl.multiple_of(i * mrows, mrows), mrows)
        X[r, :] = _dot3_w(X[r, :], fh_ref[...], fl_ref[...])
        return carry

    lax.fori_loop(0, n1 * ct // mrows, mm, 0)


def _fft_inverse(X, sta_ref, stb_ref, twa_ref, twb_ref, fh_ref, fl_ref, n1, bits, npair, ct, mrows):
    def mm(i, carry):
        r = pl.ds(pl.multiple_of(i * mrows, mrows), mrows)
        X[r, :] = _dot3_w(X[r, :], fh_ref[...], fl_ref[...])
        return carry

    lax.fori_loop(0, n1 * ct // mrows, mm, 0)

    def blk(p, carry):
        r = pl.ds(pl.multiple_of(p * ct, ct), ct)
        y = X[r, :]
        X[r, :] = y * twa_ref[pl.ds(p, 1), :] - _swap_halves(y) * twb_ref[pl.ds(p, 1), :]
        return carry

    lax.fori_loop(0, n1, blk, 0)

    for s in reversed(range(bits)):
        half = n1 >> (s + 1)

        def pair(p, carry, s=s, half=half):
            grp = p // half
            j = p - grp * half
            a = grp * 2 * half + j
            ra = pl.ds(pl.multiple_of(a * ct, ct), ct)
            rb = pl.ds(pl.multiple_of((a + half) * ct, ct), ct)
            xa = X[ra, :]
            xb = X[rb, :]
            tw = xb * sta_ref[pl.ds(s * npair + p, 1), :] - _swap_halves(xb) * stb_ref[pl.ds(s * npair + p, 1), :]
            X[ra, :] = xa + tw
            X[rb, :] = xa - tw
            return carry

        lax.fori_loop(0, npair, pair, 0)


def _conv3_lanes(x, w, b):
    L = x.shape[1]
    lane = lax.broadcasted_iota(jnp.int32, x.shape, 1)
    xp = jnp.where(lane == 0, 0.0, pltpu.roll(x, 1, 1))
    xn = jnp.where(lane == L - 1, 0.0, pltpu.roll(x, L - 1, 1))
    return xp * w[:, 0:1] + x * w[:, 1:2] + xn * w[:, 2:3] + b


def _hyena_hidden_kernel(zt_ref, w1_ref, b1_ref, w2_ref, b2_ref, fr_ref, h_ref):
    fr = fr_ref[...]
    h = jnp.sin(fr * (_dot3(w1_ref[...], zt_ref[...]) + b1_ref[...]))
    h_ref[...] = jnp.sin(fr * (_dot3(w2_ref[...], h) + b2_ref[...]))


def _hyena_hidden(zt, hw):
    nblk = max(zt.shape[1] // 2048, 1)
    tn = zt.shape[1] // nblk
    full = lambda a: pl.BlockSpec(a.shape, lambda j: (0,) * a.ndim)
    names = ("w1T", "b1", "w2T", "b2", "freq")
    return pl.pallas_call(
        _hyena_hidden_kernel,
        grid=(nblk,),
        in_specs=[pl.BlockSpec((zt.shape[0], tn), lambda j: (0, j))] + [full(hw[k]) for k in names],
        out_specs=pl.BlockSpec((HY_FH, tn), lambda j: (0, j)),
        out_shape=jax.ShapeDtypeStruct((HY_FH, zt.shape[1]), f32),
        compiler_params=_cparams(("arbitrary",)),
        name="hyena_hidden",
    )(zt, *[hw[k] for k in names])


def _hyena_kernel(x0a_ref, x1a_ref, va_ref, x0b_ref, x1b_ref, vb_ref, cw0_ref, cw1_ref, cw2_ref,
                  cb0_ref, cb1_ref, cb2_ref, d_ref, t_ref, dl_ref, h_ref,
                  w3f_ref, w3b_ref, sta_ref, stb_ref, twa_ref, twb_ref,
                  ffh_ref, ffl_ref, fih_ref, fil_ref, o_ref, X, HA, HB, *, L, n1, bits, npair, ct, mrows):
    n2 = DFT_N2
    n = 2 * L
    fft_args = (sta_ref, stb_ref, twa_ref, twb_ref)

    @pl.when(pl.program_id(1) == 0)
    def _():
        h = h_ref[...]
        hf = _dot3(w3f_ref[...], h)
        hb = _dot3(w3b_ref[...], h)
        lane = lax.broadcasted_iota(jnp.int32, hf.shape, 1)
        dec = jnp.exp(-t_ref[0:1, :] * dl_ref[...])
        hc = jnp.where(lane < L, hf, jnp.where(lane > L, hb, 0.0)) * dec
        for b in range(n1):
            X[b * ct:(b + 1) * ct, 0:n2] = hc[:, b * n2:(b + 1) * n2]
            X[b * ct:(b + 1) * ct, n2:2 * n2] = jnp.zeros((ct, n2), f32)
        _fft_forward(X, *fft_args, ffh_ref, ffl_ref, n1, bits, npair, ct, mrows)
        hs = X[...]
        HA[...] = jnp.concatenate([hs[:, :n2], hs[:, :n2]], axis=1)
        HB[...] = jnp.concatenate([-hs[:, n2:], hs[:, n2:]], axis=1)

    x0a = _conv3_lanes(x0a_ref[0], cw0_ref[...], cb0_ref[...])
    x0b = _conv3_lanes(x0b_ref[0], cw0_ref[...], cb0_ref[...])
    vva = _conv3_lanes(va_ref[0], cw2_ref[...], cb2_ref[...]) * _conv3_lanes(x1a_ref[0], cw1_ref[...], cb1_ref[...])
    vvb = _conv3_lanes(vb_ref[0], cw2_ref[...], cb2_ref[...]) * _conv3_lanes(x1b_ref[0], cw1_ref[...], cb1_ref[...])
    nb = L // n2
    for b in range(nb):
        X[b * ct:(b + 1) * ct, 0:n2] = vva[:, b * n2:(b + 1) * n2]
        X[b * ct:(b + 1) * ct, n2:2 * n2] = vvb[:, b * n2:(b + 1) * n2]
    X[nb * ct:n1 * ct, :] = jnp.zeros(((n1 - nb) * ct, 2 * n2), f32)
    _fft_forward(X, *fft_args, ffh_ref, ffl_ref, n1, bits, npair, ct, mrows)

    def spec(i, carry):
        r = pl.ds(pl.multiple_of(i * ct, ct), ct)
        x = X[r, :]
        X[r, :] = x * HA[r, :] + _swap_halves(x) * HB[r, :]
        return carry

    lax.fori_loop(0, n1, spec, 0)
    _fft_inverse(X, *fft_args, fih_ref, fil_ref, n1, bits, npair, ct, mrows)
    inv_n = 1.0 / n
    ya = jnp.concatenate([X[b * ct:(b + 1) * ct, 0:n2] for b in range(nb)], axis=1) * inv_n
    yb = jnp.concatenate([X[b * ct:(b + 1) * ct, n2:2 * n2] for b in range(nb)], axis=1) * inv_n
    dcol = d_ref[...]
    o_ref[0] = (ya + vva * dcol) * x0a
    o_ref[1] = (yb + vvb * dcol) * x0b


def _hyena(zhT, grp_a, grp_b, lane_a, lane_b, npairs, L, ct, hw):
    tabs = _fft_tables(L)
    n1, bits, npair = tabs["n1"], tabs["bits"], tabs["npair"]
    n = 2 * L
    ntile = HY_W // ct
    mrows = min(512, n1 * ct)
    zt, deltas = _hyena_pos_table(L)
    fwd = jnp.asarray(tabs["fwd"])
    inv = jnp.asarray(tabs["inv"])
    ffh = fwd.astype(bf16)
    ffl = (fwd - ffh.astype(f32)).astype(bf16)
    fih = inv.astype(bf16)
    fil = (inv - fih.astype(f32)).astype(bf16)
    xin = lambda k, grp, ln: pl.BlockSpec((1, ct, L), lambda j, p, k=k: (grp(p), j + ntile * k, ln(p)))
    chan = lambda k, w: pl.BlockSpec((ct, w), lambda j, p, k=k: (j + ntile * k, 0))
    full = lambda a: pl.BlockSpec(a.shape, lambda j, p: (0,) * a.ndim)
    hidden = _hyena_hidden(jnp.asarray(zt), hw)
    consts = [jnp.asarray(zt[0:8]), jnp.asarray(deltas)]
    in_specs = ([xin(0, grp_a, lane_a), xin(1, grp_a, lane_a), xin(2, grp_a, lane_a),
                 xin(0, grp_b, lane_b), xin(1, grp_b, lane_b), xin(2, grp_b, lane_b),
                 chan(0, 3), chan(1, 3), chan(2, 3), chan(0, 1), chan(1, 1), chan(2, 1), chan(0, 1),
                 full(consts[0]), chan(0, 1), full(hidden),
                 chan(0, HY_FH), chan(1, HY_FH)]
                + [full(jnp.asarray(tabs[k])) for k in ("sta", "stb", "twa", "twb")]
                + [full(ffh), full(ffl), full(fih), full(fil)])
    args = ([zhT] * 6 + [hw["cwT"]] * 3 + [hw["cb"]] * 3 + [hw["d"], consts[0], consts[1], hidden,
            hw["w3T"], hw["w3T"]]
            + [jnp.asarray(tabs[k]) for k in ("sta", "stb", "twa", "twb")] + [ffh, ffl, fih, fil])
    return pl.pallas_call(
        functools.partial(_hyena_kernel, L=L, n1=n1, bits=bits, npair=npair, ct=ct, mrows=mrows),
        grid=(ntile, npairs),
        in_specs=in_specs,
        out_specs=pl.BlockSpec((2, ct, L), lambda j, p: (p, j, 0)),
        out_shape=jax.ShapeDtypeStruct((2 * npairs, HY_W, L), f32),
        scratch_shapes=[pltpu.VMEM((n1 * ct, 2 * DFT_N2), f32)] * 3,
        compiler_params=_cparams(("arbitrary", "arbitrary")),
        name="hyena",
    )(*args)


def _merge_kernel(x_ref, mod_ref, nw_ref, oa_c, oa_l, ob_c, ob_l, oc_c, oc_l, od_c, od_l, wg_ref, wb_ref, wo_ref,
                  xo_ref):
    g = pl.program_id(0)
    d = D_MODEL
    x = x_ref[0]
    mod = mod_ref[pl.ds(g, 1), :]
    h = _norm_mod(x, nw_ref[...], mod[:, d:2 * d], mod[:, 0:d]).astype(bf16)
    ctx = g == 0
    pick = lambda c_ref, l_ref: jnp.where(ctx, c_ref[0], l_ref[0]).astype(bf16)
    oc_ctx = jnp.concatenate([oc_c[s] for s in range(oc_c.shape[0])], axis=1)
    oc = jnp.where(ctx, oc_ctx, oc_l[0]).astype(bf16)
    pa = jnp.dot(pick(oa_c, oa_l), wb_ref[0], preferred_element_type=f32)
    pb = jnp.dot(pick(ob_c, ob_l), wb_ref[1], preferred_element_type=f32)
    pc = lax.dot_general(oc, wb_ref[2], (((0,), (0,)), ((), ())), preferred_element_type=f32)
    pd = jnp.dot(pick(od_c, od_l), wb_ref[3], preferred_element_type=f32)
    acc = jnp.zeros_like(x)
    for nbr, proj in enumerate((pa, pb, pc, pd)):
        gate = _sigmoid(jnp.dot(h, wg_ref[:, nbr * d:(nbr + 1) * d], preferred_element_type=f32))
        acc = acc + gate * proj
    xo_ref[0] = x + mod[:, 2 * d:3 * d] * _dot(acc, wo_ref[...])


def _merge(x, mod, nw, branches, wg, wb, wo, layer, tm):
    G, T, d = x.shape
    (oa_c, oa_l), (ob_c, ob_l), (oc_c, oc_l), (od_c, od_l) = branches
    lc = oc_c.shape[2]
    tok = lambda w: pl.BlockSpec((1, tm, w), lambda g, i: (g, i, 0))
    ctx_i = lambda g, i: jnp.where(g == 0, i, 0)
    lat_g = lambda g: jnp.maximum(g - 1, 0)
    lat_i = lambda g, i: jnp.where(g == 0, 0, i)
    tok_c = pl.BlockSpec((1, tm, 256), lambda g, i: (0, ctx_i(g, i), 0))
    tok_l = pl.BlockSpec((1, tm, 256), lambda g, i: (lat_g(g), lat_i(g, i), 0))
    lay = lambda a: _layer_spec(a, layer)
    return pl.pallas_call(
        _merge_kernel,
        grid=(G, T // tm),
        in_specs=[tok(d), lay(mod), pl.BlockSpec((1, d), lambda g, i: (0, 0)),
                  tok_c, tok_l, tok_c, tok_l,
                  pl.BlockSpec((tm // lc, HY_W, lc), lambda g, i: (ctx_i(g, i), 0, 0)),
                  pl.BlockSpec((1, HY_W, tm), lambda g, i: (lat_g(g), 0, lat_i(g, i))),
                  tok_c, tok_l, lay(wg), lay(wb), lay(wo)],
        out_specs=tok(d),
        out_shape=jax.ShapeDtypeStruct((G, T, d), f32),
        compiler_params=_cparams(("arbitrary", "arbitrary")),
        name="merge",
    )(x, mod, nw.reshape(1, d), oa_c.reshape(1, T, -1), oa_l, ob_c.reshape(1, T, -1), ob_l, oc_c, oc_l,
      od_c.reshape(1, T, -1), od_l, wg, wb, wo)


def _mlp_kernel(x_ref, mod_ref, nw_ref, w1_ref, w2_ref, fnw_ref, xo_ref, *, final):
    g = pl.program_id(0)
    d = D_MODEL
    x = x_ref[0]
    mod = mod_ref[pl.ds(g, 1), :]
    h = _norm_mod(x, nw_ref[...], mod[:, 4 * d:5 * d], mod[:, 3 * d:4 * d]).astype(bf16)
    acc = jnp.zeros_like(x)
    for c in range(D_FF // d):
        a = jnp.maximum(jnp.dot(h, w1_ref[:, c * d:(c + 1) * d], preferred_element_type=f32), 0.0)
        acc = acc + _dot(a * a, w2_ref[c * d:(c + 1) * d, :])
    y = x + mod[:, 5 * d:6 * d] * acc
    xo_ref[0] = _rms(y, fnw_ref[...]) if final else y


def _mlp(x, mod, nw, w1, w2, fnw, layer, tm, final):
    G, T, d = x.shape
    tok = pl.BlockSpec((1, tm, d), lambda g, i: (g, i, 0))
    vec = pl.BlockSpec((1, d), lambda g, i: (0, 0))
    return pl.pallas_call(
        functools.partial(_mlp_kernel, final=final),
        grid=(G, T // tm),
        in_specs=[tok, _layer_spec(mod, layer), vec, _layer_spec(w1, layer), _layer_spec(w2, layer), vec],
        out_specs=tok,
        out_shape=jax.ShapeDtypeStruct((G, T, d), f32),
        compiler_params=_cparams(("arbitrary", "arbitrary")),
        name="mlp",
    )(x, mod, nw.reshape(1, d), w1, w2, fnw.reshape(1, d))


@functools.lru_cache(maxsize=None)
def _rope_tables(T):
    m = MLA_ROPE // 4
    inv = ROPE_BASE ** (-np.arange(m, dtype=np.float64) / m)
    rows = T // GRID_W
    row_pos = np.repeat(np.arange(rows), GRID_W)[:, None] * inv
    col_pos = np.tile(np.arange(GRID_W), rows)[:, None] * inv
    cos32 = np.concatenate([np.cos(row_pos), np.cos(row_pos), np.cos(col_pos), np.cos(col_pos)], axis=1)
    sin32 = np.concatenate([-np.sin(row_pos), np.sin(row_pos), -np.sin(col_pos), np.sin(col_pos)], axis=1)
    ca = np.zeros((2, T, 128))
    cb = np.zeros((2, T, 128))
    ca[:, :, 0:96] = 1.0
    ca[1, :, 64:96] = cos32
    cb[1, :, 64:96] = sin32
    cd = np.ones((2, T, 256))
    sd = np.zeros((2, T, 256))
    cd[1] = np.tile(cos32, (1, 8))
    sd[1] = np.tile(sin32, (1, 8))
    return tuple(np.asarray(a, np.float32) for a in (ca, cb, cd, sd))


def _swap_perm(width):
    base = np.concatenate([np.arange(8, 16), np.arange(0, 8), np.arange(24, 32), np.arange(16, 24)])
    return np.concatenate([base + 32 * s for s in range(width // 32)])


def _pack_weights(w_in, mla_w_uq, mla_w_ukv, gdn_conv_w, gdn_a_log, gdn_dt_bias, hy_conv_w, hy_conv_b,
                  hy_f_w1, hy_f_b1, hy_f_w2, hy_f_b2, hy_f_w3, hy_f_freq, hy_d):
    depth = w_in.shape[0]
    offs = [0] + [int(s) for s in np.cumsum(IN_SPLITS)]
    seg = lambda i: w_in[:, :, offs[i]:offs[i + 1]]
    zeros = lambda n: jnp.zeros((depth, D_MODEL, n), w_in.dtype)
    kpe = seg(2)
    kpe_sw = kpe[:, :, _swap_perm(MLA_ROPE)]
    wa = jnp.concatenate([seg(0), seg(1), zeros(64), kpe, zeros(32), zeros(64), kpe_sw, zeros(32)], axis=2)
    dq, dk, dv = seg(10), seg(11), seg(12)
    perm = _swap_perm(256)
    dv_slots = jnp.concatenate([dv.reshape(depth, D_MODEL, DA_H, DA_DV),
                                jnp.zeros((depth, D_MODEL, DA_H, LANES - DA_DV), dv.dtype)],
                               axis=3).reshape(depth, D_MODEL, DA_H * LANES)
    wd = jnp.concatenate([dq, dq[:, :, perm], dk, dk[:, :, perm], dv_slots], axis=2)
    wg = jnp.concatenate([seg(3), seg(4), seg(5), seg(6)], axis=2)
    wab = jnp.concatenate([seg(7), seg(8), zeros(128 - 4 * GDN_H)], axis=2)
    whT = jnp.swapaxes(seg(9), 1, 2)
    wgate = seg(13)
    uq = mla_w_uq.reshape(depth, MLA_Q_LORA, MLA_H, MLA_NOPE + MLA_ROPE)
    z32 = jnp.zeros((depth, MLA_Q_LORA, MLA_H, 32), uq.dtype)
    z64 = jnp.zeros((depth, MLA_Q_LORA, MLA_H, 64), uq.dtype)
    rope_sw = uq[..., MLA_NOPE:][..., _swap_perm(MLA_ROPE)]
    wqa = jnp.concatenate([uq, z32], axis=3).reshape(depth, MLA_Q_LORA, MLA_H * 128)
    wqb = jnp.concatenate([z64, rope_sw, z32], axis=3).reshape(depth, MLA_Q_LORA, MLA_H * 128)
    ukv = mla_w_ukv.reshape(depth, MLA_KV_LORA, MLA_H, MLA_NOPE + MLA_V)
    wka = jnp.concatenate([ukv[..., :MLA_NOPE], jnp.zeros((depth, MLA_KV_LORA, MLA_H, 64), ukv.dtype)],
                          axis=3).reshape(depth, MLA_KV_LORA, MLA_H * 128)
    wv = jnp.concatenate([ukv[..., MLA_NOPE:], jnp.zeros((depth, MLA_KV_LORA, MLA_H, LANES - MLA_V), ukv.dtype)],
                         axis=3).reshape(depth, MLA_KV_LORA, MLA_H * LANES)
    cast = lambda a: a.astype(bf16)
    pad128 = lambda a: jnp.pad(a.reshape(depth, 1, -1), ((0, 0), (0, 0), (0, 128 - a.shape[1] * a.shape[2])))
    hy = dict(
        cwT=jnp.swapaxes(hy_conv_w, 1, 2),
        cb=hy_conv_b.reshape(depth, -1, 1),
        d=hy_d.reshape(depth, HY_W, 1),
        w1T=jnp.pad(jnp.swapaxes(hy_f_w1, 1, 2), ((0, 0), (0, 0), (0, LANES - HY_EMB))),
        b1=hy_f_b1.reshape(depth, HY_FH, 1),
        w2T=jnp.swapaxes(hy_f_w2, 1, 2),
        b2=hy_f_b2.reshape(depth, HY_FH, 1),
        w3T=jnp.swapaxes(hy_f_w3, 1, 2),
        freq=hy_f_freq.reshape(depth, HY_FH, 1),
    )
    return dict(wa=cast(wa), wd=cast(wd), wg=cast(wg), wab=cast(wab), whT=cast(whT), wgate=cast(wgate),
                wqa=cast(wqa), wqb=cast(wqb), wka=cast(wka), wv=cast(wv),
                alog=pad128(gdn_a_log), dtb=pad128(gdn_dt_bias), hy=hy)


def kernel(x_prompt, x_sample, cache_mla_ckv, cache_mla_kpe, cache_diff_k, cache_diff_v, state_gdn, c, c_ctx, w_ada, b_ada, norm_mix_w, norm_mlp_w, w_in, mla_q_norm_w, mla_w_uq, mla_kv_norm_w, mla_w_ukv, gdn_conv_w, gdn_a_log, gdn_dt_bias, gdn_norm_w, hy_conv_w, hy_conv_b, hy_f_w1, hy_f_b1, hy_f_w2, hy_f_b2, hy_f_w3, hy_f_freq, hy_d, da_lq1, da_lk1, da_lq2, da_lk2, da_norm_w, w_branch, w_out, mlp_w1, mlp_w2, final_norm_w):
    depth = w_in.shape[0]
    bc, lc, d = x_prompt.shape
    bl, ll, _ = x_sample.shape
    T = ll
    assert bc * lc == T and d == D_MODEL and bc % 2 == 0 and bl == 2
    G = 1 + bl
    past = cache_mla_ckv.shape[2]
    tm = min(512, T)
    tq = min(256, lc)
    tq_lat, kc_lat = min(512, ll), min(1024, ll)

    pk = _pack_weights(w_in, mla_w_uq, mla_w_ukv, gdn_conv_w, gdn_a_log, gdn_dt_bias, hy_conv_w, hy_conv_b,
                       hy_f_w1, hy_f_b1, hy_f_w2, hy_f_b2, hy_f_w3, hy_f_freq, hy_d)
    wb_bf = w_branch.astype(bf16)
    wo_bf = w_out.astype(bf16)
    w1_bf = mlp_w1.astype(bf16)
    w2_bf = mlp_w2.astype(bf16)
    ca, cb, cd, sd = (jnp.asarray(t) for t in _rope_tables(T))

    cond8 = jnp.concatenate([c_ctx.reshape(1, d), c, jnp.zeros((8 - G, d), f32)], axis=0)
    mod = _modulation(cond8, w_ada, b_ada)

    kpe_pad = jnp.pad(cache_mla_kpe, ((0, 0), (0, 0), (0, 0), (MLA_NOPE, 128 - MLA_NOPE - MLA_ROPE)))
    kc_mla, vc_mla = _cache_kv(cache_mla_ckv, kpe_pad, pk["wka"], pk["wv"])
    kc_da = jnp.transpose(cache_diff_k.reshape(bl, depth, past, DA_H, 2 * DA_DK), (1, 3, 0, 2, 4)).astype(bf16)
    vc_da = jnp.transpose(cache_diff_v, (1, 3, 0, 2, 4))
    vc_da = jnp.concatenate([vc_da, jnp.ones(vc_da.shape[:-1] + (1,), f32),
                             jnp.zeros(vc_da.shape[:-1] + (LANES - DA_DV - 1,), f32)], axis=-1).astype(bf16)
    s0_bd = jnp.einsum('bldhij,hg->bldhigj', state_gdn, jnp.eye(GDN_H, dtype=f32)).reshape(
        bl, depth, 2, GDN_N, GDN_N)

    x = jnp.concatenate([x_prompt.reshape(1, T, d), x_sample], axis=0)
    new_ckv, new_kpe, new_dk, new_dv, new_state = [], [], [], [], []
    for l in range(depth):
        lam_init = 0.8 - 0.6 * math.exp(-0.3 * l)
        (zg, zab, zhT, kpe_raw, dk_raw, dv_raw, q, k, v, ckv, dq1, dq2, dk, dv) = _in_proj(
            x, mod, norm_mix_w[l], pk["wa"], pk["wd"], pk["wg"], pk["wab"], pk["whT"], ca, cb, cd, sd,
            mla_q_norm_w[l], mla_kv_norm_w[l], pk["wqa"], pk["wqb"], pk["wka"], pk["wv"], l, tm)
        new_ckv.append(ckv[0].reshape(bc, lc, MLA_KV_LORA))
        new_kpe.append(kpe_raw[0, :, MLA_NOPE:MLA_NOPE + MLA_ROPE].reshape(bc, lc, MLA_ROPE))
        new_dk.append(dk_raw[0].reshape(bc, lc, DA_H, 2, DA_DK))
        new_dv.append(dv_raw[0].reshape(bc, lc, DA_H, LANES)[..., :DA_DV])

        ctx_view = lambda a: a.reshape(a.shape[0], G * bc, lc, a.shape[3])
        oa_c = _mla_attention(ctx_view(q), ctx_view(k), ctx_view(v), 0, bc, lc, tq, lc)
        oa_l = _mla_attention(q, k, v, 1, bl, ll, tq_lat, kc_lat, cache=(kc_mla[l], vc_mla[l]))

        lamv = jnp.concatenate([da_lq1[l][None], da_lk1[l][None], da_lq2[l][None], da_lk2[l][None],
                                jnp.full((1, DA_DK), lam_init, f32), jnp.zeros((3, DA_DK), f32)], axis=0)
        od_c = _diff_attention(ctx_view(dq1), ctx_view(dq2), ctx_view(dk), ctx_view(dv), lamv, da_norm_w[l],
                               0, bc, lc, tq, lc)
        od_l = _diff_attention(dq1, dq2, dk, dv, lamv, da_norm_w[l], 1, bl, ll, tq_lat, kc_lat,
                               cache=(kc_da[l], vc_da[l]))

        zg_ctx = zg.reshape(G * bc, lc, zg.shape[2])
        zab_ctx = zab.reshape(G * bc, lc, 128)
        ob_c, s_gdn = _gdn(zg_ctx, zab_ctx, gdn_conv_w[l], pk["alog"][l], pk["dtb"][l], gdn_norm_w[l], 0, bc, lc, 2)
        ob_l = _gdn(zg, zab, gdn_conv_w[l], pk["alog"][l], pk["dtb"][l], gdn_norm_w[l], 1, bl, ll, 1,
                    s0=s0_bd, layer=l)[0]
        new_state.append(s_gdn)

        hw = {name: val[l] for name, val in pk["hy"].items()}
        oc_c = _hyena(zhT, lambda p: 0, lambda p: 0, lambda p: 2 * p, lambda p: 2 * p + 1, bc // 2, lc, 128, hw)
        oc_l = _hyena(zhT, lambda p: 1, lambda p: 2, lambda p: 0, lambda p: 0, 1, ll, 64, hw)

        x = _merge(x, mod, norm_mix_w[l], ((oa_c, oa_l), (ob_c, ob_l), (oc_c, oc_l), (od_c, od_l)),
                   pk["wgate"], wb_bf, wo_bf, l, tm)
        x = _mlp(x, mod, norm_mlp_w[l], w1_bf, w2_bf, final_norm_w, l, tm, final=l == depth - 1)

    y = x
    y_prompt = y[0].reshape(bc, lc, d)
    y_sample = y[1:]
    return (y_prompt, y_sample, jnp.stack(new_ckv, axis=1), jnp.stack(new_kpe, axis=1), jnp.stack(new_dk, axis=1),
            jnp.stack(new_dv, axis=1), jnp.stack(new_state, axis=1))
```

```python
import functools
import math

import numpy as np
import jax
import jax.numpy as jnp
from jax import lax
from jax.experimental import pallas as pl
from jax.experimental.pallas import tpu as pltpu

f32 = jnp.float32
bf16 = jnp.bfloat16

D_MODEL = 1024
GRID_W = 64
N_BRANCH = 4
MLA_H = 4
MLA_NOPE = 64
MLA_ROPE = 32
MLA_V = 64
MLA_Q_LORA = 256
MLA_KV_LORA = 128
GDN_H = 4
GDN_DK = 64
GDN_DV = 64
GDN_CHUNK = 64
HY_W = 256
HY_BANDS = 16
HY_EMB = 1 + 2 * HY_BANDS
HY_FH = 64
HY_SLOW_DECAY = math.log(1e-2) / 1.5
HY_FAST_DECAY = math.log(1e-2) / 0.3
DA_H = 4
DA_DK = 32
DA_DV = 64
D_FF = 4 * D_MODEL
ROPE_BASE = 10000.0
EPS = 1e-6
IN_SPLITS = (MLA_Q_LORA, MLA_KV_LORA, MLA_ROPE,
             GDN_H * GDN_DK, GDN_H * GDN_DK, GDN_H * GDN_DV, GDN_H * GDN_DV, 2 * GDN_H, 2 * GDN_H,
             3 * HY_W,
             DA_H * 2 * DA_DK, DA_H * 2 * DA_DK, DA_H * DA_DV,
             N_BRANCH * D_MODEL)

LOG2E = math.log2(math.e)
LANES = 128
V_ONE = 64
DFT_N2 = 256
VMEM_LIMIT = 56 * 1024 * 1024


def _cparams(sem):
    return pltpu.CompilerParams(dimension_semantics=sem, vmem_limit_bytes=VMEM_LIMIT)


def _dot(a, b):
    return jnp.dot(a.astype(bf16), b.astype(bf16), preferred_element_type=f32)


def _dot_nt(a, b):
    return lax.dot_general(a.astype(bf16), b.astype(bf16), (((1,), (1,)), ((), ())), preferred_element_type=f32)


def _dot_tn(a, b):
    return lax.dot_general(a.astype(bf16), b.astype(bf16), (((0,), (0,)), ((), ())), preferred_element_type=f32)


def _split(x):
    hi = x.astype(bf16)
    lo = (x - hi.astype(f32)).astype(bf16)
    return hi, lo


def _dot3(a, b):
    ah, al = _split(a)
    bh, bl = _split(b)
    return (jnp.dot(ah, bh, preferred_element_type=f32) + jnp.dot(ah, bl, preferred_element_type=f32)
            + jnp.dot(al, bh, preferred_element_type=f32))


def _dot3_w(a, bh, bl):
    ah, al = _split(a)
    return (jnp.dot(ah, bh, preferred_element_type=f32) + jnp.dot(ah, bl, preferred_element_type=f32)
            + jnp.dot(al, bh, preferred_element_type=f32))


def _sigmoid(x):
    return 1.0 / (1.0 + jnp.exp(-x))


def _silu(x):
    return x * _sigmoid(x)


def _softplus(x):
    return jnp.maximum(x, 0.0) + jnp.log(1.0 + jnp.exp(-jnp.abs(x)))


def _rms(x, w):
    return x * lax.rsqrt(jnp.mean(x * x, axis=-1, keepdims=True) + EPS) * w


def _mod_kernel(c_ref, w_ref, b_ref, o_ref):
    c = _silu(c_ref[...])
    o_ref[0] = _dot3(c, w_ref[0]) + b_ref[0]


def _modulation(cond8, w_ada, b_ada):
    depth, d, n6 = w_ada.shape
    tn = 1536
    return pl.pallas_call(
        _mod_kernel,
        grid=(depth, n6 // tn),
        in_specs=[pl.BlockSpec((8, d), lambda l, j: (0, 0)),
                  pl.BlockSpec((1, d, tn), lambda l, j: (l, 0, j)),
                  pl.BlockSpec((1, 1, tn), lambda l, j: (l, 0, j))],
        out_specs=pl.BlockSpec((1, 8, tn), lambda l, j: (l, 0, j)),
        out_shape=jax.ShapeDtypeStruct((depth, 8, n6), f32),
        compiler_params=_cparams(("arbitrary", "arbitrary")),
        name="modulation",
    )(cond8, w_ada, b_ada.reshape(depth, 1, n6))


def _norm_mod(x, nw, scale, shift):
    return _rms(x, nw) * (1.0 + scale) + shift


def _in_kernel(x_ref, mod_ref, nw_ref, wa_ref, wd_ref, wg_ref, wab_ref, whT_ref,
               ca_ref, cb_ref, cd_ref, sd_ref, qnw_ref, kvnw_ref, wqa_ref, wqb_ref, wka_ref, wv_ref,
               zg_ref, zab_ref, zhT_ref, kpe_ref, rdk_ref, rdv_ref,
               q_ref, k_ref, v_ref, ckv_ref, dq1_ref, dq2_ref, dk_ref, dv_ref):
    g = pl.program_id(0)
    d = D_MODEL
    mod = mod_ref[pl.ds(g, 1), :]
    h = _norm_mod(x_ref[0], nw_ref[...], mod[:, d:2 * d], mod[:, 0:d]).astype(bf16)
    zg_ref[0] = jnp.dot(h, wg_ref[...], preferred_element_type=f32)
    zab_ref[0] = jnp.dot(h, wab_ref[...], preferred_element_type=f32)
    zhT_ref[0] = lax.dot_general(whT_ref[...], h, (((1,), (1,)), ((), ())), preferred_element_type=f32)
    za = jnp.dot(h, wa_ref[...], preferred_element_type=f32)
    zd = jnp.dot(h, wd_ref[...], preferred_element_type=f32)
    kpe_ref[0] = za[:, 384:512]
    rdk_ref[0] = zd[:, 512:768]
    rdv_ref[0] = zd[:, 1024:1024 + DA_H * LANES]
    _attn_operands(za, zd, ca_ref, cb_ref, cd_ref, sd_ref, qnw_ref, kvnw_ref, wqa_ref, wqb_ref, wka_ref, wv_ref,
                   q_ref, k_ref, v_ref, ckv_ref, dq1_ref, dq2_ref, dk_ref, dv_ref)


def _layer_spec(arr, layer):
    nd = arr.ndim - 1
    return pl.BlockSpec((None,) + arr.shape[1:], lambda *_: (layer,) + (0,) * nd)


def _in_proj(x, mod, nw, wa, wd, wg, wab, whT, ca, cb, cd, sd, qnw, kvnw, wqa, wqb, wka, wv, layer, tm):
    G, T, d = x.shape
    ng, nab, nh = wg.shape[2], wab.shape[2], whT.shape[1]
    lay = lambda a: _layer_spec(a, layer)
    full = lambda shape: pl.BlockSpec(shape, lambda g, i: (0,) * len(shape))
    tok = lambda w: pl.BlockSpec((1, tm, w), lambda g, i: (g, i, 0))
    tab = lambda w: pl.BlockSpec((1, tm, w), lambda g, i: (jnp.minimum(g, 1), i, 0))
    hm = lambda w: pl.BlockSpec((4, 1, tm, w), lambda g, i: (0, g, i, 0))
    hms = lambda w: jax.ShapeDtypeStruct((4, G, T, w), bf16)
    toks = lambda w: jax.ShapeDtypeStruct((G, T, w), f32)
    return pl.pallas_call(
        _in_kernel,
        grid=(G, T // tm),
        in_specs=[tok(d), lay(mod), full((1, d)), lay(wa), lay(wd), lay(wg), lay(wab), lay(whT),
                  tab(128), tab(128), tab(256), tab(256), full((1, 256)), full((1, 128)),
                  lay(wqa), lay(wqb), lay(wka), lay(wv)],
        out_specs=[tok(ng), tok(nab), pl.BlockSpec((1, nh, tm), lambda g, i: (g, 0, i)),
                   tok(128), tok(256), tok(DA_H * LANES),
                   hm(128), hm(128), hm(128), tok(128), hm(64), hm(64), hm(64), hm(128)],
        out_shape=[toks(ng), toks(nab), jax.ShapeDtypeStruct((G, nh, T), f32),
                   toks(128), toks(256), toks(DA_H * LANES),
                   hms(128), hms(128), hms(128), toks(128), hms(64), hms(64), hms(64), hms(128)],
        compiler_params=_cparams(("arbitrary", "arbitrary")),
        name="in_proj",
    )(x, mod, nw.reshape(1, d), wa, wd, wg, wab, whT, ca, cb, cd, sd, qnw.reshape(1, -1), kvnw.reshape(1, -1),
      wqa, wqb, wka, wv)


def _ones_col(rows):
    return jnp.where(lax.broadcasted_iota(jnp.int32, (rows, LANES), 1) == V_ONE, 1.0, 0.0)


def _attn_operands(za, zd, ca_ref, cb_ref, cd_ref, sd_ref, qnw_ref, kvnw_ref, wqa_ref, wqb_ref, wka_ref, wv_ref,
                   q_ref, k_ref, v_ref, ckv_ref, dq1_ref, dq2_ref, dk_ref, dv_ref):
    cqn = _rms(za[:, 0:256], qnw_ref[...]).astype(bf16)
    ckv = _rms(za[:, 256:384], kvnw_ref[...])
    ckv_ref[0] = ckv
    ckvb = ckv.astype(bf16)
    ca = ca_ref[0]
    cb = cb_ref[0]
    qa = jnp.dot(cqn, wqa_ref[...], preferred_element_type=f32)
    qb = jnp.dot(cqn, wqb_ref[...], preferred_element_type=f32)
    kn = jnp.dot(ckvb, wka_ref[...], preferred_element_type=f32)
    vv = jnp.dot(ckvb, wv_ref[...], preferred_element_type=f32)
    krope = za[:, 384:512] * ca + za[:, 512:640] * cb
    qs = (MLA_NOPE + MLA_ROPE) ** -0.5 * LOG2E
    ones_col = _ones_col(za.shape[0])
    for h in range(MLA_H):
        sl = slice(128 * h, 128 * (h + 1))
        q_ref[h, 0] = ((qa[:, sl] * ca + qb[:, sl] * cb) * qs).astype(bf16)
        k_ref[h, 0] = (kn[:, sl] + krope).astype(bf16)
        v_ref[h, 0] = (vv[:, sl] + ones_col).astype(bf16)
    cd = cd_ref[0]
    sd = sd_ref[0]
    dqs = DA_DK ** -0.5 * LOG2E
    dq = (zd[:, 0:256] * cd + zd[:, 256:512] * sd) * dqs
    dk = zd[:, 512:768] * cd + zd[:, 768:1024] * sd
    first = (lax.broadcasted_iota(jnp.int32, dq.shape, 1) & (2 * DA_DK - 1)) < DA_DK
    dq1 = jnp.where(first, dq, 0.0)
    dq2 = jnp.where(first, 0.0, dq)
    for h in range(DA_H):
        sl = slice(64 * h, 64 * (h + 1))
        dq1_ref[h, 0] = dq1[:, sl].astype(bf16)
        dq2_ref[h, 0] = dq2[:, sl].astype(bf16)
        dk_ref[h, 0] = dk[:, sl].astype(bf16)
        dv_ref[h, 0] = (zd[:, 1024 + 128 * h:1024 + 128 * (h + 1)] + ones_col).astype(bf16)


def _cache_kv_kernel(ckv_ref, kpe_ref, wka_ref, wv_ref, k_ref, v_ref):
    ckvb = ckv_ref[0, 0].astype(bf16)
    kn = jnp.dot(ckvb, wka_ref[0], preferred_element_type=f32)
    vv = jnp.dot(ckvb, wv_ref[0], preferred_element_type=f32)
    kpe = kpe_ref[0, 0]
    ones_col = _ones_col(kpe.shape[0])
    for h in range(MLA_H):
        sl = slice(128 * h, 128 * (h + 1))
        k_ref[0, h, 0] = (kn[:, sl] + kpe).astype(bf16)
        v_ref[0, h, 0] = (vv[:, sl] + ones_col).astype(bf16)


def _cache_kv(cache_ckv, cache_kpe_pad, wka, wv):
    bl, depth, p, _ = cache_ckv.shape
    return pl.pallas_call(
        _cache_kv_kernel,
        grid=(depth, bl),
        in_specs=[pl.BlockSpec((1, 1, p, 128), lambda l, b: (b, l, 0, 0)),
                  pl.BlockSpec((1, 1, p, 128), lambda l, b: (b, l, 0, 0)),
                  pl.BlockSpec((1,) + wka.shape[1:], lambda l, b: (l, 0, 0)),
                  pl.BlockSpec((1,) + wv.shape[1:], lambda l, b: (l, 0, 0))],
        out_specs=[pl.BlockSpec((1, 4, 1, p, 128), lambda l, b: (l, 0, b, 0, 0)),
                   pl.BlockSpec((1, 4, 1, p, 128), lambda l, b: (l, 0, b, 0, 0))],
        out_shape=[jax.ShapeDtypeStruct((depth, 4, bl, p, 128), bf16),
                   jax.ShapeDtypeStruct((depth, 4, bl, p, 128), bf16)],
        compiler_params=_cparams(("arbitrary", "arbitrary")),
        name="cache_kv",
    )(cache_ckv, cache_kpe_pad, wka, wv)


ATT_RB = 64


def _attn_scratch(nchains, tq, kc):
    return [pltpu.VMEM((nchains, tq, 1), f32), pltpu.VMEM((nchains, tq, 1), f32),
            pltpu.VMEM((nchains, tq, LANES), f32), pltpu.VMEM((nchains, tq, kc), f32),
            pltpu.VMEM((nchains, tq, kc), bf16)]


def _attn_scan(q_refs, k_ref, v_ref, cache_refs, scratch, kc):
    m_sc, al_sc, acc_sc, s_sc, p_sc = scratch
    nh = k_ref.shape[0]
    lk = k_ref.shape[2]
    m_sc[...] = jnp.full(m_sc.shape, -jnp.inf, f32)
    acc_sc[...] = jnp.zeros(acc_sc.shape, f32)

    tq = m_sc.shape[1]
    chains = [(j * nh + h, q_ref, h) for j, q_ref in enumerate(q_refs) for h in range(nh)]

    def step(get_k, get_v, kw):
        for c, q_ref, h in chains:
            s_sc[c, :, 0:kw] = lax.dot_general(q_ref[h, 0], get_k(h), (((1,), (1,)), ((), ())),
                                               preferred_element_type=f32)
        for c, _, _ in chains:
            for rb in range(tq // ATT_RB):
                rows = slice(rb * ATT_RB, (rb + 1) * ATT_RB)
                s = s_sc[c, rows, 0:kw]
                m_old = m_sc[c, rows, :]
                m_new = jnp.maximum(m_old, jnp.max(s, axis=1, keepdims=True))
                p_sc[c, rows, 0:kw] = jnp.exp2(s - m_new).astype(bf16)
                al_sc[c, rows, :] = jnp.exp2(m_old - m_new)
                m_sc[c, rows, :] = m_new
        for c, _, h in chains:
            acc_sc[c] = al_sc[c] * acc_sc[c] + jnp.dot(p_sc[c, :, 0:kw], get_v(h), preferred_element_type=f32)

    def body(i, carry):
        rows = pl.ds(pl.multiple_of(i * kc, kc), kc)
        step(lambda h: k_ref[h, 0, rows, :], lambda h: v_ref[h, 0, rows, :], kc)
        return carry

    lax.fori_loop(0, lk // kc, body, 0)
    if cache_refs is not None:
        kc_ref, vc_ref = cache_refs
        step(lambda h: kc_ref[h, 0], lambda h: vc_ref[h, 0], kc_ref.shape[2])


def _attn_out(acc):
    return acc[:, 0:V_ONE] / acc[:, V_ONE:V_ONE + 1]


def _mla_attn_kernel(*refs, kc, has_cache):
    if has_cache:
        q_ref, k_ref, v_ref, kc_ref, vc_ref, o_ref, *scratch = refs
        cache_refs = (kc_ref, vc_ref)
    else:
        q_ref, k_ref, v_ref, o_ref, *scratch = refs
        cache_refs = None
    _attn_scan((q_ref,), k_ref, v_ref, cache_refs, scratch, kc)
    acc_sc = scratch[2]
    for h in range(MLA_H):
        o_ref[0, :, MLA_V * h:MLA_V * (h + 1)] = _attn_out(acc_sc[h])


def _mla_attention(q, k, v, seq_off, nseq, L, tq, kc, cache=None):
    H = q.shape[0]
    kv = lambda rows, off: pl.BlockSpec((H, 1, rows, 128), lambda b, i: (0, b + off, 0, 0))
    in_specs = [pl.BlockSpec((H, 1, tq, 128), lambda b, i: (0, b + seq_off, i, 0)), kv(L, seq_off), kv(L, seq_off)]
    args = [q, k, v]
    if cache is not None:
        in_specs += [kv(cache[0].shape[2], 0), kv(cache[0].shape[2], 0)]
        args += list(cache)
    return pl.pallas_call(
        functools.partial(_mla_attn_kernel, kc=kc, has_cache=cache is not None),
        grid=(nseq, L // tq),
        in_specs=in_specs,
        out_specs=pl.BlockSpec((1, tq, H * MLA_V), lambda b, i: (b, i, 0)),
        out_shape=jax.ShapeDtypeStruct((nseq, L, H * MLA_V), f32),
        scratch_shapes=_attn_scratch(H, tq, kc),
        compiler_params=_cparams(("arbitrary", "arbitrary")),
        name="mla_attn",
    )(*args)


def _diff_attn_kernel(*refs, kc, has_cache):
    if has_cache:
        q1_ref, q2_ref, k_ref, v_ref, kc_ref, vc_ref, lam_ref, nw_ref, o_ref, *scratch = refs
        cache_refs = (kc_ref, vc_ref)
    else:
        q1_ref, q2_ref, k_ref, v_ref, lam_ref, nw_ref, o_ref, *scratch = refs
        cache_refs = None
    acc_sc = scratch[2]
    lamv = lam_ref[...]
    lam_init = lamv[4:5, 0:1]
    lam = (jnp.exp(jnp.sum(lamv[0:1] * lamv[1:2], axis=1, keepdims=True))
           - jnp.exp(jnp.sum(lamv[2:3] * lamv[3:4], axis=1, keepdims=True)) + lam_init)
    _attn_scan((q1_ref, q2_ref), k_ref, v_ref, cache_refs, scratch, kc)
    for h in range(DA_H):
        o = _attn_out(acc_sc[h]) - lam * _attn_out(acc_sc[DA_H + h])
        o_ref[0, :, DA_DV * h:DA_DV * (h + 1)] = _rms(o, nw_ref[...]) * (1.0 - lam_init)


def _diff_attention(q1, q2, k, v, lamv, nw, seq_off, nseq, L, tq, kc, cache=None):
    H = q1.shape[0]
    hm = lambda rows, w, off: pl.BlockSpec((H, 1, rows, w), lambda b, i: (0, b + off, 0, 0))
    qs = pl.BlockSpec((H, 1, tq, 64), lambda b, i: (0, b + seq_off, i, 0))
    in_specs = [qs, qs, hm(L, 64, seq_off), hm(L, 128, seq_off)]
    args = [q1, q2, k, v]
    if cache is not None:
        p = cache[0].shape[2]
        in_specs += [hm(p, 64, 0), hm(p, 128, 0)]
        args += list(cache)
    in_specs += [pl.BlockSpec((8, DA_DK), lambda b, i: (0, 0)), pl.BlockSpec((1, DA_DV), lambda b, i: (0, 0))]
    args += [lamv, nw.reshape(1, DA_DV)]
    return pl.pallas_call(
        functools.partial(_diff_attn_kernel, kc=kc, has_cache=cache is not None),
        grid=(nseq, L // tq),
        in_specs=in_specs,
        out_specs=pl.BlockSpec((1, tq, H * DA_DV), lambda b, i: (b, i, 0)),
        out_shape=jax.ShapeDtypeStruct((nseq, L, H * DA_DV), f32),
        scratch_shapes=_attn_scratch(2 * H, tq, kc),
        compiler_params=_cparams(("arbitrary", "arbitrary")),
        name="diff_attn",
    )(*args)


GDN_N = GDN_H * GDN_CHUNK
GDN_INV_BASE = 8
(M_BD, M_EYE, M_BASE, M_OFF8, M_OFF16, M_OFF32, M_DIR) = range(7)
GDN_NMASK = M_DIR + 6


def _gdn_fill_masks(msk):
    n, c = GDN_N, GDN_CHUNK
    rr = lax.broadcasted_iota(jnp.int32, (n, n), 0)
    cc = lax.broadcasted_iota(jnp.int32, (n, n), 1)
    blk = lambda x, s: lax.shift_right_logical(x, int(math.log2(s)))
    bd = blk(rr, c) == blk(cc, c)
    ri = rr & (c - 1)
    cj = cc & (c - 1)

    def put(i, cond):
        msk[i] = jnp.where(cond, 1.0, 0.0)

    put(M_BD, bd)
    put(M_EYE, rr == cc)
    put(M_BASE, blk(rr, GDN_INV_BASE) == blk(cc, GDN_INV_BASE))
    for slot, s in ((M_OFF8, 8), (M_OFF16, 16), (M_OFF32, 32)):
        put(slot, (blk(rr, 2 * s) == blk(cc, 2 * s)) & (blk(rr, s) != blk(cc, s)))
    put(M_DIR + 0, bd & (ri >= cj))
    put(M_DIR + 1, bd & (ri > cj))
    put(M_DIR + 2, bd & (ri <= cj))
    put(M_DIR + 3, bd & (ri <= cj))
    put(M_DIR + 4, bd & (ri < cj))
    put(M_DIR + 5, bd & (ri >= cj))


def _gdn_conv_chunk(x_ref, s, w, r, L):
    c = GDN_CHUNK
    x = x_ref[s, pl.ds(r, c), :]
    prev8 = x_ref[s, pl.ds(pl.multiple_of(jnp.maximum(r - 8, 0), 8), 8), :]
    next8 = x_ref[s, pl.ds(pl.multiple_of(jnp.minimum(r + c, L - 8), 8), 8), :]
    prev = jnp.where(r > 0, prev8[7:8, :], 0.0)
    nxt = jnp.where(r + c < L, next8[0:1, :], 0.0)
    row = lax.broadcasted_iota(jnp.int32, x.shape, 0)
    xp = jnp.where(row == 0, prev, pltpu.roll(x, 1, 0))
    xn = jnp.where(row == c - 1, nxt, pltpu.roll(x, c - 1, 0))
    y = xp * w[0:1] + x * w[1:2] + xn * w[2:3]
    return _silu(y)


def _group_sum(y, bdb):
    hi, lo = _split(y)
    return jnp.dot(hi, bdb, preferred_element_type=f32) + jnp.dot(lo, bdb, preferred_element_type=f32)


(B_K, B_Q, B_KB, B_RU, B_E, B_TRI, B_T, B_P, B_X, B_U, B_W, B_IN, B_VN) = range(13)
GDN_NBUF = 13
GDN_DOUBLE_BUFFER_MAX = 2 * 1024 * 1024


def _gdn_chunk_chains(xs, gabs, alog, dtb, S, pools, msk):
    n, c = GDN_N, GDN_CHUNK
    bdb = msk[M_BD].astype(bf16)
    small = tuple({} for _ in xs)

    def stage(fn):
        for ch in range(len(xs)):
            fn(ch, pools[ch], small[ch])

    def prep(ch, B, sm):
        d = ch % 2
        xq, xk, xv = xs[ch]
        bd = msk[M_BD]
        q = xq * lax.rsqrt(_group_sum(xq * xq, bdb) + EPS) * (GDN_DK ** -0.5)
        k = xk * lax.rsqrt(_group_sum(xk * xk, bdb) + EPS)
        gab = gabs[ch]
        lane = lax.broadcasted_iota(jnp.int32, gab.shape, 1)
        gfull = -jnp.exp(alog) * _softplus(gab + dtb)
        bfull = _sigmoid(gab)

        def stack(arr, base):
            return jnp.concatenate([jnp.sum(jnp.where(lane == base + h, arr, 0.0), axis=1, keepdims=True)
                                    for h in range(GDN_H)], axis=0)

        g_stack = stack(gfull, d * GDN_H)
        b_stack = stack(bfull, (2 + d) * GDN_H)
        g_b = jnp.broadcast_to(g_stack, (n, n))
        g_row = jnp.sum(g_b * msk[M_EYE], axis=0, keepdims=True)
        gc_row = jnp.sum(g_b * msk[M_DIR + 3 * d + 2], axis=0, keepdims=True)
        g_rb = jnp.broadcast_to(g_row, (n, n))
        gc_col = jnp.sum(g_rb * msk[M_DIR + 3 * d], axis=1, keepdims=True)
        g_last = jnp.sum(g_rb * bd, axis=1, keepdims=True)
        B[B_E] = jnp.exp(jnp.minimum(gc_col - gc_row, 0.0))
        tile4 = lambda x: jnp.concatenate([x] * GDN_H, axis=0)
        k_bd = tile4(k) * bd
        B[B_K] = k_bd
        B[B_KB] = k_bd * b_stack
        B[B_Q] = tile4(q) * bd
        B[B_RU] = tile4(xv) * (bd * b_stack)
        sm.update(gc_col=gc_col, egc=jnp.exp(gc_col), g_last=g_last)

    def tri(ch, B, sm):
        t = _dot_nt(B[B_KB], B[B_K]) * (B[B_E] * msk[M_DIR + 3 * (ch % 2) + 1])
        B[B_TRI] = t
        nm = -(t * msk[M_BASE])
        B[B_P] = nm
        B[B_T] = msk[M_EYE] + nm

    def intra(ch, B, sm):
        B[B_IN] = _dot_nt(B[B_Q], B[B_K]) * (B[B_E] * msk[M_DIR + 3 * (ch % 2)])

    def square(d, B, sm):
        B[B_P] = _dot(B[B_P], B[B_P])

    def extend(d, B, sm):
        B[B_T] = B[B_T] + _dot(B[B_T], B[B_P])

    stage(prep)
    stage(tri)
    stage(intra)
    for _ in range(2):
        stage(square)
        stage(extend)
    for slot in (M_OFF8, M_OFF16, M_OFF32):
        def cross(d, B, sm, slot=slot):
            B[B_X] = _dot(B[B_T], B[B_TRI] * msk[slot])

        def merge(d, B, sm):
            B[B_T] = B[B_T] - _dot(B[B_X], B[B_T])

        stage(cross)
        stage(merge)

    def solve(d, B, sm):
        uw = _dot(B[B_T], jnp.concatenate([B[B_RU], B[B_KB] * sm["egc"]], axis=1))
        B[B_U] = uw[:, 0:n]
        B[B_W] = uw[:, n:2 * n]

    def v_new(d, B, sm):
        B[B_VN] = B[B_U] - _dot(B[B_W], S[d])

    def out(d, B, sm):
        o_bd = _dot(B[B_Q] * sm["egc"], S[d]) + _dot(B[B_IN], B[B_VN])
        sm["o"] = o_bd[0:c] + o_bd[c:2 * c] + o_bd[2 * c:3 * c] + o_bd[3 * c:4 * c]

    def update(d, B, sm):
        S[d] = (S[d] * jnp.exp(sm["g_last"])
                + _dot_tn(B[B_K] * jnp.exp(sm["g_last"] - sm["gc_col"]), B[B_VN]))

    for fn in (solve, v_new, out, update):
        stage(fn)
    return [sm["o"] for sm in small]


def _gdn_kernel(*refs, has_s0):
    if has_s0:
        (q_ref, k_ref, v_ref, z_ref, cwq_ref, cwk_ref, cwv_ref, gab_ref, alog_ref, dtb_ref, nw_ref, s0_ref,
         o_ref, of, ob, msk, S, *pools) = refs
        sfin_ref = None
    else:
        (q_ref, k_ref, v_ref, z_ref, cwq_ref, cwk_ref, cwv_ref, gab_ref, alog_ref, dtb_ref, nw_ref,
         o_ref, sfin_ref, of, ob, msk, S, *pools) = refs
    c = GDN_CHUNK
    ns, L = q_ref.shape[0], q_ref.shape[1]
    n = L // c

    @pl.when(pl.program_id(0) == 0)
    def _():
        _gdn_fill_masks(msk)

    alog = alog_ref[...]
    dtb = dtb_ref[...]
    cwq, cwk, cwv = cwq_ref[...], cwk_ref[...], cwv_ref[...]
    if has_s0:
        for s in range(ns):
            S[2 * s:2 * s + 2] = s0_ref[s, 0]
    else:
        S[...] = jnp.zeros(S.shape, f32)

    def body(i, carry):
        starts = (pl.multiple_of(i * c, c), pl.multiple_of((n - 1 - i) * c, c))
        chains = [(s, r) for s in range(ns) for r in starts]
        xs = [(_gdn_conv_chunk(q_ref, s, cwq, r, L), _gdn_conv_chunk(k_ref, s, cwk, r, L),
               _gdn_conv_chunk(v_ref, s, cwv, r, L)) for s, r in chains]
        gabs = [gab_ref[s, pl.ds(r, c), :] for s, r in chains]
        outs = _gdn_chunk_chains(xs, gabs, alog, dtb, S, pools, msk)
        for ch, (s, r) in enumerate(chains):
            (of, ob)[ch % 2][s, pl.ds(r, c), :] = outs[ch]
        return carry

    lax.fori_loop(0, n, body, 0)
    if sfin_ref is not None:
        for s in range(ns):
            for direction in range(2):
                for h in range(GDN_H):
                    sfin_ref[s, direction, h] = S[2 * s + direction, c * h:c * (h + 1), c * h:c * (h + 1)]

    bdb = msk[M_BD].astype(bf16)
    nw = nw_ref[...]

    def norm_gate(i, carry):
        rows = pl.ds(pl.multiple_of(i * c, c), c)
        for s in range(ns):
            o = of[s, rows, :] + ob[s, rows, :]
            ms = _group_sum(o * o, bdb) * (1.0 / GDN_DV)
            o_ref[s, rows, :] = o * lax.rsqrt(ms + EPS) * nw * _silu(z_ref[s, rows, :])
        return carry

    lax.fori_loop(0, n, norm_gate, 0)


def _gdn(zg, zab, cw, alog_row, dtb_row, nw, seq_off, nseq, L, ns, s0=None, layer=0):
    H = GDN_H
    w = H * GDN_DK
    assert seq_off % ns == 0 and nseq % ns == 0
    off = seq_off // ns
    mode = dict(pipeline_mode=pl.Buffered(1)) if ns * L * w * 4 > GDN_DOUBLE_BUFFER_MAX else {}
    tok = lambda k: pl.BlockSpec((ns, L, w), lambda s, k=k: (s + off, 0, k), **mode)
    cws = lambda k: pl.BlockSpec((3, w), lambda s, k=k: (0, k))
    full = lambda shape: pl.BlockSpec(shape, lambda s: (0,) * len(shape))
    in_specs = [tok(0), tok(1), tok(2), tok(3), cws(0), cws(1), cws(2),
                pl.BlockSpec((ns, L, 128), lambda s: (s + off, 0, 0)),
                full((1, 128)), full((1, 128)), full((1, w))]
    args = [zg, zg, zg, zg, cw, cw, cw, zab, alog_row, dtb_row, jnp.tile(nw.reshape(1, GDN_DV), (1, H))]
    out_specs = [pl.BlockSpec((ns, L, w), lambda s: (s, 0, 0))]
    out_shape = [jax.ShapeDtypeStruct((nseq, L, w), f32)]
    if s0 is not None:
        in_specs.append(pl.BlockSpec((ns, 1, 2, GDN_N, GDN_N), lambda s: (s, layer, 0, 0, 0)))
        args.append(s0)
    else:
        out_specs.append(pl.BlockSpec((ns, 2, H, GDN_DK, GDN_DV), lambda s: (s, 0, 0, 0, 0)))
        out_shape.append(jax.ShapeDtypeStruct((nseq, 2, H, GDN_DK, GDN_DV), f32))
    pool = pltpu.VMEM((GDN_NBUF, GDN_N, GDN_N), f32)
    return pl.pallas_call(
        functools.partial(_gdn_kernel, has_s0=s0 is not None),
        grid=(nseq // ns,),
        in_specs=in_specs,
        out_specs=out_specs,
        out_shape=out_shape,
        scratch_shapes=[pltpu.VMEM((ns, L, w), f32), pltpu.VMEM((ns, L, w), f32),
                        pltpu.VMEM((GDN_NMASK, GDN_N, GDN_N), f32), pltpu.VMEM((2 * ns, GDN_N, GDN_N), f32)]
                       + [pool] * (2 * ns),
        compiler_params=_cparams(("arbitrary",)),
        name="gdn",
    )(*args)


def _bitrev(p, bits):
    r = 0
    for _ in range(bits):
        r = (r << 1) | (p & 1)
        p >>= 1
    return r


@functools.lru_cache(maxsize=None)
def _fft_tables(L):
    n = 2 * L
    n2 = DFT_N2
    n1 = n // n2
    bits = n1.bit_length() - 1
    npair = max(n1 // 2, 1)
    sta = np.zeros((max(bits, 1) * npair, 2 * n2), np.float64)
    stb = np.zeros_like(sta)
    for s in range(bits):
        half = n1 >> (s + 1)
        for p in range(npair):
            j = p % half
            ang = -2.0 * np.pi * j / (2 * half)
            wr, wi = np.cos(ang), np.sin(ang)
            sta[s * npair + p, :] = wr
            stb[s * npair + p, :n2] = -wi
            stb[s * npair + p, n2:] = wi
    twa = np.zeros((n1, 2 * n2), np.float64)
    twb = np.zeros_like(twa)
    lanes = np.arange(n2)
    for p in range(n1):
        ang = -2.0 * np.pi * lanes * _bitrev(p, bits) / n
        twa[p, :n2] = np.cos(ang)
        twa[p, n2:] = np.cos(ang)
        twb[p, :n2] = -np.sin(ang)
        twb[p, n2:] = np.sin(ang)
    kn = np.outer(lanes, lanes) * (-2.0 * np.pi / n2)
    fr, fi = np.cos(kn), np.sin(kn)
    fwd = np.block([[fr, fi], [-fi, fr]])
    inv = np.block([[fr, -fi], [fi, fr]])
    as32 = lambda a: np.asarray(a, np.float32)
    return dict(n1=n1, bits=bits, npair=npair, sta=as32(sta), stb=as32(stb), twa=as32(twa), twb=as32(twb),
                fwd=as32(fwd), inv=as32(inv))


@functools.lru_cache(maxsize=None)
def _hyena_pos_table(L):
    n = 2 * L
    idx = np.arange(n)
    pos = np.where(idx < L, idx, n - idx).astype(np.float64)
    pos[L] = 0.0
    t = pos / max(L - 1, 1)
    bands = np.linspace(1e-4, HY_BANDS - 1, HY_BANDS).astype(np.float32).astype(np.float64)
    ang = (2.0 * math.pi * pos / L)[None, :] * bands[:, None]
    z = np.zeros((LANES, n), np.float64)
    z[0] = t
    z[1:1 + HY_BANDS] = np.cos(ang)
    z[1 + HY_BANDS:1 + 2 * HY_BANDS] = -np.sin(ang)
    deltas = np.abs(np.linspace(HY_SLOW_DECAY, HY_FAST_DECAY, HY_W)).reshape(HY_W, 1)
    return np.asarray(z, np.float32), np.asarray(deltas, np.float32)


def _swap_halves(x):
    n2 = x.shape[1] // 2
    return jnp.concatenate([x[:, n2:], x[:, :n2]], axis=1)


def _fft_forward(X, sta_ref, stb_ref, twa_ref, twb_ref, fh_ref, fl_ref, n1, bits, npair, ct, mrows):
    for s in range(bits):
        half = n1 >> (s + 1)

        def pair(p, carry, s=s, half=half):
            grp = p // half
            j = p - grp * half
            a = grp * 2 * half + j
            ra = pl.ds(pl.multiple_of(a * ct, ct), ct)
            rb = pl.ds(pl.multiple_of((a + half) * ct, ct), ct)
            xa = X[ra, :]
            xb = X[rb, :]
            X[ra, :] = xa + xb
            d = xa - xb
            X[rb, :] = d * sta_ref[pl.ds(s * npair + p, 1), :] + _swap_halves(d) * stb_ref[pl.ds(s * npair + p, 1), :]
            return carry

        lax.fori_loop(0, npair, pair, 0)

    def blk(p, carry):
        r = pl.ds(pl.multiple_of(p * ct, ct), ct)
        y = X[r, :]
        X[r, :] = y * twa_ref[pl.ds(p, 1), :] + _swap_halves(y) * twb_ref[pl.ds(p, 1), :]
        return carry

    lax.fori_loop(0, n1, blk, 0)

    def mm(i, carry):
        r = pl.ds(pl.multiple_of(i * mrows, mrows), mrows)
        X[r, :] = _dot3_w(X[r, :], fh_ref[...], fl_ref[...])
        return carry

    lax.fori_loop(0, n1 * ct // mrows, mm, 0)


def _fft_inverse(X, sta_ref, stb_ref, twa_ref, twb_ref, fh_ref, fl_ref, n1, bits, npair, ct, mrows):
    def mm(i, carry):
        r = pl.ds(pl.multiple_of(i * mrows, mrows), mrows)
        X[r, :] = _dot3_w(X[r, :], fh_ref[...], fl_ref[...])
        return carry

    lax.fori_loop(0, n1 * ct // mrows, mm, 0)

    def blk(p, carry):
        r = pl.ds(pl.multiple_of(p * ct, ct), ct)
        y = X[r, :]
        X[r, :] = y * twa_ref[pl.ds(p, 1), :] - _swap_halves(y) * twb_ref[pl.ds(p, 1), :]
        return carry

    lax.fori_loop(0, n1, blk, 0)

    for s in reversed(range(bits)):
        half = n1 >> (s + 1)

        def pair(p, carry, s=s, half=half):
            grp = p // half
            j = p - grp * half
            a = grp * 2 * half + j
            ra = pl.ds(pl.multiple_of(a * ct, ct), ct)
            rb = pl.ds(pl.multiple_of((a + half) * ct, ct), ct)
            xa = X[ra, :]
            xb = X[rb, :]
            tw = xb * sta_ref[pl.ds(s * npair + p, 1), :] - _swap_halves(xb) * stb_ref[pl.ds(s * npair + p, 1), :]
            X[ra, :] = xa + tw
            X[rb, :] = xa - tw
            return carry

        lax.fori_loop(0, npair, pair, 0)


def _conv3_lanes(x, w, b):
    L = x.shape[1]
    lane = lax.broadcasted_iota(jnp.int32, x.shape, 1)
    xp = jnp.where(lane == 0, 0.0, pltpu.roll(x, 1, 1))
    xn = jnp.where(lane == L - 1, 0.0, pltpu.roll(x, L - 1, 1))
    return xp * w[:, 0:1] + x * w[:, 1:2] + xn * w[:, 2:3] + b


def _hyena_hidden_kernel(zt_ref, w1_ref, b1_ref, w2_ref, b2_ref, fr_ref, h_ref):
    fr = fr_ref[...]
    h = jnp.sin(fr * (_dot3(w1_ref[...], zt_ref[...]) + b1_ref[...]))
    h_ref[...] = jnp.sin(fr * (_dot3(w2_ref[...], h) + b2_ref[...]))


def _hyena_hidden(zt, hw):
    nblk = max(zt.shape[1] // 2048, 1)
    tn = zt.shape[1] // nblk
    full = lambda a: pl.BlockSpec(a.shape, lambda j: (0,) * a.ndim)
    names = ("w1T", "b1", "w2T", "b2", "freq")
    return pl.pallas_call(
        _hyena_hidden_kernel,
        grid=(nblk,),
        in_specs=[pl.BlockSpec((zt.shape[0], tn), lambda j: (0, j))] + [full(hw[k]) for k in names],
        out_specs=pl.BlockSpec((HY_FH, tn), lambda j: (0, j)),
        out_shape=jax.ShapeDtypeStruct((HY_FH, zt.shape[1]), f32),
        compiler_params=_cparams(("arbitrary",)),
        name="hyena_hidden",
    )(zt, *[hw[k] for k in names])


def _hyena_kernel(x0a_ref, x1a_ref, va_ref, x0b_ref, x1b_ref, vb_ref, cw0_ref, cw1_ref, cw2_ref,
                  cb0_ref, cb1_ref, cb2_ref, d_ref, t_ref, dl_ref, h_ref,
                  w3f_ref, w3b_ref, sta_ref, stb_ref, twa_ref, twb_ref,
                  ffh_ref, ffl_ref, fih_ref, fil_ref, o_ref, X, HA, HB, *, L, n1, bits, npair, ct, mrows):
    n2 = DFT_N2
    n = 2 * L
    fft_args = (sta_ref, stb_ref, twa_ref, twb_ref)

    @pl.when(pl.program_id(1) == 0)
    def _():
        h = h_ref[...]
        hf = _dot3(w3f_ref[...], h)
        hb = _dot3(w3b_ref[...], h)
        lane = lax.broadcasted_iota(jnp.int32, hf.shape, 1)
        dec = jnp.exp(-t_ref[0:1, :] * dl_ref[...])
        hc = jnp.where(lane < L, hf, jnp.where(lane > L, hb, 0.0)) * dec
        for b in range(n1):
            X[b * ct:(b + 1) * ct, 0:n2] = hc[:, b * n2:(b + 1) * n2]
            X[b * ct:(b + 1) * ct, n2:2 * n2] = jnp.zeros((ct, n2), f32)
        _fft_forward(X, *fft_args, ffh_ref, ffl_ref, n1, bits, npair, ct, mrows)
        hs = X[...]
        HA[...] = jnp.concatenate([hs[:, :n2], hs[:, :n2]], axis=1)
        HB[...] = jnp.concatenate([-hs[:, n2:], hs[:, n2:]], axis=1)

    x0a = _conv3_lanes(x0a_ref[0], cw0_ref[...], cb0_ref[...])
    x0b = _conv3_lanes(x0b_ref[0], cw0_ref[...], cb0_ref[...])
    vva = _conv3_lanes(va_ref[0], cw2_ref[...], cb2_ref[...]) * _conv3_lanes(x1a_ref[0], cw1_ref[...], cb1_ref[...])
    vvb = _conv3_lanes(vb_ref[0], cw2_ref[...], cb2_ref[...]) * _conv3_lanes(x1b_ref[0], cw1_ref[...], cb1_ref[...])
    nb = L // n2
    for b in range(nb):
        X[b * ct:(b + 1) * ct, 0:n2] = vva[:, b * n2:(b + 1) * n2]
        X[b * ct:(b + 1) * ct, n2:2 * n2] = vvb[:, b * n2:(b + 1) * n2]
    X[nb * ct:n1 * ct, :] = jnp.zeros(((n1 - nb) * ct, 2 * n2), f32)
    _fft_forward(X, *fft_args, ffh_ref, ffl_ref, n1, bits, npair, ct, mrows)

    def spec(i, carry):
        r = pl.ds(pl.multiple_of(i * ct, ct), ct)
        x = X[r, :]
        X[r, :] = x * HA[r, :] + _swap_halves(x) * HB[r, :]
        return carry

    lax.fori_loop(0, n1, spec, 0)
    _fft_inverse(X, *fft_args, fih_ref, fil_ref, n1, bits, npair, ct, mrows)
    inv_n = 1.0 / n
    ya = jnp.concatenate([X[b * ct:(b + 1) * ct, 0:n2] for b in range(nb)], axis=1) * inv_n
    yb = jnp.concatenate([X[b * ct:(b + 1) * ct, n2:2 * n2] for b in range(nb)], axis=1) * inv_n
    dcol = d_ref[...]
    o_ref[0] = (ya + vva * dcol) * x0a
    o_ref[1] = (yb + vvb * dcol) * x0b


def _hyena(zhT, grp_a, grp_b, lane_a, lane_b, npairs, L, ct, hw):
    tabs = _fft_tables(L)
    n1, bits, npair = tabs["n1"], tabs["bits"], tabs["npair"]
    n = 2 * L
    ntile = HY_W // ct
    mrows = min(512, n1 * ct)
    zt, deltas = _hyena_pos_table(L)
    fwd = jnp.asarray(tabs["fwd"])
    inv = jnp.asarray(tabs["inv"])
    ffh = fwd.astype(bf16)
    ffl = (fwd - ffh.astype(f32)).astype(bf16)
    fih = inv.astype(bf16)
    fil = (inv - fih.astype(f32)).astype(bf16)
    xin = lambda k, grp, ln: pl.BlockSpec((1, ct, L), lambda j, p, k=k: (grp(p), j + ntile * k, ln(p)))
    chan = lambda k, w: pl.BlockSpec((ct, w), lambda j, p, k=k: (j + ntile * k, 0))
    full = lambda a: pl.BlockSpec(a.shape, lambda j, p: (0,) * a.ndim)
    hidden = _hyena_hidden(jnp.asarray(zt), hw)
    consts = [jnp.asarray(zt[0:8]), jnp.asarray(deltas)]
    in_specs = ([xin(0, grp_a, lane_a), xin(1, grp_a, lane_a), xin(2, grp_a, lane_a),
                 xin(0, grp_b, lane_b), xin(1, grp_b, lane_b), xin(2, grp_b, lane_b),
                 chan(0, 3), chan(1, 3), chan(2, 3), chan(0, 1), chan(1, 1), chan(2, 1), chan(0, 1),
                 full(consts[0]), chan(0, 1), full(hidden),
                 chan(0, HY_FH), chan(1, HY_FH)]
                + [full(jnp.asarray(tabs[k])) for k in ("sta", "stb", "twa", "twb")]
                + [full(ffh), full(ffl), full(fih), full(fil)])
    args = ([zhT] * 6 + [hw["cwT"]] * 3 + [hw["cb"]] * 3 + [hw["d"], consts[0], consts[1], hidden,
            hw["w3T"], hw["w3T"]]
            + [jnp.asarray(tabs[k]) for k in ("sta", "stb", "twa", "twb")] + [ffh, ffl, fih, fil])
    return pl.pallas_call(
        functools.partial(_hyena_kernel, L=L, n1=n1, bits=bits, npair=npair, ct=ct, mrows=mrows),
        grid=(ntile, npairs),
        in_specs=in_specs,
        out_specs=pl.BlockSpec((2, ct, L), lambda j, p: (p, j, 0)),
        out_shape=jax.ShapeDtypeStruct((2 * npairs, HY_W, L), f32),
        scratch_shapes=[pltpu.VMEM((n1 * ct, 2 * DFT_N2), f32)] * 3,
        compiler_params=_cparams(("arbitrary", "arbitrary")),
        name="hyena",
    )(*args)


def _merge_kernel(x_ref, mod_ref, nw_ref, oa_c, oa_l, ob_c, ob_l, oc_c, oc_l, od_c, od_l, wg_ref, wb_ref, wo_ref,
                  xo_ref):
    g = pl.program_id(0)
    d = D_MODEL
    x = x_ref[0]
    mod = mod_ref[pl.ds(g, 1), :]
    h = _norm_mod(x, nw_ref[...], mod[:, d:2 * d], mod[:, 0:d]).astype(bf16)
    ctx = g == 0
    pick = lambda c_ref, l_ref: jnp.where(ctx, c_ref[0], l_ref[0]).astype(bf16)
    oc_ctx = jnp.concatenate([oc_c[s] for s in range(oc_c.shape[0])], axis=1)
    oc = jnp.where(ctx, oc_ctx, oc_l[0]).astype(bf16)
    pa = jnp.dot(pick(oa_c, oa_l), wb_ref[0], preferred_element_type=f32)
    pb = jnp.dot(pick(ob_c, ob_l), wb_ref[1], preferred_element_type=f32)
    pc = lax.dot_general(oc, wb_ref[2], (((0,), (0,)), ((), ())), preferred_element_type=f32)
    pd = jnp.dot(pick(od_c, od_l), wb_ref[3], preferred_element_type=f32)
    acc = jnp.zeros_like(x)
    for nbr, proj in enumerate((pa, pb, pc, pd)):
        gate = _sigmoid(jnp.dot(h, wg_ref[:, nbr * d:(nbr + 1) * d], preferred_element_type=f32))
        acc = acc + gate * proj
    xo_ref[0] = x + mod[:, 2 * d:3 * d] * _dot(acc, wo_ref[...])


def _merge(x, mod, nw, branches, wg, wb, wo, layer, tm):
    G, T, d = x.shape
    (oa_c, oa_l), (ob_c, ob_l), (oc_c, oc_l), (od_c, od_l) = branches
    lc = oc_c.shape[2]
    tok = lambda w: pl.BlockSpec((1, tm, w), lambda g, i: (g, i, 0))
    ctx_i = lambda g, i: jnp.where(g == 0, i, 0)
    lat_g = lambda g: jnp.maximum(g - 1, 0)
    lat_i = lambda g, i: jnp.where(g == 0, 0, i)
    tok_c = pl.BlockSpec((1, tm, 256), lambda g, i: (0, ctx_i(g, i), 0))
    tok_l = pl.BlockSpec((1, tm, 256), lambda g, i: (lat_g(g), lat_i(g, i), 0))
    lay = lambda a: _layer_spec(a, layer)
    return pl.pallas_call(
        _merge_kernel,
        grid=(G, T // tm),
        in_specs=[tok(d), lay(mod), pl.BlockSpec((1, d), lambda g, i: (0, 0)),
                  tok_c, tok_l, tok_c, tok_l,
                  pl.BlockSpec((tm // lc, HY_W, lc), lambda g, i: (ctx_i(g, i), 0, 0)),
                  pl.BlockSpec((1, HY_W, tm), lambda g, i: (lat_g(g), 0, lat_i(g, i))),
                  tok_c, tok_l, lay(wg), lay(wb), lay(wo)],
        out_specs=tok(d),
        out_shape=jax.ShapeDtypeStruct((G, T, d), f32),
        compiler_params=_cparams(("arbitrary", "arbitrary")),
        name="merge",
    )(x, mod, nw.reshape(1, d), oa_c.reshape(1, T, -1), oa_l, ob_c.reshape(1, T, -1), ob_l, oc_c, oc_l,
      od_c.reshape(1, T, -1), od_l, wg, wb, wo)


def _mlp_kernel(x_ref, mod_ref, nw_ref, w1_ref, w2_ref, fnw_ref, xo_ref, *, final):
    g = pl.program_id(0)
    d = D_MODEL
    x = x_ref[0]
    mod = mod_ref[pl.ds(g, 1), :]
    h = _norm_mod(x, nw_ref[...], mod[:, 4 * d:5 * d], mod[:, 3 * d:4 * d]).astype(bf16)
    acc = jnp.zeros_like(x)
    for c in range(D_FF // d):
        a = jnp.maximum(jnp.dot(h, w1_ref[:, c * d:(c + 1) * d], preferred_element_type=f32), 0.0)
        acc = acc + _dot(a * a, w2_ref[c * d:(c + 1) * d, :])
    y = x + mod[:, 5 * d:6 * d] * acc
    xo_ref[0] = _rms(y, fnw_ref[...]) if final else y


def _mlp(x, mod, nw, w1, w2, fnw, layer, tm, final):
    G, T, d = x.shape
    tok = pl.BlockSpec((1, tm, d), lambda g, i: (g, i, 0))
    vec = pl.BlockSpec((1, d), lambda g, i: (0, 0))
    return pl.pallas_call(
        functools.partial(_mlp_kernel, final=final),
        grid=(G, T // tm),
        in_specs=[tok, _layer_spec(mod, layer), vec, _layer_spec(w1, layer), _layer_spec(w2, layer), vec],
        out_specs=tok,
        out_shape=jax.ShapeDtypeStruct((G, T, d), f32),
        compiler_params=_cparams(("arbitrary", "arbitrary")),
        name="mlp",
    )(x, mod, nw.reshape(1, d), w1, w2, fnw.reshape(1, d))


@functools.lru_cache(maxsize=None)
def _rope_tables(T):
    m = MLA_ROPE // 4
    inv = ROPE_BASE ** (-np.arange(m, dtype=np.float64) / m)
    rows = T // GRID_W
    row_pos = np.repeat(np.arange(rows), GRID_W)[:, None] * inv
    col_pos = np.tile(np.arange(GRID_W), rows)[:, None] * inv
    cos32 = np.concatenate([np.cos(row_pos), np.cos(row_pos), np.cos(col_pos), np.cos(col_pos)], axis=1)
    sin32 = np.concatenate([-np.sin(row_pos), np.sin(row_pos), -np.sin(col_pos), np.sin(col_pos)], axis=1)
    ca = np.zeros((2, T, 128))
    cb = np.zeros((2, T, 128))
    ca[:, :, 0:96] = 1.0
    ca[1, :, 64:96] = cos32
    cb[1, :, 64:96] = sin32
    cd = np.ones((2, T, 256))
    sd = np.zeros((2, T, 256))
    cd[1] = np.tile(cos32, (1, 8))
    sd[1] = np.tile(sin32, (1, 8))
    return tuple(np.asarray(a, np.float32) for a in (ca, cb, cd, sd))


def _swap_perm(width):
    base = np.concatenate([np.arange(8, 16), np.arange(0, 8), np.arange(24, 32), np.arange(16, 24)])
    return np.concatenate([base + 32 * s for s in range(width // 32)])


def _pack_weights(w_in, mla_w_uq, mla_w_ukv, gdn_conv_w, gdn_a_log, gdn_dt_bias, hy_conv_w, hy_conv_b,
                  hy_f_w1, hy_f_b1, hy_f_w2, hy_f_b2, hy_f_w3, hy_f_freq, hy_d):
    depth = w_in.shape[0]
    offs = [0] + [int(s) for s in np.cumsum(IN_SPLITS)]
    seg = lambda i: w_in[:, :, offs[i]:offs[i + 1]]
    zeros = lambda n: jnp.zeros((depth, D_MODEL, n), w_in.dtype)
    kpe = seg(2)
    kpe_sw = kpe[:, :, _swap_perm(MLA_ROPE)]
    wa = jnp.concatenate([seg(0), seg(1), zeros(64), kpe, zeros(32), zeros(64), kpe_sw, zeros(32)], axis=2)
    dq, dk, dv = seg(10), seg(11), seg(12)
    perm = _swap_perm(256)
    dv_slots = jnp.concatenate([dv.reshape(depth, D_MODEL, DA_H, DA_DV),
                                jnp.zeros((depth, D_MODEL, DA_H, LANES - DA_DV), dv.dtype)],
                               axis=3).reshape(depth, D_MODEL, DA_H * LANES)
    wd = jnp.concatenate([dq, dq[:, :, perm], dk, dk[:, :, perm], dv_slots], axis=2)
    wg = jnp.concatenate([seg(3), seg(4), seg(5), seg(6)], axis=2)
    wab = jnp.concatenate([seg(7), seg(8), zeros(128 - 4 * GDN_H)], axis=2)
    whT = jnp.swapaxes(seg(9), 1, 2)
    wgate = seg(13)
    uq = mla_w_uq.reshape(depth, MLA_Q_LORA, MLA_H, MLA_NOPE + MLA_ROPE)
    z32 = jnp.zeros((depth, MLA_Q_LORA, MLA_H, 32), uq.dtype)
    z64 = jnp.zeros((depth, MLA_Q_LORA, MLA_H, 64), uq.dtype)
    rope_sw = uq[..., MLA_NOPE:][..., _swap_perm(MLA_ROPE)]
    wqa = jnp.concatenate([uq, z32], axis=3).reshape(depth, MLA_Q_LORA, MLA_H * 128)
    wqb = jnp.concatenate([z64, rope_sw, z32], axis=3).reshape(depth, MLA_Q_LORA, MLA_H * 128)
    ukv = mla_w_ukv.reshape(depth, MLA_KV_LORA, MLA_H, MLA_NOPE + MLA_V)
    wka = jnp.concatenate([ukv[..., :MLA_NOPE], jnp.zeros((depth, MLA_KV_LORA, MLA_H, 64), ukv.dtype)],
                          axis=3).reshape(depth, MLA_KV_LORA, MLA_H * 128)
    wv = jnp.concatenate([ukv[..., MLA_NOPE:], jnp.zeros((depth, MLA_KV_LORA, MLA_H, LANES - MLA_V), ukv.dtype)],
                         axis=3).reshape(depth, MLA_KV_LORA, MLA_H * LANES)
    cast = lambda a: a.astype(bf16)
    pad128 = lambda a: jnp.pad(a.reshape(depth, 1, -1), ((0, 0), (0, 0), (0, 128 - a.shape[1] * a.shape[2])))
    hy = dict(
        cwT=jnp.swapaxes(hy_conv_w, 1, 2),
        cb=hy_conv_b.reshape(depth, -1, 1),
        d=hy_d.reshape(depth, HY_W, 1),
        w1T=jnp.pad(jnp.swapaxes(hy_f_w1, 1, 2), ((0, 0), (0, 0), (0, LANES - HY_EMB))),
        b1=hy_f_b1.reshape(depth, HY_FH, 1),
        w2T=jnp.swapaxes(hy_f_w2, 1, 2),
        b2=hy_f_b2.reshape(depth, HY_FH, 1),
        w3T=jnp.swapaxes(hy_f_w3, 1, 2),
        freq=hy_f_freq.reshape(depth, HY_FH, 1),
    )
    return dict(wa=cast(wa), wd=cast(wd), wg=cast(wg), wab=cast(wab), whT=cast(whT), wgate=cast(wgate),
                wqa=cast(wqa), wqb=cast(wqb), wka=cast(wka), wv=cast(wv),
                alog=pad128(gdn_a_log), dtb=pad128(gdn_dt_bias), hy=hy)


def kernel(x_prompt, x_sample, cache_mla_ckv, cache_mla_kpe, cache_diff_k, cache_diff_v, state_gdn, c, c_ctx, w_ada, b_ada, norm_mix_w, norm_mlp_w, w_in, mla_q_norm_w, mla_w_uq, mla_kv_norm_w, mla_w_ukv, gdn_conv_w, gdn_a_log, gdn_dt_bias, gdn_norm_w, hy_conv_w, hy_conv_b, hy_f_w1, hy_f_b1, hy_f_w2, hy_f_b2, hy_f_w3, hy_f_freq, hy_d, da_lq1, da_lk1, da_lq2, da_lk2, da_norm_w, w_branch, w_out, mlp_w1, mlp_w2, final_norm_w):
    depth = w_in.shape[0]
    bc, lc, d = x_prompt.shape
    bl, ll, _ = x_sample.shape
    T = ll
    assert bc * lc == T and d == D_MODEL and bc % 2 == 0 and bl == 2
    G = 1 + bl
    past = cache_mla_ckv.shape[2]
    tm = min(512, T)
    tq = min(256, lc)
    tq_lat, kc_lat = min(512, ll), min(1024, ll)

    pk = _pack_weights(w_in, mla_w_uq, mla_w_ukv, gdn_conv_w, gdn_a_log, gdn_dt_bias, hy_conv_w, hy_conv_b,
                       hy_f_w1, hy_f_b1, hy_f_w2, hy_f_b2, hy_f_w3, hy_f_freq, hy_d)
    wb_bf = w_branch.astype(bf16)
    wo_bf = w_out.astype(bf16)
    w1_bf = mlp_w1.astype(bf16)
    w2_bf = mlp_w2.astype(bf16)
    ca, cb, cd, sd = (jnp.asarray(t) for t in _rope_tables(T))

    cond8 = jnp.concatenate([c_ctx.reshape(1, d), c, jnp.zeros((8 - G, d), f32)], axis=0)
    mod = _modulation(cond8, w_ada, b_ada)

    kpe_pad = jnp.pad(cache_mla_kpe, ((0, 0), (0, 0), (0, 0), (MLA_NOPE, 128 - MLA_NOPE - MLA_ROPE)))
    kc_mla, vc_mla = _cache_kv(cache_mla_ckv, kpe_pad, pk["wka"], pk["wv"])
    kc_da = jnp.transpose(cache_diff_k.reshape(bl, depth, past, DA_H, 2 * DA_DK), (1, 3, 0, 2, 4)).astype(bf16)
    vc_da = jnp.transpose(cache_diff_v, (1, 3, 0, 2, 4))
    vc_da = jnp.concatenate([vc_da, jnp.ones(vc_da.shape[:-1] + (1,), f32),
                             jnp.zeros(vc_da.shape[:-1] + (LANES - DA_DV - 1,), f32)], axis=-1).astype(bf16)
    s0_bd = jnp.einsum('bldhij,hg->bldhigj', state_gdn, jnp.eye(GDN_H, dtype=f32)).reshape(
        bl, depth, 2, GDN_N, GDN_N)

    x = jnp.concatenate([x_prompt.reshape(1, T, d), x_sample], axis=0)
    new_ckv, new_kpe, new_dk, new_dv, new_state = [], [], [], [], []
    for l in range(depth):
        lam_init = 0.8 - 0.6 * math.exp(-0.3 * l)
        (zg, zab, zhT, kpe_raw, dk_raw, dv_raw, q, k, v, ckv, dq1, dq2, dk, dv) = _in_proj(
            x, mod, norm_mix_w[l], pk["wa"], pk["wd"], pk["wg"], pk["wab"], pk["whT"], ca, cb, cd, sd,
            mla_q_norm_w[l], mla_kv_norm_w[l], pk["wqa"], pk["wqb"], pk["wka"], pk["wv"], l, tm)
        new_ckv.append(ckv[0].reshape(bc, lc, MLA_KV_LORA))
        new_kpe.append(kpe_raw[0, :, MLA_NOPE:MLA_NOPE + MLA_ROPE].reshape(bc, lc, MLA_ROPE))
        new_dk.append(dk_raw[0].reshape(bc, lc, DA_H, 2, DA_DK))
        new_dv.append(dv_raw[0].reshape(bc, lc, DA_H, LANES)[..., :DA_DV])

        ctx_view = lambda a: a.reshape(a.shape[0], G * bc, lc, a.shape[3])
        oa_c = _mla_attention(ctx_view(q), ctx_view(k), ctx_view(v), 0, bc, lc, tq, lc)
        oa_l = _mla_attention(q, k, v, 1, bl, ll, tq_lat, kc_lat, cache=(kc_mla[l], vc_mla[l]))

        lamv = jnp.concatenate([da_lq1[l][None], da_lk1[l][None], da_lq2[l][None], da_lk2[l][None],
                                jnp.full((1, DA_DK), lam_init, f32), jnp.zeros((3, DA_DK), f32)], axis=0)
        od_c = _diff_attention(ctx_view(dq1), ctx_view(dq2), ctx_view(dk), ctx_view(dv), lamv, da_norm_w[l],
                               0, bc, lc, tq, lc)
        od_l = _diff_attention(dq1, dq2, dk, dv, lamv, da_norm_w[l], 1, bl, ll, tq_lat, kc_lat,
                               cache=(kc_da[l], vc_da[l]))

        zg_ctx = zg.reshape(G * bc, lc, zg.shape[2])
        zab_ctx = zab.reshape(G * bc, lc, 128)
        ob_c, s_gdn = _gdn(zg_ctx, zab_ctx, gdn_conv_w[l], pk["alog"][l], pk["dtb"][l], gdn_norm_w[l], 0, bc, lc,
                           2)
        ob_l = _gdn(zg, zab, gdn_conv_w[l], pk["alog"][l], pk["dtb"][l], gdn_norm_w[l], 1, bl, ll, 1,
                    s0=s0_bd, layer=l)[0]
        new_state.append(s_gdn)

        hw = {name: val[l] for name, val in pk["hy"].items()}
        oc_c = _hyena(zhT, lambda p: 0, lambda p: 0, lambda p: 2 * p, lambda p: 2 * p + 1, bc // 2, lc, 128, hw)
        oc_l = _hyena(zhT, lambda p: 1, lambda p: 2, lambda p: 0, lambda p: 0, 1, ll, 64, hw)

        x = _merge(x, mod, norm_mix_w[l], ((oa_c, oa_l), (ob_c, ob_l), (oc_c, oc_l), (od_c, od_l)),
                   pk["wgate"], wb_bf, wo_bf, l, tm)
        x = _mlp(x, mod, norm_mlp_w[l], w1_bf, w2_bf, final_norm_w, l, tm, final=l == depth - 1)

    y = x
    y_prompt = y[0].reshape(bc, lc, d)
    y_sample = y[1:]
    return (y_prompt, y_sample, jnp.stack(new_ckv, axis=1), jnp.stack(new_kpe, axis=1), jnp.stack(new_dk, axis=1),
            jnp.stack(new_dv, axis=1), jnp.stack(new_state, axis=1))
```

```python
import functools
import math

import numpy as np
import jax
import jax.numpy as jnp
from jax import lax
from jax.experimental import pallas as pl
from jax.experimental.pallas import tpu as pltpu

f32 = jnp.float32
bf16 = jnp.bfloat16

D_MODEL = 1024
GRID_W = 64
N_BRANCH = 4
MLA_H = 4
MLA_NOPE = 64
MLA_ROPE = 32
MLA_V = 64
MLA_Q_LORA = 256
MLA_KV_LORA = 128
GDN_H = 4
GDN_DK = 64
GDN_DV = 64
GDN_CHUNK = 64
HY_W = 256
HY_BANDS = 16
HY_EMB = 1 + 2 * HY_BANDS
HY_FH = 64
HY_SLOW_DECAY = math.log(1e-2) / 1.5
HY_FAST_DECAY = math.log(1e-2) / 0.3
DA_H = 4
DA_DK = 32
DA_DV = 64
D_FF = 4 * D_MODEL
ROPE_BASE = 10000.0
EPS = 1e-6
IN_SPLITS = (MLA_Q_LORA, MLA_KV_LORA, MLA_ROPE,
             GDN_H * GDN_DK, GDN_H * GDN_DK, GDN_H * GDN_DV, GDN_H * GDN_DV, 2 * GDN_H, 2 * GDN_H,
             3 * HY_W,
             DA_H * 2 * DA_DK, DA_H * 2 * DA_DK, DA_H * DA_DV,
             N_BRANCH * D_MODEL)

LOG2E = math.log2(math.e)
LANES = 128
V_ONE = 64
DFT_N2 = 256
VMEM_LIMIT = 56 * 1024 * 1024


def _cparams(sem):
    return pltpu.CompilerParams(dimension_semantics=sem, vmem_limit_bytes=VMEM_LIMIT)


def _dot(a, b):
    return jnp.dot(a.astype(bf16), b.astype(bf16), preferred_element_type=f32)


def _dot_nt(a, b):
    return lax.dot_general(a.astype(bf16), b.astype(bf16), (((1,), (1,)), ((), ())), preferred_element_type=f32)


def _dot_tn(a, b):
    return lax.dot_general(a.astype(bf16), b.astype(bf16), (((0,), (0,)), ((), ())), preferred_element_type=f32)


def _split(x):
    hi = x.astype(bf16)
    lo = (x - hi.astype(f32)).astype(bf16)
    return hi, lo


def _dot3(a, b):
    ah, al = _split(a)
    bh, bl = _split(b)
    return (jnp.dot(ah, bh, preferred_element_type=f32) + jnp.dot(ah, bl, preferred_element_type=f32)
            + jnp.dot(al, bh, preferred_element_type=f32))


def _dot3_w(a, bh, bl):
    ah, al = _split(a)
    return (jnp.dot(ah, bh, preferred_element_type=f32) + jnp.dot(ah, bl, preferred_element_type=f32)
            + jnp.dot(al, bh, preferred_element_type=f32))


def _sigmoid(x):
    return 1.0 / (1.0 + jnp.exp(-x))


def _silu(x):
    return x * _sigmoid(x)


def _softplus(x):
    return jnp.maximum(x, 0.0) + jnp.log(1.0 + jnp.exp(-jnp.abs(x)))


def _rms(x, w):
    return x * lax.rsqrt(jnp.mean(x * x, axis=-1, keepdims=True) + EPS) * w


def _mod_kernel(c_ref, w_ref, b_ref, o_ref):
    c = _silu(c_ref[...])
    o_ref[0] = _dot3(c, w_ref[0]) + b_ref[0]


def _modulation(cond8, w_ada, b_ada):
    depth, d, n6 = w_ada.shape
    tn = 1536
    return pl.pallas_call(
        _mod_kernel,
        grid=(depth, n6 // tn),
        in_specs=[pl.BlockSpec((8, d), lambda l, j: (0, 0)),
                  pl.BlockSpec((1, d, tn), lambda l, j: (l, 0, j)),
                  pl.BlockSpec((1, 1, tn), lambda l, j: (l, 0, j))],
        out_specs=pl.BlockSpec((1, 8, tn), lambda l, j: (l, 0, j)),
        out_shape=jax.ShapeDtypeStruct((depth, 8, n6), f32),
        compiler_params=_cparams(("arbitrary", "arbitrary")),
        name="modulation",
    )(cond8, w_ada, b_ada.reshape(depth, 1, n6))


def _norm_mod(x, nw, scale, shift):
    return _rms(x, nw) * (1.0 + scale) + shift


def _in_kernel(x_ref, mod_ref, nw_ref, wa_ref, wd_ref, wg_ref, wab_ref, whT_ref,
               ca_ref, cb_ref, cd_ref, sd_ref, qnw_ref, kvnw_ref, wqa_ref, wqb_ref, wka_ref, wv_ref,
               zg_ref, zab_ref, zhT_ref, kpe_ref, rdk_ref, rdv_ref,
               q_ref, k_ref, v_ref, ckv_ref, dq1_ref, dq2_ref, dk_ref, dv_ref):
    g = pl.program_id(0)
    d = D_MODEL
    mod = mod_ref[pl.ds(g, 1), :]
    h = _norm_mod(x_ref[0], nw_ref[...], mod[:, d:2 * d], mod[:, 0:d]).astype(bf16)
    zg_ref[0] = jnp.dot(h, wg_ref[...], preferred_element_type=f32)
    zab_ref[0] = jnp.dot(h, wab_ref[...], preferred_element_type=f32)
    zhT_ref[0] = lax.dot_general(whT_ref[...], h, (((1,), (1,)), ((), ())), preferred_element_type=f32)
    za = jnp.dot(h, wa_ref[...], preferred_element_type=f32)
    zd = jnp.dot(h, wd_ref[...], preferred_element_type=f32)
    kpe_ref[0] = za[:, 384:512]
    rdk_ref[0] = zd[:, 512:768]
    rdv_ref[0] = zd[:, 1024:1024 + DA_H * LANES]
    _attn_operands(za, zd, ca_ref, cb_ref, cd_ref, sd_ref, qnw_ref, kvnw_ref, wqa_ref, wqb_ref, wka_ref, wv_ref,
                   q_ref, k_ref, v_ref, ckv_ref, dq1_ref, dq2_ref, dk_ref, dv_ref)


def _layer_spec(arr, layer):
    nd = arr.ndim - 1
    return pl.BlockSpec((None,) + arr.shape[1:], lambda *_: (layer,) + (0,) * nd)


def _in_proj(x, mod, nw, wa, wd, wg, wab, whT, ca, cb, cd, sd, qnw, kvnw, wqa, wqb, wka, wv, layer, tm):
    G, T, d = x.shape
    ng, nab, nh = wg.shape[2], wab.shape[2], whT.shape[1]
    lay = lambda a: _layer_spec(a, layer)
    full = lambda shape: pl.BlockSpec(shape, lambda g, i: (0,) * len(shape))
    tok = lambda w: pl.BlockSpec((1, tm, w), lambda g, i: (g, i, 0))
    tab = lambda w: pl.BlockSpec((1, tm, w), lambda g, i: (jnp.minimum(g, 1), i, 0))
    hm = lambda w: pl.BlockSpec((4, 1, tm, w), lambda g, i: (0, g, i, 0))
    hms = lambda w: jax.ShapeDtypeStruct((4, G, T, w), bf16)
    toks = lambda w: jax.ShapeDtypeStruct((G, T, w), f32)
    return pl.pallas_call(
        _in_kernel,
        grid=(G, T // tm),
        in_specs=[tok(d), lay(mod), full((1, d)), lay(wa), lay(wd), lay(wg), lay(wab), lay(whT),
                  tab(128), tab(128), tab(256), tab(256), full((1, 256)), full((1, 128)),
                  lay(wqa), lay(wqb), lay(wka), lay(wv)],
        out_specs=[tok(ng), tok(nab), pl.BlockSpec((1, nh, tm), lambda g, i: (g, 0, i)),
                   tok(128), tok(256), tok(DA_H * LANES),
                   hm(128), hm(128), hm(128), tok(128), hm(64), hm(64), hm(64), hm(128)],
        out_shape=[toks(ng), toks(nab), jax.ShapeDtypeStruct((G, nh, T), f32),
                   toks(128), toks(256), toks(DA_H * LANES),
                   hms(128), hms(128), hms(128), toks(128), hms(64), hms(64), hms(64), hms(128)],
        compiler_params=_cparams(("arbitrary", "arbitrary")),
        name="in_proj",
    )(x, mod, nw.reshape(1, d), wa, wd, wg, wab, whT, ca, cb, cd, sd, qnw.reshape(1, -1), kvnw.reshape(1, -1),
      wqa, wqb, wka, wv)


def _ones_col(rows):
    return jnp.where(lax.broadcasted_iota(jnp.int32, (rows, LANES), 1) == V_ONE, 1.0, 0.0)


def _attn_operands(za, zd, ca_ref, cb_ref, cd_ref, sd_ref, qnw_ref, kvnw_ref, wqa_ref, wqb_ref, wka_ref, wv_ref,
                   q_ref, k_ref, v_ref, ckv_ref, dq1_ref, dq2_ref, dk_ref, dv_ref):
    cqn = _rms(za[:, 0:256], qnw_ref[...]).astype(bf16)
    ckv = _rms(za[:, 256:384], kvnw_ref[...])
    ckv_ref[0] = ckv
    ckvb = ckv.astype(bf16)
    ca = ca_ref[0]
    cb = cb_ref[0]
    qa = jnp.dot(cqn, wqa_ref[...], preferred_element_type=f32)
    qb = jnp.dot(cqn, wqb_ref[...], preferred_element_type=f32)
    kn = jnp.dot(ckvb, wka_ref[...], preferred_element_type=f32)
    vv = jnp.dot(ckvb, wv_ref[...], preferred_element_type=f32)
    krope = za[:, 384:512] * ca + za[:, 512:640] * cb
    qs = (MLA_NOPE + MLA_ROPE) ** -0.5 * LOG2E
    ones_col = _ones_col(za.shape[0])
    for h in range(MLA_H):
        sl = slice(128 * h, 128 * (h + 1))
        q_ref[h, 0] = ((qa[:, sl] * ca + qb[:, sl] * cb) * qs).astype(bf16)
        k_ref[h, 0] = (kn[:, sl] + krope).astype(bf16)
        v_ref[h, 0] = (vv[:, sl] + ones_col).astype(bf16)
    cd = cd_ref[0]
    sd = sd_ref[0]
    dqs = DA_DK ** -0.5 * LOG2E
    dq = (zd[:, 0:256] * cd + zd[:, 256:512] * sd) * dqs
    dk = zd[:, 512:768] * cd + zd[:, 768:1024] * sd
    first = (lax.broadcasted_iota(jnp.int32, dq.shape, 1) & (2 * DA_DK - 1)) < DA_DK
    dq1 = jnp.where(first, dq, 0.0)
    dq2 = jnp.where(first, 0.0, dq)
    for h in range(DA_H):
        sl = slice(64 * h, 64 * (h + 1))
        dq1_ref[h, 0] = dq1[:, sl].astype(bf16)
        dq2_ref[h, 0] = dq2[:, sl].astype(bf16)
        dk_ref[h, 0] = dk[:, sl].astype(bf16)
        dv_ref[h, 0] = (zd[:, 1024 + 128 * h:1024 + 128 * (h + 1)] + ones_col).astype(bf16)


def _cache_kv_kernel(ckv_ref, kpe_ref, wka_ref, wv_ref, k_ref, v_ref):
    ckvb = ckv_ref[0, 0].astype(bf16)
    kn = jnp.dot(ckvb, wka_ref[0], preferred_element_type=f32)
    vv = jnp.dot(ckvb, wv_ref[0], preferred_element_type=f32)
    kpe = kpe_ref[0, 0]
    ones_col = _ones_col(kpe.shape[0])
    for h in range(MLA_H):
        sl = slice(128 * h, 128 * (h + 1))
        k_ref[0, h, 0] = (kn[:, sl] + kpe).astype(bf16)
        v_ref[0, h, 0] = (vv[:, sl] + ones_col).astype(bf16)


def _cache_kv(cache_ckv, cache_kpe_pad, wka, wv):
    bl, depth, p, _ = cache_ckv.shape
    return pl.pallas_call(
        _cache_kv_kernel,
        grid=(depth, bl),
        in_specs=[pl.BlockSpec((1, 1, p, 128), lambda l, b: (b, l, 0, 0)),
                  pl.BlockSpec((1, 1, p, 128), lambda l, b: (b, l, 0, 0)),
                  pl.BlockSpec((1,) + wka.shape[1:], lambda l, b: (l, 0, 0)),
                  pl.BlockSpec((1,) + wv.shape[1:], lambda l, b: (l, 0, 0))],
        out_specs=[pl.BlockSpec((1, 4, 1, p, 128), lambda l, b: (l, 0, b, 0, 0)),
                   pl.BlockSpec((1, 4, 1, p, 128), lambda l, b: (l, 0, b, 0, 0))],
        out_shape=[jax.ShapeDtypeStruct((depth, 4, bl, p, 128), bf16),
                   jax.ShapeDtypeStruct((depth, 4, bl, p, 128), bf16)],
        compiler_params=_cparams(("arbitrary", "arbitrary")),
        name="cache_kv",
    )(cache_ckv, cache_kpe_pad, wka, wv)


ATT_RB = 64


def _attn_scratch(nchains, tq, kc):
    return [pltpu.VMEM((nchains, tq, 1), f32), pltpu.VMEM((nchains, tq, 1), f32),
            pltpu.VMEM((nchains, tq, LANES), f32), pltpu.VMEM((nchains, tq, kc), f32),
            pltpu.VMEM((nchains, tq, kc), bf16)]


def _attn_scan(q_refs, k_ref, v_ref, cache_refs, scratch, kc):
    m_sc, al_sc, acc_sc, s_sc, p_sc = scratch
    nh = k_ref.shape[0]
    lk = k_ref.shape[2]
    m_sc[...] = jnp.full(m_sc.shape, -jnp.inf, f32)
    acc_sc[...] = jnp.zeros(acc_sc.shape, f32)

    tq = m_sc.shape[1]
    chains = [(j * nh + h, q_ref, h) for j, q_ref in enumerate(q_refs) for h in range(nh)]

    def step(get_k, get_v, kw):
        for c, q_ref, h in chains:
            s_sc[c, :, 0:kw] = lax.dot_general(q_ref[h, 0], get_k(h), (((1,), (1,)), ((), ())),
                                               preferred_element_type=f32)
        for c, _, _ in chains:
            for rb in range(tq // ATT_RB):
                rows = slice(rb * ATT_RB, (rb + 1) * ATT_RB)
                s = s_sc[c, rows, 0:kw]
                m_old = m_sc[c, rows, :]
                m_new = jnp.maximum(m_old, jnp.max(s, axis=1, keepdims=True))
                p_sc[c, rows, 0:kw] = jnp.exp2(s - m_new).astype(bf16)
                al_sc[c, rows, :] = jnp.exp2(m_old - m_new)
                m_sc[c, rows, :] = m_new
        for c, _, h in chains:
            acc_sc[c] = al_sc[c] * acc_sc[c] + jnp.dot(p_sc[c, :, 0:kw], get_v(h), preferred_element_type=f32)

    def body(i, carry):
        rows = pl.ds(pl.multiple_of(i * kc, kc), kc)
        step(lambda h: k_ref[h, 0, rows, :], lambda h: v_ref[h, 0, rows, :], kc)
        return carry

    lax.fori_loop(0, lk // kc, body, 0)
    if cache_refs is not None:
        kc_ref, vc_ref = cache_refs
        step(lambda h: kc_ref[h, 0], lambda h: vc_ref[h, 0], kc_ref.shape[2])


def _attn_out(acc):
    return acc[:, 0:V_ONE] / acc[:, V_ONE:V_ONE + 1]


def _mla_attn_kernel(*refs, kc, has_cache):
    if has_cache:
        q_ref, k_ref, v_ref, kc_ref, vc_ref, o_ref, *scratch = refs
        cache_refs = (kc_ref, vc_ref)
    else:
        q_ref, k_ref, v_ref, o_ref, *scratch = refs
        cache_refs = None
    _attn_scan((q_ref,), k_ref, v_ref, cache_refs, scratch, kc)
    acc_sc = scratch[2]
    for h in range(MLA_H):
        o_ref[0, :, MLA_V * h:MLA_V * (h + 1)] = _attn_out(acc_sc[h])


def _mla_attention(q, k, v, seq_off, nseq, L, tq, kc, cache=None):
    H = q.shape[0]
    kv = lambda rows, off: pl.BlockSpec((H, 1, rows, 128), lambda b, i: (0, b + off, 0, 0))
    in_specs = [pl.BlockSpec((H, 1, tq, 128), lambda b, i: (0, b + seq_off, i, 0)), kv(L, seq_off), kv(L, seq_off)]
    args = [q, k, v]
    if cache is not None:
        in_specs += [kv(cache[0].shape[2], 0), kv(cache[0].shape[2], 0)]
        args += list(cache)
    return pl.pallas_call(
        functools.partial(_mla_attn_kernel, kc=kc, has_cache=cache is not None),
        grid=(nseq, L // tq),
        in_specs=in_specs,
        out_specs=pl.BlockSpec((1, tq, H * MLA_V), lambda b, i: (b, i, 0)),
        out_shape=jax.ShapeDtypeStruct((nseq, L, H * MLA_V), f32),
        scratch_shapes=_attn_scratch(H, tq, kc),
        compiler_params=_cparams(("arbitrary", "arbitrary")),
        name="mla_attn",
    )(*args)


def _diff_attn_kernel(*refs, kc, has_cache):
    if has_cache:
        q1_ref, q2_ref, k_ref, v_ref, kc_ref, vc_ref, lam_ref, nw_ref, o_ref, *scratch = refs
        cache_refs = (kc_ref, vc_ref)
    else:
        q1_ref, q2_ref, k_ref, v_ref, lam_ref, nw_ref, o_ref, *scratch = refs
        cache_refs = None
    acc_sc = scratch[2]
    lamv = lam_ref[...]
    lam_init = lamv[4:5, 0:1]
    lam = (jnp.exp(jnp.sum(lamv[0:1] * lamv[1:2], axis=1, keepdims=True))
           - jnp.exp(jnp.sum(lamv[2:3] * lamv[3:4], axis=1, keepdims=True)) + lam_init)
    _attn_scan((q1_ref, q2_ref), k_ref, v_ref, cache_refs, scratch, kc)
    for h in range(DA_H):
        o = _attn_out(acc_sc[h]) - lam * _attn_out(acc_sc[DA_H + h])
        o_ref[0, :, DA_DV * h:DA_DV * (h + 1)] = _rms(o, nw_ref[...]) * (1.0 - lam_init)


def _diff_attention(q1, q2, k, v, lamv, nw, seq_off, nseq, L, tq, kc, cache=None):
    H = q1.shape[0]
    hm = lambda rows, w, off: pl.BlockSpec((H, 1, rows, w), lambda b, i: (0, b + off, 0, 0))
    qs = pl.BlockSpec((H, 1, tq, 64), lambda b, i: (0, b + seq_off, i, 0))
    in_specs = [qs, qs, hm(L, 64, seq_off), hm(L, 128, seq_off)]
    args = [q1, q2, k, v]
    if cache is not None:
        p = cache[0].shape[2]
        in_specs += [hm(p, 64, 0), hm(p, 128, 0)]
        args += list(cache)
    in_specs += [pl.BlockSpec((8, DA_DK), lambda b, i: (0, 0)), pl.BlockSpec((1, DA_DV), lambda b, i: (0, 0))]
    args += [lamv, nw.reshape(1, DA_DV)]
    return pl.pallas_call(
        functools.partial(_diff_attn_kernel, kc=kc, has_cache=cache is not None),
        grid=(nseq, L // tq),
        in_specs=in_specs,
        out_specs=pl.BlockSpec((1, tq, H * DA_DV), lambda b, i: (b, i, 0)),
        out_shape=jax.ShapeDtypeStruct((nseq, L, H * DA_DV), f32),
        scratch_shapes=_attn_scratch(2 * H, tq, kc),
        compiler_params=_cparams(("arbitrary", "arbitrary")),
        name="diff_attn",
    )(*args)


GDN_N = GDN_H * GDN_CHUNK
GDN_INV_BASE = 8
(M_BD, M_EYE, M_BASE, M_OFF8, M_OFF16, M_OFF32, M_DIR) = range(7)
GDN_NMASK = M_DIR + 6


def _gdn_fill_masks(msk):
    n, c = GDN_N, GDN_CHUNK
    rr = lax.broadcasted_iota(jnp.int32, (n, n), 0)
    cc = lax.broadcasted_iota(jnp.int32, (n, n), 1)
    blk = lambda x, s: lax.shift_right_logical(x, int(math.log2(s)))
    bd = blk(rr, c) == blk(cc, c)
    ri = rr & (c - 1)
    cj = cc & (c - 1)

    def put(i, cond):
        msk[i] = jnp.where(cond, 1.0, 0.0)

    put(M_BD, bd)
    put(M_EYE, rr == cc)
    put(M_BASE, blk(rr, GDN_INV_BASE) == blk(cc, GDN_INV_BASE))
    for slot, s in ((M_OFF8, 8), (M_OFF16, 16), (M_OFF32, 32)):
        put(slot, (blk(rr, 2 * s) == blk(cc, 2 * s)) & (blk(rr, s) != blk(cc, s)))
    put(M_DIR + 0, bd & (ri >= cj))
    put(M_DIR + 1, bd & (ri > cj))
    put(M_DIR + 2, bd & (ri <= cj))
    put(M_DIR + 3, bd & (ri <= cj))
    put(M_DIR + 4, bd & (ri < cj))
    put(M_DIR + 5, bd & (ri >= cj))


def _gdn_conv_chunk(x_ref, s, w, r, L):
    c = GDN_CHUNK
    x = x_ref[s, pl.ds(r, c), :]
    prev8 = x_ref[s, pl.ds(pl.multiple_of(jnp.maximum(r - 8, 0), 8), 8), :]
    next8 = x_ref[s, pl.ds(pl.multiple_of(jnp.minimum(r + c, L - 8), 8), 8), :]
    prev = jnp.where(r > 0, prev8[7:8, :], 0.0)
    nxt = jnp.where(r + c < L, next8[0:1, :], 0.0)
    row = lax.broadcasted_iota(jnp.int32, x.shape, 0)
    xp = jnp.where(row == 0, prev, pltpu.roll(x, 1, 0))
    xn = jnp.where(row == c - 1, nxt, pltpu.roll(x, c - 1, 0))
    y = xp * w[0:1] + x * w[1:2] + xn * w[2:3]
    return _silu(y)


def _group_sum(y, bdb):
    hi, lo = _split(y)
    return jnp.dot(hi, bdb, preferred_element_type=f32) + jnp.dot(lo, bdb, preferred_element_type=f32)


(B_K, B_Q, B_KB, B_RU, B_E, B_TRI, B_T, B_P, B_X, B_U, B_W, B_IN, B_VN) = range(13)
GDN_NBUF = 13
GDN_DOUBLE_BUFFER_MAX = 2 * 1024 * 1024


def _gdn_chunk_chains(xs, gabs, alog, dtb, S, pools, msk):
    n, c = GDN_N, GDN_CHUNK
    bdb = msk[M_BD].astype(bf16)
    small = tuple({} for _ in xs)

    def stage(fn):
        for ch in range(len(xs)):
            fn(ch, pools[ch], small[ch])

    def prep(ch, B, sm):
        d = ch % 2
        xq, xk, xv = xs[ch]
        bd = msk[M_BD]
        q = xq * lax.rsqrt(_group_sum(xq * xq, bdb) + EPS) * (GDN_DK ** -0.5)
        k = xk * lax.rsqrt(_group_sum(xk * xk, bdb) + EPS)
        gab = gabs[ch]
        lane = lax.broadcasted_iota(jnp.int32, gab.shape, 1)
        gfull = -jnp.exp(alog) * _softplus(gab + dtb)
        bfull = _sigmoid(gab)

        def stack(arr, base):
            return jnp.concatenate([jnp.sum(jnp.where(lane == base + h, arr, 0.0), axis=1, keepdims=True)
                                    for h in range(GDN_H)], axis=0)

        g_stack = stack(gfull, d * GDN_H)
        b_stack = stack(bfull, (2 + d) * GDN_H)
        g_b = jnp.broadcast_to(g_stack, (n, n))
        g_row = jnp.sum(g_b * msk[M_EYE], axis=0, keepdims=True)
        gc_row = jnp.sum(g_b * msk[M_DIR + 3 * d + 2], axis=0, keepdims=True)
        g_rb = jnp.broadcast_to(g_row, (n, n))
        gc_col = jnp.sum(g_rb * msk[M_DIR + 3 * d], axis=1, keepdims=True)
        g_last = jnp.sum(g_rb * bd, axis=1, keepdims=True)
        B[B_E] = jnp.exp(jnp.minimum(gc_col - gc_row, 0.0))
        tile4 = lambda x: jnp.concatenate([x] * GDN_H, axis=0)
        k_bd = tile4(k) * bd
        B[B_K] = k_bd
        B[B_KB] = k_bd * b_stack
        B[B_Q] = tile4(q) * bd
        B[B_RU] = tile4(xv) * (bd * b_stack)
        sm.update(gc_col=gc_col, egc=jnp.exp(gc_col), g_last=g_last)

    def tri(ch, B, sm):
        t = _dot_nt(B[B_KB], B[B_K]) * (B[B_E] * msk[M_DIR + 3 * (ch % 2) + 1])
        B[B_TRI] = t
        nm = -(t * msk[M_BASE])
        B[B_P] = nm
        B[B_T] = msk[M_EYE] + nm

    def intra(ch, B, sm):
        B[B_IN] = _dot_nt(B[B_Q], B[B_K]) * (B[B_E] * msk[M_DIR + 3 * (ch % 2)])

    def square(d, B, sm):
        B[B_P] = _dot(B[B_P], B[B_P])

    def extend(d, B, sm):
        B[B_T] = B[B_T] + _dot(B[B_T], B[B_P])

    stage(prep)
    stage(tri)
    stage(intra)
    for _ in range(2):
        stage(square)
        stage(extend)
    for slot in (M_OFF8, M_OFF16, M_OFF32):
        def cross(d, B, sm, slot=slot):
            B[B_X] = _dot(B[B_T], B[B_TRI] * msk[slot])

        def merge(d, B, sm):
            B[B_T] = B[B_T] - _dot(B[B_X], B[B_T])

        stage(cross)
        stage(merge)

    def solve(d, B, sm):
        uw = _dot(B[B_T], jnp.concatenate([B[B_RU], B[B_KB] * sm["egc"]], axis=1))
        B[B_U] = uw[:, 0:n]
        B[B_W] = uw[:, n:2 * n]

    def v_new(d, B, sm):
        B[B_VN] = B[B_U] - _dot(B[B_W], S[d])

    def out(d, B, sm):
        o_bd = _dot(B[B_Q] * sm["egc"], S[d]) + _dot(B[B_IN], B[B_VN])
        sm["o"] = o_bd[0:c] + o_bd[c:2 * c] + o_bd[2 * c:3 * c] + o_bd[3 * c:4 * c]

    def update(d, B, sm):
        S[d] = (S[d] * jnp.exp(sm["g_last"])
                + _dot_tn(B[B_K] * jnp.exp(sm["g_last"] - sm["gc_col"]), B[B_VN]))

    for fn in (solve, v_new, out, update):
        stage(fn)
    return [sm["o"] for sm in small]


def _gdn_kernel(*refs, has_s0):
    if has_s0:
        (q_ref, k_ref, v_ref, z_ref, cwq_ref, cwk_ref, cwv_ref, gab_ref, alog_ref, dtb_ref, nw_ref, s0_ref,
         o_ref, of, ob, msk, S, *pools) = refs
        sfin_ref = None
    else:
        (q_ref, k_ref, v_ref, z_ref, cwq_ref, cwk_ref, cwv_ref, gab_ref, alog_ref, dtb_ref, nw_ref,
         o_ref, sfin_ref, of, ob, msk, S, *pools) = refs
    c = GDN_CHUNK
    ns, L = q_ref.shape[0], q_ref.shape[1]
    n = L // c

    @pl.when(pl.program_id(0) == 0)
    def _():
        _gdn_fill_masks(msk)

    alog = alog_ref[...]
    dtb = dtb_ref[...]
    cwq, cwk, cwv = cwq_ref[...], cwk_ref[...], cwv_ref[...]
    if has_s0:
        for s in range(ns):
            S[2 * s:2 * s + 2] = s0_ref[s, 0]
    else:
        S[...] = jnp.zeros(S.shape, f32)

    def body(i, carry):
        starts = (pl.multiple_of(i * c, c), pl.multiple_of((n - 1 - i) * c, c))
        chains = [(s, r) for s in range(ns) for r in starts]
        xs = [(_gdn_conv_chunk(q_ref, s, cwq, r, L), _gdn_conv_chunk(k_ref, s, cwk, r, L),
               _gdn_conv_chunk(v_ref, s, cwv, r, L)) for s, r in chains]
        gabs = [gab_ref[s, pl.ds(r, c), :] for s, r in chains]
        outs = _gdn_chunk_chains(xs, gabs, alog, dtb, S, pools, msk)
        for ch, (s, r) in enumerate(chains):
            (of, ob)[ch % 2][s, pl.ds(r, c), :] = outs[ch]
        return carry

    lax.fori_loop(0, n, body, 0, unroll=2)
    if sfin_ref is not None:
        for s in range(ns):
            for direction in range(2):
                for h in range(GDN_H):
                    sfin_ref[s, direction, h] = S[2 * s + direction, c * h:c * (h + 1), c * h:c * (h + 1)]

    bdb = msk[M_BD].astype(bf16)
    nw = nw_ref[...]

    def norm_gate(i, carry):
        rows = pl.ds(pl.multiple_of(i * c, c), c)
        for s in range(ns):
            o = of[s, rows, :] + ob[s, rows, :]
            ms = _group_sum(o * o, bdb) * (1.0 / GDN_DV)
            o_ref[s, rows, :] = o * lax.rsqrt(ms + EPS) * nw * _silu(z_ref[s, rows, :])
        return carry

    lax.fori_loop(0, n, norm_gate, 0)


def _gdn(zg, zab, cw, alog_row, dtb_row, nw, seq_off, nseq, L, ns, s0=None, layer=0):
    H = GDN_H
    w = H * GDN_DK
    assert seq_off % ns == 0 and nseq % ns == 0
    off = seq_off // ns
    mode = dict(pipeline_mode=pl.Buffered(1)) if ns * L * w * 4 > GDN_DOUBLE_BUFFER_MAX else {}
    tok = lambda k: pl.BlockSpec((ns, L, w), lambda s, k=k: (s + off, 0, k), **mode)
    cws = lambda k: pl.BlockSpec((3, w), lambda s, k=k: (0, k))
    full = lambda shape: pl.BlockSpec(shape, lambda s: (0,) * len(shape))
    in_specs = [tok(0), tok(1), tok(2), tok(3), cws(0), cws(1), cws(2),
                pl.BlockSpec((ns, L, 128), lambda s: (s + off, 0, 0)),
                full((1, 128)), full((1, 128)), full((1, w))]
    args = [zg, zg, zg, zg, cw, cw, cw, zab, alog_row, dtb_row, jnp.tile(nw.reshape(1, GDN_DV), (1, H))]
    out_specs = [pl.BlockSpec((ns, L, w), lambda s: (s, 0, 0))]
    out_shape = [jax.ShapeDtypeStruct((nseq, L, w), f32)]
    if s0 is not None:
        in_specs.append(pl.BlockSpec((ns, 1, 2, GDN_N, GDN_N), lambda s: (s, layer, 0, 0, 0)))
        args.append(s0)
    else:
        out_specs.append(pl.BlockSpec((ns, 2, H, GDN_DK, GDN_DV), lambda s: (s, 0, 0, 0, 0)))
        out_shape.append(jax.ShapeDtypeStruct((nseq, 2, H, GDN_DK, GDN_DV), f32))
    pool = pltpu.VMEM((GDN_NBUF, GDN_N, GDN_N), f32)
    return pl.pallas_call(
        functools.partial(_gdn_kernel, has_s0=s0 is not None),
        grid=(nseq // ns,),
        in_specs=in_specs,
        out_specs=out_specs,
        out_shape=out_shape,
        scratch_shapes=[pltpu.VMEM((ns, L, w), f32), pltpu.VMEM((ns, L, w), f32),
                        pltpu.VMEM((GDN_NMASK, GDN_N, GDN_N), f32), pltpu.VMEM((2 * ns, GDN_N, GDN_N), f32)]
                       + [pool] * (2 * ns),
        compiler_params=_cparams(("arbitrary",)),
        name="gdn",
    )(*args)


def _bitrev(p, bits):
    r = 0
    for _ in range(bits):
        r = (r << 1) | (p & 1)
        p >>= 1
    return r


@functools.lru_cache(maxsize=None)
def _fft_tables(L):
    n = 2 * L
    n2 = DFT_N2
    n1 = n // n2
    bits = n1.bit_length() - 1
    npair = max(n1 // 2, 1)
    sta = np.zeros((max(bits, 1) * npair, 2 * n2), np.float64)
    stb = np.zeros_like(sta)
    for s in range(bits):
        half = n1 >> (s + 1)
        for p in range(npair):
            j = p % half
            ang = -2.0 * np.pi * j / (2 * half)
            wr, wi = np.cos(ang), np.sin(ang)
            sta[s * npair + p, :] = wr
            stb[s * npair + p, :n2] = -wi
            stb[s * npair + p, n2:] = wi
    twa = np.zeros((n1, 2 * n2), np.float64)
    twb = np.zeros_like(twa)
    lanes = np.arange(n2)
    for p in range(n1):
        ang = -2.0 * np.pi * lanes * _bitrev(p, bits) / n
        twa[p, :n2] = np.cos(ang)
        twa[p, n2:] = np.cos(ang)
        twb[p, :n2] = -np.sin(ang)
        twb[p, n2:] = np.sin(ang)
    kn = np.outer(lanes, lanes) * (-2.0 * np.pi / n2)
    fr, fi = np.cos(kn), np.sin(kn)
    fwd = np.block([[fr, fi], [-fi, fr]])
    inv = np.block([[fr, -fi], [fi, fr]])
    as32 = lambda a: np.asarray(a, np.float32)
    return dict(n1=n1, bits=bits, npair=npair, sta=as32(sta), stb=as32(stb), twa=as32(twa), twb=as32(twb),
                fwd=as32(fwd), inv=as32(inv))


@functools.lru_cache(maxsize=None)
def _hyena_pos_table(L):
    n = 2 * L
    idx = np.arange(n)
    pos = np.where(idx < L, idx, n - idx).astype(np.float64)
    pos[L] = 0.0
    t = pos / max(L - 1, 1)
    bands = np.linspace(1e-4, HY_BANDS - 1, HY_BANDS).astype(np.float32).astype(np.float64)
    ang = (2.0 * math.pi * pos / L)[None, :] * bands[:, None]
    z = np.zeros((LANES, n), np.float64)
    z[0] = t
    z[1:1 + HY_BANDS] = np.cos(ang)
    z[1 + HY_BANDS:1 + 2 * HY_BANDS] = -np.sin(ang)
    deltas = np.abs(np.linspace(HY_SLOW_DECAY, HY_FAST_DECAY, HY_W)).reshape(HY_W, 1)
    return np.asarray(z, np.float32), np.asarray(deltas, np.float32)


def _swap_halves(x):
    n2 = x.shape[1] // 2
    return jnp.concatenate([x[:, n2:], x[:, :n2]], axis=1)


def _fft_forward(X, sta_ref, stb_ref, twa_ref, twb_ref, fh_ref, fl_ref, n1, bits, npair, ct, mrows):
    for s in range(bits):
        half = n1 >> (s + 1)

        def pair(p, carry, s=s, half=half):
            grp = p // half
            j = p - grp * half
            a = grp * 2 * half + j
            ra = pl.ds(pl.multiple_of(a * ct, ct), ct)
            rb = pl.ds(pl.multiple_of((a + half) * ct, ct), ct)
            xa = X[ra, :]
            xb = X[rb, :]
            X[ra, :] = xa + xb
            d = xa - xb
            X[rb, :] = d * sta_ref[pl.ds(s * npair + p, 1), :] + _swap_halves(d) * stb_ref[pl.ds(s * npair + p, 1), :]
            return carry

        lax.fori_loop(0, npair, pair, 0)

    def blk(p, carry):
        r = pl.ds(pl.multiple_of(p * ct, ct), ct)
        y = X[r, :]
        X[r, :] = y * twa_ref[pl.ds(p, 1), :] + _swap_halves(y) * twb_ref[pl.ds(p, 1), :]
        return carry

    lax.fori_loop(0, n1, blk, 0)

    def mm(i, carry):
        r = pl.ds(pl.multiple_of(i * mrows, mrows), mrows)
        X[r, :] = _dot3_w(X[r, :], fh_ref[...], fl_ref[...])
        return carry

    lax.fori_loop(0, n1 * ct // mrows, mm, 0)


def _fft_inverse(X, sta_ref, stb_ref, twa_ref, twb_ref, fh_ref, fl_ref, n1, bits, npair, ct, mrows):
    def mm(i, carry):
        r = pl.ds(pl.multiple_of(i * mrows, mrows), mrows)
        X[r, :] = _dot3_w(X[r, :], fh_ref[...], fl_ref[...])
        return carry

    lax.fori_loop(0, n1 * ct // mrows, mm, 0)

    def blk(p, carry):
        r = pl.ds(pl.multiple_of(p * ct, ct), ct)
        y = X[r, :]
        X[r, :] = y * twa_ref[pl.ds(p, 1), :] - _swap_halves(y) * twb_ref[pl.ds(p, 1), :]
        return carry

    lax.fori_loop(0, n1, blk, 0)

    for s in reversed(range(bits)):
        half = n1 >> (s + 1)

        def pair(p, carry, s=s, half=half):
            grp = p // half
            j = p - grp * half
            a = grp * 2 * half + j
            ra = pl.ds(pl.multiple_of(a * ct, ct), ct)
            rb = pl.ds(pl.multiple_of((a + half) * ct, ct), ct)
            xa = X[ra, :]
            xb = X[rb, :]
            tw = xb * sta_ref[pl.ds(s * npair + p, 1), :] - _swap_halves(xb) * stb_ref[pl.ds(s * npair + p, 1), :]
            X[ra, :] = xa + tw
            X[rb, :] = xa - tw
            return carry

        lax.fori_loop(0, npair, pair, 0)


def _conv3_lanes(x, w, b):
    L = x.shape[1]
    lane = lax.broadcasted_iota(jnp.int32, x.shape, 1)
    xp = jnp.where(lane == 0, 0.0, pltpu.roll(x, 1, 1))
    xn = jnp.where(lane == L - 1, 0.0, pltpu.roll(x, L - 1, 1))
    return xp * w[:, 0:1] + x * w[:, 1:2] + xn * w[:, 2:3] + b


def _hyena_hidden_kernel(zt_ref, w1_ref, b1_ref, w2_ref, b2_ref, fr_ref, h_ref):
    fr = fr_ref[...]
    h = jnp.sin(fr * (_dot3(w1_ref[...], zt_ref[...]) + b1_ref[...]))
    h_ref[...] = jnp.sin(fr * (_dot3(w2_ref[...], h) + b2_ref[...]))


def _hyena_hidden(zt, hw):
    nblk = max(zt.shape[1] // 2048, 1)
    tn = zt.shape[1] // nblk
    full = lambda a: pl.BlockSpec(a.shape, lambda j: (0,) * a.ndim)
    names = ("w1T", "b1", "w2T", "b2", "freq")
    return pl.pallas_call(
        _hyena_hidden_kernel,
        grid=(nblk,),
        in_specs=[pl.BlockSpec((zt.shape[0], tn), lambda j: (0, j))] + [full(hw[k]) for k in names],
        out_specs=pl.BlockSpec((HY_FH, tn), lambda j: (0, j)),
        out_shape=jax.ShapeDtypeStruct((HY_FH, zt.shape[1]), f32),
        compiler_params=_cparams(("arbitrary",)),
        name="hyena_hidden",
    )(zt, *[hw[k] for k in names])


def _hyena_kernel(x0a_ref, x1a_ref, va_ref, x0b_ref, x1b_ref, vb_ref, cw0_ref, cw1_ref, cw2_ref,
                  cb0_ref, cb1_ref, cb2_ref, d_ref, t_ref, dl_ref, h_ref,
                  w3f_ref, w3b_ref, sta_ref, stb_ref, twa_ref, twb_ref,
                  ffh_ref, ffl_ref, fih_ref, fil_ref, o_ref, X, HA, HB, *, L, n1, bits, npair, ct, mrows):
    n2 = DFT_N2
    n = 2 * L
    fft_args = (sta_ref, stb_ref, twa_ref, twb_ref)

    @pl.when(pl.program_id(1) == 0)
    def _():
        h = h_ref[...]
        hf = _dot3(w3f_ref[...], h)
        hb = _dot3(w3b_ref[...], h)
        lane = lax.broadcasted_iota(jnp.int32, hf.shape, 1)
        dec = jnp.exp(-t_ref[0:1, :] * dl_ref[...])
        hc = jnp.where(lane < L, hf, jnp.where(lane > L, hb, 0.0)) * dec
        for b in range(n1):
            X[b * ct:(b + 1) * ct, 0:n2] = hc[:, b * n2:(b + 1) * n2]
            X[b * ct:(b + 1) * ct, n2:2 * n2] = jnp.zeros((ct, n2), f32)
        _fft_forward(X, *fft_args, ffh_ref, ffl_ref, n1, bits, npair, ct, mrows)
        hs = X[...]
        HA[...] = jnp.concatenate([hs[:, :n2], hs[:, :n2]], axis=1)
        HB[...] = jnp.concatenate([-hs[:, n2:], hs[:, n2:]], axis=1)

    x0a = _conv3_lanes(x0a_ref[0], cw0_ref[...], cb0_ref[...])
    x0b = _conv3_lanes(x0b_ref[0], cw0_ref[...], cb0_ref[...])
    vva = _conv3_lanes(va_ref[0], cw2_ref[...], cb2_ref[...]) * _conv3_lanes(x1a_ref[0], cw1_ref[...], cb1_ref[...])
    vvb = _conv3_lanes(vb_ref[0], cw2_ref[...], cb2_ref[...]) * _conv3_lanes(x1b_ref[0], cw1_ref[...], cb1_ref[...])
    nb = L // n2
    for b in range(nb):
        X[b * ct:(b + 1) * ct, 0:n2] = vva[:, b * n2:(b + 1) * n2]
        X[b * ct:(b + 1) * ct, n2:2 * n2] = vvb[:, b * n2:(b + 1) * n2]
    X[nb * ct:n1 * ct, :] = jnp.zeros(((n1 - nb) * ct, 2 * n2), f32)
    _fft_forward(X, *fft_args, ffh_ref, ffl_ref, n1, bits, npair, ct, mrows)

    def spec(i, carry):
        r = pl.ds(pl.multiple_of(i * ct, ct), ct)
        x = X[r, :]
        X[r, :] = x * HA[r, :] + _swap_halves(x) * HB[r, :]
        return carry

    lax.fori_loop(0, n1, spec, 0)
    _fft_inverse(X, *fft_args, fih_ref, fil_ref, n1, bits, npair, ct, mrows)
    inv_n = 1.0 / n
    ya = jnp.concatenate([X[b * ct:(b + 1) * ct, 0:n2] for b in range(nb)], axis=1) * inv_n
    yb = jnp.concatenate([X[b * ct:(b + 1) * ct, n2:2 * n2] for b in range(nb)], axis=1) * inv_n
    dcol = d_ref[...]
    o_ref[0] = (ya + vva * dcol) * x0a
    o_ref[1] = (yb + vvb * dcol) * x0b


def _hyena(zhT, grp_a, grp_b, lane_a, lane_b, npairs, L, ct, hw):
    tabs = _fft_tables(L)
    n1, bits, npair = tabs["n1"], tabs["bits"], tabs["npair"]
    n = 2 * L
    ntile = HY_W // ct
    mrows = min(512, n1 * ct)
    zt, deltas = _hyena_pos_table(L)
    fwd = jnp.asarray(tabs["fwd"])
    inv = jnp.asarray(tabs["inv"])
    ffh = fwd.astype(bf16)
    ffl = (fwd - ffh.astype(f32)).astype(bf16)
    fih = inv.astype(bf16)
    fil = (inv - fih.astype(f32)).astype(bf16)
    xin = lambda k, grp, ln: pl.BlockSpec((1, ct, L), lambda j, p, k=k: (grp(p), j + ntile * k, ln(p)))
    chan = lambda k, w: pl.BlockSpec((ct, w), lambda j, p, k=k: (j + ntile * k, 0))
    full = lambda a: pl.BlockSpec(a.shape, lambda j, p: (0,) * a.ndim)
    hidden = _hyena_hidden(jnp.asarray(zt), hw)
    consts = [jnp.asarray(zt[0:8]), jnp.asarray(deltas)]
    in_specs = ([xin(0, grp_a, lane_a), xin(1, grp_a, lane_a), xin(2, grp_a, lane_a),
                 xin(0, grp_b, lane_b), xin(1, grp_b, lane_b), xin(2, grp_b, lane_b),
                 chan(0, 3), chan(1, 3), chan(2, 3), chan(0, 1), chan(1, 1), chan(2, 1), chan(0, 1),
                 full(consts[0]), chan(0, 1), full(hidden),
                 chan(0, HY_FH), chan(1, HY_FH)]
                + [full(jnp.asarray(tabs[k])) for k in ("sta", "stb", "twa", "twb")]
                + [full(ffh), full(ffl), full(fih), full(fil)])
    args = ([zhT] * 6 + [hw["cwT"]] * 3 + [hw["cb"]] * 3 + [hw["d"], consts[0], consts[1], hidden,
            hw["w3T"], hw["w3T"]]
            + [jnp.asarray(tabs[k]) for k in ("sta", "stb", "twa", "twb")] + [ffh, ffl, fih, fil])
    return pl.pallas_call(
        functools.partial(_hyena_kernel, L=L, n1=n1, bits=bits, npair=npair, ct=ct, mrows=mrows),
        grid=(ntile, npairs),
        in_specs=in_specs,
        out_specs=pl.BlockSpec((2, ct, L), lambda j, p: (p, j, 0)),
        out_shape=jax.ShapeDtypeStruct((2 * npairs, HY_W, L), f32),
        scratch_shapes=[pltpu.VMEM((n1 * ct, 2 * DFT_N2), f32)] * 3,
        compiler_params=_cparams(("arbitrary", "arbitrary")),
        name="hyena",
    )(*args)


def _merge_kernel(x_ref, mod_ref, nw_ref, oa_c, oa_l, ob_c, ob_l, oc_c, oc_l, od_c, od_l, wg_ref, wb_ref, wo_ref,
                  xo_ref):
    g = pl.program_id(0)
    d = D_MODEL
    x = x_ref[0]
    mod = mod_ref[pl.ds(g, 1), :]
    h = _norm_mod(x, nw_ref[...], mod[:, d:2 * d], mod[:, 0:d]).astype(bf16)
    ctx = g == 0
    pick = lambda c_ref, l_ref: jnp.where(ctx, c_ref[0], l_ref[0]).astype(bf16)
    oc_ctx = jnp.concatenate([oc_c[s] for s in range(oc_c.shape[0])], axis=1)
    oc = jnp.where(ctx, oc_ctx, oc_l[0]).astype(bf16)
    pa = jnp.dot(pick(oa_c, oa_l), wb_ref[0], preferred_element_type=f32)
    pb = jnp.dot(pick(ob_c, ob_l), wb_ref[1], preferred_element_type=f32)
    pc = lax.dot_general(oc, wb_ref[2], (((0,), (0,)), ((), ())), preferred_element_type=f32)
    pd = jnp.dot(pick(od_c, od_l), wb_ref[3], preferred_element_type=f32)
    acc = jnp.zeros_like(x)
    for nbr, proj in enumerate((pa, pb, pc, pd)):
        gate = _sigmoid(jnp.dot(h, wg_ref[:, nbr * d:(nbr + 1) * d], preferred_element_type=f32))
        acc = acc + gate * proj
    xo_ref[0] = x + mod[:, 2 * d:3 * d] * _dot(acc, wo_ref[...])


def _merge(x, mod, nw, branches, wg, wb, wo, layer, tm):
    G, T, d = x.shape
    (oa_c, oa_l), (ob_c, ob_l), (oc_c, oc_l), (od_c, od_l) = branches
    lc = oc_c.shape[2]
    tok = lambda w: pl.BlockSpec((1, tm, w), lambda g, i: (g, i, 0))
    ctx_i = lambda g, i: jnp.where(g == 0, i, 0)
    lat_g = lambda g: jnp.maximum(g - 1, 0)
    lat_i = lambda g, i: jnp.where(g == 0, 0, i)
    tok_c = pl.BlockSpec((1, tm, 256), lambda g, i: (0, ctx_i(g, i), 0))
    tok_l = pl.BlockSpec((1, tm, 256), lambda g, i: (lat_g(g), lat_i(g, i), 0))
    lay = lambda a: _layer_spec(a, layer)
    return pl.pallas_call(
        _merge_kernel,
        grid=(G, T // tm),
        in_specs=[tok(d), lay(mod), pl.BlockSpec((1, d), lambda g, i: (0, 0)),
                  tok_c, tok_l, tok_c, tok_l,
                  pl.BlockSpec((tm // lc, HY_W, lc), lambda g, i: (ctx_i(g, i), 0, 0)),
                  pl.BlockSpec((1, HY_W, tm), lambda g, i: (lat_g(g), 0, lat_i(g, i))),
                  tok_c, tok_l, lay(wg), lay(wb), lay(wo)],
        out_specs=tok(d),
        out_shape=jax.ShapeDtypeStruct((G, T, d), f32),
        compiler_params=_cparams(("arbitrary", "arbitrary")),
        name="merge",
    )(x, mod, nw.reshape(1, d), oa_c.reshape(1, T, -1), oa_l, ob_c.reshape(1, T, -1), ob_l, oc_c, oc_l,
      od_c.reshape(1, T, -1), od_l, wg, wb, wo)


def _mlp_kernel(x_ref, mod_ref, nw_ref, w1_ref, w2_ref, fnw_ref, xo_ref, *, final):
    g = pl.program_id(0)
    d = D_MODEL
    x = x_ref[0]
    mod = mod_ref[pl.ds(g, 1), :]
    h = _norm_mod(x, nw_ref[...], mod[:, 4 * d:5 * d], mod[:, 3 * d:4 * d]).astype(bf16)
    acc = jnp.zeros_like(x)
    for c in range(D_FF // d):
        a = jnp.maximum(jnp.dot(h, w1_ref[:, c * d:(c + 1) * d], preferred_element_type=f32), 0.0)
        acc = acc + _dot(a * a, w2_ref[c * d:(c + 1) * d, :])
    y = x + mod[:, 5 * d:6 * d] * acc
    xo_ref[0] = _rms(y, fnw_ref[...]) if final else y


def _mlp(x, mod, nw, w1, w2, fnw, layer, tm, final):
    G, T, d = x.shape
    tok = pl.BlockSpec((1, tm, d), lambda g, i: (g, i, 0))
    vec = pl.BlockSpec((1, d), lambda g, i: (0, 0))
    return pl.pallas_call(
        functools.partial(_mlp_kernel, final=final),
        grid=(G, T // tm),
        in_specs=[tok, _layer_spec(mod, layer), vec, _layer_spec(w1, layer), _layer_spec(w2, layer), vec],
        out_specs=tok,
        out_shape=jax.ShapeDtypeStruct((G, T, d), f32),
        compiler_params=_cparams(("arbitrary", "arbitrary")),
        name="mlp",
    )(x, mod, nw.reshape(1, d), w1, w2, fnw.reshape(1, d))


@functools.lru_cache(maxsize=None)
def _rope_tables(T):
    m = MLA_ROPE // 4
    inv = ROPE_BASE ** (-np.arange(m, dtype=np.float64) / m)
    rows = T // GRID_W
    row_pos = np.repeat(np.arange(rows), GRID_W)[:, None] * inv
    col_pos = np.tile(np.arange(GRID_W), rows)[:, None] * inv
    cos32 = np.concatenate([np.cos(row_pos), np.cos(row_pos), np.cos(col_pos), np.cos(col_pos)], axis=1)
    sin32 = np.concatenate([-np.sin(row_pos), np.sin(row_pos), -np.sin(col_pos), np.sin(col_pos)], axis=1)
    ca = np.zeros((2, T, 128))
    cb = np.zeros((2, T, 128))
    ca[:, :, 0:96] = 1.0
    ca[1, :, 64:96] = cos32
    cb[1, :, 64:96] = sin32
    cd = np.ones((2, T, 256))
    sd = np.zeros((2, T, 256))
    cd[1] = np.tile(cos32, (1, 8))
    sd[1] = np.tile(sin32, (1, 8))
    return tuple(np.asarray(a, np.float32) for a in (ca, cb, cd, sd))


def _swap_perm(width):
    base = np.concatenate([np.arange(8, 16), np.arange(0, 8), np.arange(24, 32), np.arange(16, 24)])
    return np.concatenate([base + 32 * s for s in range(width // 32)])


def _pack_weights(w_in, mla_w_uq, mla_w_ukv, gdn_conv_w, gdn_a_log, gdn_dt_bias, hy_conv_w, hy_conv_b,
                  hy_f_w1, hy_f_b1, hy_f_w2, hy_f_b2, hy_f_w3, hy_f_freq, hy_d):
    depth = w_in.shape[0]
    offs = [0] + [int(s) for s in np.cumsum(IN_SPLITS)]
    seg = lambda i: w_in[:, :, offs[i]:offs[i + 1]]
    zeros = lambda n: jnp.zeros((depth, D_MODEL, n), w_in.dtype)
    kpe = seg(2)
    kpe_sw = kpe[:, :, _swap_perm(MLA_ROPE)]
    wa = jnp.concatenate([seg(0), seg(1), zeros(64), kpe, zeros(32), zeros(64), kpe_sw, zeros(32)], axis=2)
    dq, dk, dv = seg(10), seg(11), seg(12)
    perm = _swap_perm(256)
    dv_slots = jnp.concatenate([dv.reshape(depth, D_MODEL, DA_H, DA_DV),
                                jnp.zeros((depth, D_MODEL, DA_H, LANES - DA_DV), dv.dtype)],
                               axis=3).reshape(depth, D_MODEL, DA_H * LANES)
    wd = jnp.concatenate([dq, dq[:, :, perm], dk, dk[:, :, perm], dv_slots], axis=2)
    wg = jnp.concatenate([seg(3), seg(4), seg(5), seg(6)], axis=2)
    wab = jnp.concatenate([seg(7), seg(8), zeros(128 - 4 * GDN_H)], axis=2)
    whT = jnp.swapaxes(seg(9), 1, 2)
    wgate = seg(13)
    uq = mla_w_uq.reshape(depth, MLA_Q_LORA, MLA_H, MLA_NOPE + MLA_ROPE)
    z32 = jnp.zeros((depth, MLA_Q_LORA, MLA_H, 32), uq.dtype)
    z64 = jnp.zeros((depth, MLA_Q_LORA, MLA_H, 64), uq.dtype)
    rope_sw = uq[..., MLA_NOPE:][..., _swap_perm(MLA_ROPE)]
    wqa = jnp.concatenate([uq, z32], axis=3).reshape(depth, MLA_Q_LORA, MLA_H * 128)
    wqb = jnp.concatenate([z64, rope_sw, z32], axis=3).reshape(depth, MLA_Q_LORA, MLA_H * 128)
    ukv = mla_w_ukv.reshape(depth, MLA_KV_LORA, MLA_H, MLA_NOPE + MLA_V)
    wka = jnp.concatenate([ukv[..., :MLA_NOPE], jnp.zeros((depth, MLA_KV_LORA, MLA_H, 64), ukv.dtype)],
                          axis=3).reshape(depth, MLA_KV_LORA, MLA_H * 128)
    wv = jnp.concatenate([ukv[..., MLA_NOPE:], jnp.zeros((depth, MLA_KV_LORA, MLA_H, LANES - MLA_V), ukv.dtype)],
                         axis=3).reshape(depth, MLA_KV_LORA, MLA_H * LANES)
    cast = lambda a: a.astype(bf16)
    pad128 = lambda a: jnp.pad(a.reshape(depth, 1, -1), ((0, 0), (0, 0), (0, 128 - a.shape[1] * a.shape[2])))
    hy = dict(
        cwT=jnp.swapaxes(hy_conv_w, 1, 2),
        cb=hy_conv_b.reshape(depth, -1, 1),
        d=hy_d.reshape(depth, HY_W, 1),
        w1T=jnp.pad(jnp.swapaxes(hy_f_w1, 1, 2), ((0, 0), (0, 0), (0, LANES - HY_EMB))),
        b1=hy_f_b1.reshape(depth, HY_FH, 1),
        w2T=jnp.swapaxes(hy_f_w2, 1, 2),
        b2=hy_f_b2.reshape(depth, HY_FH, 1),
        w3T=jnp.swapaxes(hy_f_w3, 1, 2),
        freq=hy_f_freq.reshape(depth, HY_FH, 1),
    )
    return dict(wa=cast(wa), wd=cast(wd), wg=cast(wg), wab=cast(wab), whT=cast(whT), wgate=cast(wgate),
                wqa=cast(wqa), wqb=cast(wqb), wka=cast(wka), wv=cast(wv),
                alog=pad128(gdn_a_log), dtb=pad128(gdn_dt_bias), hy=hy)


def kernel(x_prompt, x_sample, cache_mla_ckv, cache_mla_kpe, cache_diff_k, cache_diff_v, state_gdn, c, c_ctx, w_ada, b_ada, norm_mix_w, norm_mlp_w, w_in, mla_q_norm_w, mla_w_uq, mla_kv_norm_w, mla_w_ukv, gdn_conv_w, gdn_a_log, gdn_dt_bias, gdn_norm_w, hy_conv_w, hy_conv_b, hy_f_w1, hy_f_b1, hy_f_w2, hy_f_b2, hy_f_w3, hy_f_freq, hy_d, da_lq1, da_lk1, da_lq2, da_lk2, da_norm_w, w_branch, w_out, mlp_w1, mlp_w2, final_norm_w):
    depth = w_in.shape[0]
    bc, lc, d = x_prompt.shape
    bl, ll, _ = x_sample.shape
    T = ll
    assert bc * lc == T and d == D_MODEL and bc % 2 == 0 and bl == 2
    G = 1 + bl
    past = cache_mla_ckv.shape[2]
    tm = min(512, T)
    tq = min(256, lc)
    tq_lat, kc_lat = min(512, ll), min(1024, ll)

    pk = _pack_weights(w_in, mla_w_uq, mla_w_ukv, gdn_conv_w, gdn_a_log, gdn_dt_bias, hy_conv_w, hy_conv_b,
                       hy_f_w1, hy_f_b1, hy_f_w2, hy_f_b2, hy_f_w3, hy_f_freq, hy_d)
    wb_bf = w_branch.astype(bf16)
    wo_bf = w_out.astype(bf16)
    w1_bf = mlp_w1.astype(bf16)
    w2_bf = mlp_w2.astype(bf16)
    ca, cb, cd, sd = (jnp.asarray(t) for t in _rope_tables(T))

    cond8 = jnp.concatenate([c_ctx.reshape(1, d), c, jnp.zeros((8 - G, d), f32)], axis=0)
    mod = _modulation(cond8, w_ada, b_ada)

    kpe_pad = jnp.pad(cache_mla_kpe, ((0, 0), (0, 0), (0, 0), (MLA_NOPE, 128 - MLA_NOPE - MLA_ROPE)))
    kc_mla, vc_mla = _cache_kv(cache_mla_ckv, kpe_pad, pk["wka"], pk["wv"])
    kc_da = jnp.transpose(cache_diff_k.reshape(bl, depth, past, DA_H, 2 * DA_DK), (1, 3, 0, 2, 4)).astype(bf16)
    vc_da = jnp.transpose(cache_diff_v, (1, 3, 0, 2, 4))
    vc_da = jnp.concatenate([vc_da, jnp.ones(vc_da.shape[:-1] + (1,), f32),
                             jnp.zeros(vc_da.shape[:-1] + (LANES - DA_DV - 1,), f32)], axis=-1).astype(bf16)
    s0_bd = jnp.einsum('bldhij,hg->bldhigj', state_gdn, jnp.eye(GDN_H, dtype=f32)).reshape(
        bl, depth, 2, GDN_N, GDN_N)

    x = jnp.concatenate([x_prompt.reshape(1, T, d), x_sample], axis=0)
    new_ckv, new_kpe, new_dk, new_dv, new_state = [], [], [], [], []
    for l in range(depth):
        lam_init = 0.8 - 0.6 * math.exp(-0.3 * l)
        (zg, zab, zhT, kpe_raw, dk_raw, dv_raw, q, k, v, ckv, dq1, dq2, dk, dv) = _in_proj(
            x, mod, norm_mix_w[l], pk["wa"], pk["wd"], pk["wg"], pk["wab"], pk["whT"], ca, cb, cd, sd,
            mla_q_norm_w[l], mla_kv_norm_w[l], pk["wqa"], pk["wqb"], pk["wka"], pk["wv"], l, tm)
        new_ckv.append(ckv[0].reshape(bc, lc, MLA_KV_LORA))
        new_kpe.append(kpe_raw[0, :, MLA_NOPE:MLA_NOPE + MLA_ROPE].reshape(bc, lc, MLA_ROPE))
        new_dk.append(dk_raw[0].reshape(bc, lc, DA_H, 2, DA_DK))
        new_dv.append(dv_raw[0].reshape(bc, lc, DA_H, LANES)[..., :DA_DV])

        ctx_view = lambda a: a.reshape(a.shape[0], G * bc, lc, a.shape[3])
        oa_c = _mla_attention(ctx_view(q), ctx_view(k), ctx_view(v), 0, bc, lc, tq, lc)
        oa_l = _mla_attention(q, k, v, 1, bl, ll, tq_lat, kc_lat, cache=(kc_mla[l], vc_mla[l]))

        lamv = jnp.concatenate([da_lq1[l][None], da_lk1[l][None], da_lq2[l][None], da_lk2[l][None],
                                jnp.full((1, DA_DK), lam_init, f32), jnp.zeros((3, DA_DK), f32)], axis=0)
        od_c = _diff_attention(ctx_view(dq1), ctx_view(dq2), ctx_view(dk), ctx_view(dv), lamv, da_norm_w[l],
                               0, bc, lc, tq, lc)
        od_l = _diff_attention(dq1, dq2, dk, dv, lamv, da_norm_w[l], 1, bl, ll, tq_lat, kc_lat,
                               cache=(kc_da[l], vc_da[l]))

        zg_ctx = zg.reshape(G * bc, lc, zg.shape[2])
        zab_ctx = zab.reshape(G * bc, lc, 128)
        ob_c, s_gdn = _gdn(zg_ctx, zab_ctx, gdn_conv_w[l], pk["alog"][l], pk["dtb"][l], gdn_norm_w[l], 0, bc, lc,
                           2)
        ob_l = _gdn(zg, zab, gdn_conv_w[l], pk["alog"][l], pk["dtb"][l], gdn_norm_w[l], 1, bl, ll, 1,
                    s0=s0_bd, layer=l)[0]
        new_state.append(s_gdn)

        hw = {name: val[l] for name, val in pk["hy"].items()}
        oc_c = _hyena(zhT, lambda p: 0, lambda p: 0, lambda p: 2 * p, lambda p: 2 * p + 1, bc // 2, lc, 128, hw)
        oc_l = _hyena(zhT, lambda p: 1, lambda p: 2, lambda p: 0, lambda p: 0, 1, ll, 64, hw)

        x = _merge(x, mod, norm_mix_w[l], ((oa_c, oa_l), (ob_c, ob_l), (oc_c, oc_l), (od_c, od_l)),
                   pk["wgate"], wb_bf, wo_bf, l, tm)
        x = _mlp(x, mod, norm_mlp_w[l], w1_bf, w2_bf, final_norm_w, l, tm, final=l == depth - 1)

    y = x
    y_prompt = y[0].reshape(bc, lc, d)
    y_sample = y[1:]
    return (y_prompt, y_sample, jnp.stack(new_ckv, axis=1), jnp.stack(new_kpe, axis=1), jnp.stack(new_dk, axis=1),
            jnp.stack(new_dv, axis=1), jnp.stack(new_state, axis=1))
```
